```python
import math, functools
import jax, jax.numpy as jnp
from jax import lax
import numpy as np

D_MODEL = 1024
BATCH = 2
SEQ = 8192
DEPTH = 4
DEC_BATCH = 128
DEC_SEQ = 8
PAST_LEN = 2048
PAGE_SIZE = 128

WA = D_MODEL // 4
A_GROUPS = 4
CONV_A = 3
HB = 6
DKB = 64
DVB = 64
WB = HB * DVB
QKV_B = 2 * HB * DKB + WB
GDN_CONV = 4
GDN_CHUNK = 64
HC = 6
DH = 64
WC = HC * DH
HI = 4
DI = 64
TOPK_MAX = 256
Q_BLOCK = 128
ROPE_DIM = DH // 4
ROPE_THETA = 500000.0
D_MIX = WA + WB + WC
N_IN = 3 * WA + QKV_B + WB + 2 * HB + 3 * WC + HI * DI + DI + HI
D_FF = 2816
EPS = 1e-6

kernel_name = 'hymba_style_conv_gdn_dsa_decoder_step'

WEIGHT_NAMES = ('ffn1_norm', 'ffn1_w_gate', 'ffn1_w_up', 'ffn1_w_down', 'mix_norm', 'w_in', 'conv_a_w',
                'gdn_conv_w', 'gdn_a_log', 'gdn_dt_bias', 'gdn_out_norm', 'attn_q_norm', 'attn_k_norm',
                'w_out', 'ffn2_norm', 'ffn2_w_gate', 'ffn2_w_up', 'ffn2_w_down')


def in_proj_split_points():
    sizes = (WA, WA, WA, QKV_B, WB, HB, HB, 3 * WC, HI * DI, DI, HI)
    pts, acc = [], 0
    for s in sizes[:-1]:
        acc += s
        pts.append(acc)
    return pts


def rmsnorm(x, g):
    xf = x.astype(jnp.float32)
    y = xf * lax.rsqrt(jnp.mean(xf * xf, axis=-1, keepdims=True) + EPS)
    return (y * g.astype(jnp.float32)).astype(x.dtype)


def l2norm(x):
    xf = x.astype(jnp.float32)
    return (xf * lax.rsqrt(jnp.sum(xf * xf, axis=-1, keepdims=True) + EPS)).astype(x.dtype)


def swiglu(x, wg, wu, wd):
    return (jax.nn.silu(x @ wg) * (x @ wu)) @ wd


def rope_partial(x, pos):
    half = ROPE_DIM // 2
    inv_freq = ROPE_THETA ** (-jnp.arange(half, dtype=jnp.float32) / half)
    ang = pos.astype(jnp.float32)[:, None] * inv_freq[None, :]
    cos = jnp.cos(ang)[:, None, :]
    sin = jnp.sin(ang)[:, None, :]
    xr = x[..., :ROPE_DIM].astype(jnp.float32)
    x1, x2 = xr[..., :half], xr[..., half:]
    rot = jnp.concatenate([x1 * cos - x2 * sin, x2 * cos + x1 * sin], axis=-1).astype(x.dtype)
    return jnp.concatenate([rot, x[..., ROPE_DIM:]], axis=-1)


def causal_dwconv(x, w, prev):
    width = w.shape[0]
    t = x.shape[1]
    xp = jnp.concatenate([prev.astype(x.dtype), x], axis=1)
    y = xp[:, 0:t] * w[0]
    for j in range(1, width):
        y = y + xp[:, j:j + t] * w[j]
    return y, xp[:, t:]


def gated_delta_chunked(q, k, v, g, beta, s0):
    f32 = jnp.float32
    bn, t, h, dk = q.shape
    dv = v.shape[-1]
    c = min(GDN_CHUNK, t)
    pad = (-t) % c

    def prep(a):
        a = a.astype(f32)
        if pad:
            a = jnp.pad(a, [(0, 0), (0, pad)] + [(0, 0)] * (a.ndim - 2))
        n = a.shape[1] // c
        a = a.reshape((bn, n, c) + a.shape[2:])
        return jnp.transpose(a, (1, 0, 3, 2) + tuple(range(4, a.ndim)))

    incl = jnp.tril(jnp.ones((c, c), dtype=bool))
    strict = jnp.tril(jnp.ones((c, c), dtype=bool), -1)
    eye = jnp.eye(c, dtype=f32)

    def step(s, inp):
        qc, kc, vc, gc, bc = inp
        gcum = jnp.cumsum(gc, axis=-1)
        decay = jnp.exp(jnp.where(incl, gcum[..., :, None] - gcum[..., None, :], -jnp.inf))
        kb = kc * bc[..., None]
        a_mat = jnp.einsum('bhik,bhjk->bhij', kb, kc) * jnp.where(strict, decay, 0.0)
        t_mat = lax.linalg.triangular_solve(eye + a_mat, jnp.broadcast_to(eye, a_mat.shape),
                                            left_side=True, lower=True, unit_diagonal=True)
        u = jnp.einsum('bhij,bhjv->bhiv', t_mat, vc * bc[..., None])
        wk = jnp.einsum('bhij,bhjk->bhik', t_mat, kb * jnp.exp(gcum)[..., None])
        v_new = u - jnp.einsum('bhik,bhkv->bhiv', wk, s)
        qk = jnp.einsum('bhik,bhjk->bhij', qc, kc) * decay
        out = (jnp.einsum('bhik,bhkv->bhiv', qc * jnp.exp(gcum)[..., None], s)
               + jnp.einsum('bhij,bhjv->bhiv', qk, v_new))
        g_last = gcum[..., -1]
        k_dec = kc * jnp.exp(g_last[..., None] - gcum)[..., None]
        s = s * jnp.exp(g_last)[..., None, None] + jnp.einsum('bhik,bhiv->bhkv', k_dec, v_new)
        return s, out

    s, out = lax.scan(step, s0.astype(f32), (prep(q), prep(k), prep(v), prep(g), prep(beta)))
    out = jnp.transpose(out, (1, 0, 3, 2, 4)).reshape(bn, -1, h, dv)[:, :t]
    return out.astype(v.dtype), s.astype(s0.dtype)


def attend_selected(q, kg, vg, valid):
    s = jnp.einsum('bthd,btkhd->bthk', q.astype(jnp.float32), kg.astype(jnp.float32)) * (DH ** -0.5)
    s = jnp.where(valid[:, :, None, :], s, -jnp.inf)
    p = jax.nn.softmax(s, axis=-1).astype(vg.dtype)
    return jnp.einsum('bthk,btkhd->bthd', p, vg)


def indexer_scores(qi, ki, wi):
    rel = jax.nn.relu(jnp.einsum('bthd,bsd->bths', qi.astype(jnp.float32), ki.astype(jnp.float32)))
    return jnp.einsum('bth,bths->bts', wi.astype(jnp.float32), rel)


def sparse_attn_prompt(q, k, v, qi, ki, wi):
    bn, s_len, h, dh = q.shape
    topk = min(TOPK_MAX, s_len // 4)
    nblk = s_len // Q_BLOCK
    bidx = jnp.arange(bn)[:, None, None]

    def blockify(a):
        a = a.reshape((bn, nblk, Q_BLOCK) + a.shape[2:])
        return jnp.moveaxis(a, 1, 0)

    def one_block(args):
        blk, qb, qib, wib = args
        pos_q = blk * Q_BLOCK + jnp.arange(Q_BLOCK)
        score = indexer_scores(qib, ki, wib)
        vis = jnp.arange(s_len)[None, :] <= pos_q[:, None]
        score = jnp.where(vis[None], score, -jnp.inf)
        vals, idx = lax.top_k(score, topk)
        return attend_selected(qb, k[bidx, idx], v[bidx, idx], jnp.isfinite(vals))

    out = lax.map(one_block, (jnp.arange(nblk), blockify(q), blockify(qi), blockify(wi)))
    return jnp.moveaxis(out, 0, 1).reshape(bn, s_len, h * dh)


def sparse_attn_sample(q, k, v, qi, ki, wi, cache_k, cache_v, cache_ki, page_table):
    bn, t, h, dh = q.shape
    page = cache_k.shape[1]
    past = page_table.shape[1] * page
    l_keys = past + t
    topk = min(TOPK_MAX, l_keys // 4)
    ki_all = jnp.concatenate([cache_ki[page_table].reshape(bn, past, DI), ki.astype(cache_ki.dtype)], axis=1)
    score = indexer_scores(qi, ki_all, wi)
    vis = jnp.arange(l_keys)[None, :] <= (past + jnp.arange(t))[:, None]
    score = jnp.where(vis[None], score, -jnp.inf)
    vals, idx = lax.top_k(score, topk)
    bidx = jnp.arange(bn)[:, None, None]
    is_past = (idx < past)[..., None, None]
    pidx = jnp.minimum(idx, past - 1)
    phys = page_table[bidx, pidx // page]
    off = pidx % page
    nidx = jnp.clip(idx - past, 0, t - 1)
    kg = jnp.where(is_past, cache_k[phys, off], k[bidx, nidx].astype(cache_k.dtype))
    vg = jnp.where(is_past, cache_v[phys, off], v[bidx, nidx].astype(cache_v.dtype))
    out = attend_selected(q, kg, vg, jnp.isfinite(vals))
    return out.reshape(bn, t, h * dh).astype(q.dtype)


def trunk_layer(x, pos, conv_a_prev, gdn_conv_prev, gdn_s0, attend, w):
    bn, t, _ = x.shape
    x = x + 0.5 * swiglu(rmsnorm(x, w['ffn1_norm']), w['ffn1_w_gate'], w['ffn1_w_up'], w['ffn1_w_down'])
    h = rmsnorm(x, w['mix_norm'])
    (a_b, a_c, a_x, b_qkv, b_z, b_a, b_b, c_qkv, c_qi, c_ki, c_w) = jnp.split(
        h @ w['w_in'], in_proj_split_points(), axis=-1)
    conv_u, conv_a_new = causal_dwconv(a_c * a_x, w['conv_a_w'], conv_a_prev)
    y_a = a_b * conv_u
    qkv, gdn_conv_new = causal_dwconv(b_qkv, w['gdn_conv_w'], gdn_conv_prev)
    qkv = jax.nn.silu(qkv)
    qb, kb, vb = jnp.split(qkv, [HB * DKB, 2 * HB * DKB], axis=-1)
    qb = l2norm(qb.reshape(bn, t, HB, DKB)) * (DKB ** -0.5)
    kb = l2norm(kb.reshape(bn, t, HB, DKB))
    vb = vb.reshape(bn, t, HB, DVB)
    beta = jax.nn.sigmoid(b_b.astype(jnp.float32))
    g = -jnp.exp(w['gdn_a_log'].astype(jnp.float32)) * jax.nn.softplus(
        b_a.astype(jnp.float32) + w['gdn_dt_bias'].astype(jnp.float32))
    o, s_new = gated_delta_chunked(qb, kb, vb, g, beta, gdn_s0)
    y_b = (rmsnorm(o, w['gdn_out_norm']) * jax.nn.silu(b_z.reshape(bn, t, HB, DVB))).reshape(bn, t, WB)
    qc, kc, vc = jnp.split(c_qkv, 3, axis=-1)
    qc = rope_partial(rmsnorm(qc.reshape(bn, t, HC, DH), w['attn_q_norm']), pos)
    kc = rope_partial(rmsnorm(kc.reshape(bn, t, HC, DH), w['attn_k_norm']), pos)
    vc = vc.reshape(bn, t, HC, DH)
    qi = rope_partial(c_qi.reshape(bn, t, HI, DI), pos)
    ki = rope_partial(c_ki.reshape(bn, t, 1, DI), pos)[:, :, 0]
    wi = c_w * ((HI * DI) ** -0.5)
    y_c = attend(qc, kc, vc, qi, ki, wi)
    x = x + jnp.concatenate([y_a, y_b, y_c], axis=-1) @ w['w_out']
    x = x + 0.5 * swiglu(rmsnorm(x, w['ffn2_norm']), w['ffn2_w_gate'], w['ffn2_w_up'], w['ffn2_w_down'])
    return x, (conv_a_new, gdn_conv_new, s_new, kc, vc, ki)


def setup_inputs(seed: int = 0) -> dict:
    key = jax.random.key(seed)
    keys = iter(jax.random.split(key, 40))
    f32 = jnp.float32

    def nrm(shape, scale):
        return jax.random.normal(next(keys), shape, f32) * scale

    def gain(shape):
        return 1.0 + nrm(shape, 0.02)

    n_pages = PAST_LEN // PAGE_SIZE
    used = DEC_BATCH * n_pages
    n_pool = used + (used + 3) // 4
    page_table = jax.random.permutation(next(keys), n_pool)[:used].reshape(DEC_BATCH, n_pages).astype(jnp.int32)
    a_log = jnp.log(jax.random.uniform(next(keys), (DEPTH, HB), f32, 1.0, 16.0))
    dt = jnp.exp(jax.random.uniform(next(keys), (DEPTH, HB), f32, math.log(1e-3), math.log(1e-1)))
    dt_bias = dt + jnp.log(-jnp.expm1(-dt))
    return {
        'x_prompt': nrm((BATCH, SEQ, D_MODEL), 1.0),
        'x_sample': nrm((DEC_BATCH, DEC_SEQ, D_MODEL), 1.0),
        'state_conv_a': nrm((DEPTH, DEC_BATCH, CONV_A - 1, WA), 1.0),
        'state_gdn_conv': nrm((DEPTH, DEC_BATCH, GDN_CONV - 1, QKV_B), 1.0),
        'state_gdn': nrm((DEPTH, DEC_BATCH, HB, DKB, DVB), 0.3),
        'cache_k': nrm((DEPTH, n_pool, PAGE_SIZE, HC, DH), 1.0),
        'cache_v': nrm((DEPTH, n_pool, PAGE_SIZE, HC, DH), 1.0),
        'cache_kidx': nrm((DEPTH, n_pool, PAGE_SIZE, DI), 1.0),
        'page_table': page_table,
        'ffn1_norm': gain((DEPTH, D_MODEL)),
        'ffn1_w_gate': nrm((DEPTH, D_MODEL, D_FF), D_MODEL ** -0.5),
        'ffn1_w_up': nrm((DEPTH, D_MODEL, D_FF), D_MODEL ** -0.5),
        'ffn1_w_down': nrm((DEPTH, D_FF, D_MODEL), D_FF ** -0.5),
        'mix_norm': gain((DEPTH, D_MODEL)),
        'w_in': nrm((DEPTH, D_MODEL, N_IN), D_MODEL ** -0.5),
        'conv_a_w': nrm((DEPTH, CONV_A, WA), CONV_A ** -0.5),
        'gdn_conv_w': nrm((DEPTH, GDN_CONV, QKV_B), GDN_CONV ** -0.5),
        'gdn_a_log': a_log,
        'gdn_dt_bias': dt_bias,
        'gdn_out_norm': gain((DEPTH, DVB)),
        'attn_q_norm': gain((DEPTH, DH)),
        'attn_k_norm': gain((DEPTH, DH)),
        'w_out': nrm((DEPTH, D_MIX, D_MODEL), D_MIX ** -0.5),
        'ffn2_norm': gain((DEPTH, D_MODEL)),
        'ffn2_w_gate': nrm((DEPTH, D_MODEL, D_FF), D_MODEL ** -0.5),
        'ffn2_w_up': nrm((DEPTH, D_MODEL, D_FF), D_MODEL ** -0.5),
        'ffn2_w_down': nrm((DEPTH, D_FF, D_MODEL), D_FF ** -0.5),
    }


def reference(x_prompt, x_sample, state_conv_a, state_gdn_conv, state_gdn, cache_k, cache_v, cache_kidx,
              page_table, ffn1_norm, ffn1_w_gate, ffn1_w_up, ffn1_w_down, mix_norm, w_in, conv_a_w,
              gdn_conv_w, gdn_a_log, gdn_dt_bias, gdn_out_norm, attn_q_norm, attn_k_norm, w_out,
              ffn2_norm, ffn2_w_gate, ffn2_w_up, ffn2_w_down):
    weights = (ffn1_norm, ffn1_w_gate, ffn1_w_up, ffn1_w_down, mix_norm, w_in, conv_a_w, gdn_conv_w,
               gdn_a_log, gdn_dt_bias, gdn_out_norm, attn_q_norm, attn_k_norm, w_out, ffn2_norm,
               ffn2_w_gate, ffn2_w_up, ffn2_w_down)
    b_p, t_p = x_prompt.shape[0], x_prompt.shape[1]
    t_s = x_sample.shape[1]
    past = page_table.shape[1] * cache_k.shape[2]
    pos_p = jnp.arange(t_p)
    pos_s = past + jnp.arange(t_s)
    yp, ys = x_prompt, x_sample
    outs_p, outs_s = [], []
    for l in range(DEPTH):
        w = {name: arr[l] for name, arr in zip(WEIGHT_NAMES, weights)}
        zero_a = jnp.zeros((b_p, CONV_A - 1, WA), yp.dtype)
        zero_b = jnp.zeros((b_p, GDN_CONV - 1, QKV_B), yp.dtype)
        zero_s = jnp.zeros((b_p, HB, DKB, DVB), jnp.float32)
        yp, st_p = trunk_layer(yp, pos_p, zero_a, zero_b, zero_s, sparse_attn_prompt, w)
        attend_s = functools.partial(sparse_attn_sample, cache_k=cache_k[l], cache_v=cache_v[l],
                                     cache_ki=cache_kidx[l], page_table=page_table)
        ys, st_s = trunk_layer(ys, pos_s, state_conv_a[l], state_gdn_conv[l], state_gdn[l], attend_s, w)
        outs_p.append(st_p)
        outs_s.append(st_s)
    p_conv_a, p_gdn_conv, p_gdn, p_k, p_v, p_kidx = [jnp.stack(z) for z in zip(*outs_p)]
    s_conv_a, s_gdn_conv, s_gdn, s_k, s_v, s_kidx = [jnp.stack(z) for z in zip(*outs_s)]
    return (yp, ys, p_conv_a, p_gdn_conv, p_gdn, p_k, p_v, p_kidx,
            s_conv_a, s_gdn_conv, s_gdn, s_k, s_v, s_kidx)
```

```python
import functools
import math

import jax
import jax.numpy as jnp
from jax import lax
from jax.experimental import pallas as pl
from jax.experimental.pallas import tpu as pltpu

F32 = jnp.float32
BF16 = jnp.bfloat16
I32 = jnp.int32

EPS = 1e-6
LANES = 128
SUBLANES = 8
VMEM_LIMIT = 56 * 1024 * 1024

A_GROUPS = 4
CONV_A = 3
HB = 6
DKB = 64
DVB = 64
GDN_CONV = 4
GDN_ROWS = 64
HC = 6
DH = 64
HI = 4
DI = 64
TOPK_MAX = 256
ROPE_DIM = DH // 4
ROPE_THETA = 500000.0
INT_MIN = -2 ** 31
NEG_BIG = -1e30

WA = 256
QKV_B = 2 * HB * DKB + HB * DVB
WB = HB * DVB
WC = HC * DH
OFF_BQKV = 0
OFF_CQKV = QKV_B
OFF_A = 2 * QKV_B
OFF_BZ = OFF_A + 3 * WA
OFF_CI = OFF_BZ + WB
N_P = OFF_CI + 384
LANE_W = DI
LANE_GA = DI + HI
LANE_GB = DI + HI + HB


def _pick(n, cands):
    for c in cands:
        if n % c == 0:
            return c
    raise ValueError(f"no tile for {n}")


def _cparams(sem):
    return pltpu.CompilerParams(dimension_semantics=sem, vmem_limit_bytes=VMEM_LIMIT)


def _dot(a, b):
    return jnp.dot(a, b, preferred_element_type=F32)


def _dot_nt(a, b):
    return lax.dot_general(a, b, (((1,), (1,)), ((), ())), preferred_element_type=F32)


def _dot_hi(a, b):
    return jnp.dot(a, b, preferred_element_type=F32, precision=lax.Precision.HIGHEST)


def _dot_nt_hi(a, b):
    return lax.dot_general(a, b, (((1,), (1,)), ((), ())), preferred_element_type=F32,
                           precision=lax.Precision.HIGHEST)


def _segsum(x, bd):
    hi = x.astype(BF16)
    lo = (x - hi.astype(F32)).astype(BF16)
    return _dot(hi, bd) + _dot(lo, bd)


def _sigmoid(x):
    return 1.0 / (1.0 + jnp.exp(-x))


def _silu(x):
    return x * _sigmoid(x)


def _softplus(x):
    return jnp.maximum(x, 0.0) + jnp.log(1.0 + jnp.exp(-jnp.abs(x)))


def _ffn_body(x_ref, g_ref, wg_ref, wu_ref, wd_ref, o_ref, h_ref, acc_ref):
    j = pl.program_id(1)

    @pl.when(j == 0)
    def _():
        x = x_ref[...]
        ms = jnp.mean(x * x, axis=-1, keepdims=True)
        h_ref[...] = (x * lax.rsqrt(ms + EPS) * g_ref[...]).astype(BF16)
        acc_ref[...] = jnp.zeros_like(acc_ref)

    h = h_ref[...]
    a = _dot(h, wg_ref[...])
    b = _dot(h, wu_ref[...])
    t = (_silu(a) * b).astype(BF16)
    acc_ref[...] += _dot(t, wd_ref[...])

    @pl.when(j == pl.num_programs(1) - 1)
    def _():
        o_ref[...] = x_ref[...] + 0.5 * acc_ref[...]


def _ffn(x, g, wg, wu, wd):
    m, d = x.shape
    f = wg.shape[1]
    tm = _pick(m, (512, 256, 128, 64, 32, 16, 8))
    tf = _pick(f, (256, 128))
    return pl.pallas_call(
        _ffn_body,
        grid=(m // tm, f // tf),
        in_specs=[pl.BlockSpec((tm, d), lambda i, j: (i, 0)),
                  pl.BlockSpec((1, d), lambda i, j: (0, 0)),
                  pl.BlockSpec((d, tf), lambda i, j: (0, j)),
                  pl.BlockSpec((d, tf), lambda i, j: (0, j)),
                  pl.BlockSpec((tf, d), lambda i, j: (j, 0))],
        out_specs=pl.BlockSpec((tm, d), lambda i, j: (i, 0)),
        out_shape=jax.ShapeDtypeStruct((m, d), F32),
        scratch_shapes=[pltpu.VMEM((tm, d), BF16), pltpu.VMEM((tm, d), F32)],
        compiler_params=_cparams(("parallel", "arbitrary")),
    )(x, g, wg, wu, wd)


def _inproj_body(x_ref, g_ref, w_ref, o_ref, h_ref):
    @pl.when(pl.program_id(1) == 0)
    def _():
        x = x_ref[...]
        ms = jnp.mean(x * x, axis=-1, keepdims=True)
        h_ref[...] = (x * lax.rsqrt(ms + EPS) * g_ref[...]).astype(BF16)

    o_ref[...] = _dot(h_ref[...], w_ref[...])


def _inproj(x, g, w):
    m, d = x.shape
    n = w.shape[1]
    tm = _pick(m, (512, 256, 128, 64, 32, 16, 8))
    tn = _pick(n, (768, 384, 128))
    return pl.pallas_call(
        _inproj_body,
        grid=(m // tm, n // tn),
        in_specs=[pl.BlockSpec((tm, d), lambda i, j: (i, 0)),
                  pl.BlockSpec((1, d), lambda i, j: (0, 0)),
                  pl.BlockSpec((d, tn), lambda i, j: (0, j))],
        out_specs=pl.BlockSpec((tm, tn), lambda i, j: (i, j)),
        out_shape=jax.ShapeDtypeStruct((m, n), F32),
        scratch_shapes=[pltpu.VMEM((tm, d), BF16)],
        compiler_params=_cparams(("parallel", "arbitrary")),
    )(x, g, w)


def _outproj_body(x_ref, ya_ref, yb_ref, yc_ref, w_ref, o_ref):
    acc = _dot(ya_ref[...].astype(BF16), w_ref[0:WA, :])
    acc += _dot(yb_ref[...].astype(BF16), w_ref[WA:WA + WB, :])
    acc += _dot(yc_ref[...].astype(BF16), w_ref[WA + WB:WA + WB + WC, :])
    o_ref[...] = x_ref[...] + acc


def _outproj(x, ya, yb, yc, w):
    m, d = x.shape
    tm = _pick(m, (512, 256, 128, 64, 32, 16, 8))
    return pl.pallas_call(
        _outproj_body,
        grid=(m // tm,),
        in_specs=[pl.BlockSpec((tm, d), lambda i: (i, 0)),
                  pl.BlockSpec((tm, WA), lambda i: (i, 0)),
                  pl.BlockSpec((tm, WB), lambda i: (i, 0)),
                  pl.BlockSpec((tm, WC), lambda i: (i, 0)),
                  pl.BlockSpec(w.shape, lambda i: (0, 0))],
        out_specs=pl.BlockSpec((tm, d), lambda i: (i, 0)),
        out_shape=jax.ShapeDtypeStruct((m, d), F32),
        compiler_params=_cparams(("parallel",)),
    )(x, ya, yb, yc, w)


def _conv_taps(u, w_ref, prev_ref, ext_ref, tt, width):
    it = pl.program_id(1)

    @pl.when(it == 0)
    def _():
        ext_ref[:, 0:SUBLANES, :] = prev_ref[...]

    @pl.when(it > 0)
    def _():
        ext_ref[:, 0:SUBLANES, :] = ext_ref[:, tt:tt + SUBLANES, :]

    ext_ref[:, SUBLANES:SUBLANES + tt, :] = u
    acc = None
    for s in range(width):
        term = ext_ref[:, SUBLANES - s:SUBLANES - s + tt, :] * w_ref[width - 1 - s:width - s, :]
        acc = term if acc is None else acc + term
    return acc, ext_ref[:, tt:tt + SUBLANES, :]


def _group_tiles(bsz, t):
    if t >= 128:
        return 1, _pick(t, (256, 128))
    return _pick(bsz, (16, 8, 4, 2, 1)), t


def _mixa_body(p_ref, w_ref, prev_ref, y_ref, st_ref, ext_ref, *, bb, tt):
    pa = p_ref[...]
    u = (pa[:, WA:2 * WA] * pa[:, 2 * WA:3 * WA]).reshape(bb, tt, WA)
    conv, tail = _conv_taps(u, w_ref, prev_ref, ext_ref, tt, CONV_A)
    y_ref[...] = pa[:, 0:WA] * conv.reshape(bb * tt, WA)
    st_ref[...] = tail


def _mixer_a(p, row0, bsz, t, w, prev8):
    bb, tt = _group_tiles(bsz, t)
    r = bb * tt
    nb, nt = bsz // bb, t // tt
    base = row0 // r
    assert row0 % r == 0
    return pl.pallas_call(
        functools.partial(_mixa_body, bb=bb, tt=tt),
        grid=(nb, nt),
        in_specs=[pl.BlockSpec((r, 3 * WA), lambda i, j: (base + i * nt + j, OFF_A // (3 * WA))),
                  pl.BlockSpec(w.shape, lambda i, j: (0, 0)),
                  pl.BlockSpec((bb, SUBLANES, WA), lambda i, j: (i, 0, 0))],
        out_specs=[pl.BlockSpec((r, WA), lambda i, j: (i * nt + j, 0)),
                   pl.BlockSpec((bb, SUBLANES, WA), lambda i, j: (i, 0, 0))],
        out_shape=[jax.ShapeDtypeStruct((bsz * t, WA), F32),
                   jax.ShapeDtypeStruct((bsz, SUBLANES, WA), F32)],
        scratch_shapes=[pltpu.VMEM((bb, tt + SUBLANES, WA), F32)],
        compiler_params=_cparams(("parallel", "arbitrary")),
    )(p, w, prev8)


def _gdnprep_body(p_ref, ci_ref, w_ref, prev_ref, bd_ref, avec_ref, dtb_ref,
                  q_ref, k_ref, v_ref, gb_ref, st_ref, ext_ref, *, bb, tt):
    r = bb * tt
    x = p_ref[...].reshape(bb, tt, QKV_B)
    conv, tail = _conv_taps(x, w_ref, prev_ref, ext_ref, tt, GDN_CONV)
    st_ref[...] = tail
    a = _silu(conv.reshape(r, QKV_B))
    bd = bd_ref[...]
    nq = HB * DKB // LANES
    for c in range(2 * nq):
        xc = a[:, c * LANES:(c + 1) * LANES]
        n = xc * lax.rsqrt(_segsum(xc * xc, bd) + EPS)
        if c < nq:
            q_ref[:, c * LANES:(c + 1) * LANES] = n * (DKB ** -0.5)
        else:
            k_ref[:, (c - nq) * LANES:(c - nq + 1) * LANES] = n
    v_ref[...] = a[:, 2 * HB * DKB:]
    raw = ci_ref[...]
    g = -avec_ref[...] * _softplus(raw + dtb_ref[...])
    lane = lax.broadcasted_iota(I32, raw.shape, 1)
    gb_ref[...] = jnp.where((lane >= LANE_GB) & (lane < LANE_GB + HB), _sigmoid(raw), g)


def _gdn_prep(p, row0, bsz, t, w, prev8, bd, avec, dtb):
    bb, tt = _group_tiles(bsz, t)
    r = bb * tt
    nb, nt = bsz // bb, t // tt
    base = row0 // r
    assert row0 % r == 0
    m = bsz * t
    rows = lambda i, j: (i * nt + j, 0)
    return pl.pallas_call(
        functools.partial(_gdnprep_body, bb=bb, tt=tt),
        grid=(nb, nt),
        in_specs=[pl.BlockSpec((r, QKV_B), lambda i, j: (base + i * nt + j, OFF_BQKV // QKV_B)),
                  pl.BlockSpec((r, LANES), lambda i, j: (base + i * nt + j, (OFF_CI + 2 * LANES) // LANES)),
                  pl.BlockSpec(w.shape, lambda i, j: (0, 0)),
                  pl.BlockSpec((bb, SUBLANES, QKV_B), lambda i, j: (i, 0, 0)),
                  pl.BlockSpec((LANES, LANES), lambda i, j: (0, 0)),
                  pl.BlockSpec((1, LANES), lambda i, j: (0, 0)),
                  pl.BlockSpec((1, LANES), lambda i, j: (0, 0))],
        out_specs=[pl.BlockSpec((r, HB * DKB), rows),
                   pl.BlockSpec((r, HB * DKB), rows),
                   pl.BlockSpec((r, WB), rows),
                   pl.BlockSpec((r, LANES), rows),
                   pl.BlockSpec((bb, SUBLANES, QKV_B), lambda i, j: (i, 0, 0))],
        out_shape=[jax.ShapeDtypeStruct((m, HB * DKB), F32),
                   jax.ShapeDtypeStruct((m, HB * DKB), F32),
                   jax.ShapeDtypeStruct((m, WB), F32),
                   jax.ShapeDtypeStruct((m, LANES), F32),
                   jax.ShapeDtypeStruct((bsz, SUBLANES, QKV_B), F32)],
        scratch_shapes=[pltpu.VMEM((bb, tt + SUBLANES, QKV_B), F32)],
        compiler_params=_cparams(("parallel", "arbitrary")),
    )(p, p, w, prev8, bd, avec, dtb)


def _gdn_body(q_ref, k_ref, v_ref, gb_ref, z_ref, s0_ref, gn_ref, y_ref, sout_ref, s_scr, *, c, g):
    r = GDN_ROWS
    it = pl.program_id(1)

    @pl.when(it == 0)
    def _():
        s_scr[...] = s0_ref[...]

    ri = lax.broadcasted_iota(I32, (r, r), 0)
    ci = lax.broadcasted_iota(I32, (r, r), 1)
    same = (ri // c) == (ci // c)
    incl = same & (ci <= ri)
    strict = same & (ci < ri)
    eye = (ri == ci).astype(F32)
    lastsel = (ci == (ri // c) * c + (c - 1)).astype(F32)
    gb = gb_ref[...]
    gcum = _dot_hi(incl.astype(F32), gb)
    glast = _dot_hi(lastsel, gcum)
    lane = lax.broadcasted_iota(I32, (r, LANES), 1)
    if g > 1:
        er = lax.broadcasted_iota(I32, (r, g * DKB), 0)
        ec = lax.broadcasted_iota(I32, (r, g * DKB), 1)
        emask = (er // c) == (ec // DKB)
        tr = lax.broadcasted_iota(I32, (DKB, g * DKB), 0)
        tc = lax.broadcasted_iota(I32, (DKB, g * DKB), 1)
        tile_mat = ((tc % DKB) == tr).astype(BF16)
        xr = lax.broadcasted_iota(I32, (g * DKB, r), 0)
        xc = lax.broadcasted_iota(I32, (g * DKB, r), 1)
        emask_t = (xr // DKB) == (xc // c)
        sr = lax.broadcasted_iota(I32, (g * DKB, DKB), 0)
        sc = lax.broadcasted_iota(I32, (g * DKB, DKB), 1)
        tile_mat_t = ((sr % DKB) == sc).astype(BF16)
        esel = (xc == (xr // DKB) * c + (c - 1)).astype(F32)
        gend = _dot_hi(esel, gcum)
    eye_k = (lax.broadcasted_iota(I32, (DKB, DKB), 0) == lax.broadcasted_iota(I32, (DKB, DKB), 1)).astype(BF16)
    n_sq = max(0, int(math.ceil(math.log2(c))) - 1)

    for h in range(HB):
        q = q_ref[:, h * DKB:(h + 1) * DKB]
        k = k_ref[:, h * DKB:(h + 1) * DKB]
        v = v_ref[:, h * DVB:(h + 1) * DVB]
        gcol = gcum[:, LANE_GA + h:LANE_GA + h + 1]
        glc = glast[:, LANE_GA + h:LANE_GA + h + 1]
        beta = gb[:, LANE_GB + h:LANE_GB + h + 1]
        onehot = (lane == LANE_GA + h).astype(F32)
        grow = _dot_nt_hi(onehot, gcum)
        decay = jnp.exp(jnp.where(incl, gcol - grow, -jnp.inf))
        kb = k * beta
        a_mat = _dot_nt(kb, k) * jnp.where(strict, decay, 0.0)
        pw = -a_mat
        t_mat = eye + pw
        for _ in range(n_sq):
            pw = _dot_hi(pw, pw)
            t_mat = t_mat + _dot_hi(t_mat, pw)
        u = _dot(t_mat, v * beta)
        wk = _dot(t_mat, kb * jnp.exp(gcol))
        qg = q * jnp.exp(gcol)
        k_dec = k * jnp.exp(glc - gcol)
        s = s_scr[h]
        if g > 1:
            wk_e = jnp.where(emask, _dot(wk.astype(BF16), tile_mat), 0.0)
            qg_e = jnp.where(emask, _dot(qg.astype(BF16), tile_mat), 0.0)
            kd_t = jnp.where(emask_t, _dot_nt(tile_mat_t, k_dec.astype(BF16)), 0.0)
            srow = jnp.exp(gend[:, LANE_GA + h:LANE_GA + h + 1])
        else:
            wk_e, qg_e = wk, qg
            kd_t = _dot_nt(eye_k, k_dec.astype(BF16))
            srow = jnp.exp(gcum[r - 1:r, LANE_GA + h:LANE_GA + h + 1])
        v_new = u - _dot(wk_e, s)
        qk = _dot_nt(q, k) * decay
        out = _dot(qg_e, s) + _dot(qk, v_new)
        s_scr[h] = s * srow + _dot(kd_t, v_new)
        ms = jnp.mean(out * out, axis=-1, keepdims=True)
        zz = z_ref[:, h * DVB:(h + 1) * DVB]
        y_ref[:, h * DVB:(h + 1) * DVB] = out * lax.rsqrt(ms + EPS) * gn_ref[...] * _silu(zz)

    @pl.when(it == pl.num_programs(1) - 1)
    def _():
        sout_ref[...] = s_scr[...]


def _gdn(q, k, v, gb, p, row0, bsz, t, s0, gn):
    c = min(GDN_ROWS, t)
    g = GDN_ROWS // c
    assert t % c == 0 and bsz % g == 0 and row0 % GDN_ROWS == 0
    nb, nt = bsz // g, t // c
    base = row0 // GDN_ROWS
    rows = lambda i, j: (i * nt + j, 0)
    s0r = s0.reshape(nb, g, HB, DKB, DVB).transpose(0, 2, 1, 3, 4).reshape(nb, HB, g * DKB, DVB)
    y, s_out = pl.pallas_call(
        functools.partial(_gdn_body, c=c, g=g),
        grid=(nb, nt),
        in_specs=[pl.BlockSpec((GDN_ROWS, HB * DKB), rows),
                  pl.BlockSpec((GDN_ROWS, HB * DKB), rows),
                  pl.BlockSpec((GDN_ROWS, WB), rows),
                  pl.BlockSpec((GDN_ROWS, LANES), rows),
                  pl.BlockSpec((GDN_ROWS, WB), lambda i, j: (base + i * nt + j, OFF_BZ // WB)),
                  pl.BlockSpec((None, HB, g * DKB, DVB), lambda i, j: (i, 0, 0, 0)),
                  pl.BlockSpec((1, DVB), lambda i, j: (0, 0))],
        out_specs=[pl.BlockSpec((GDN_ROWS, WB), rows),
                   pl.BlockSpec((None, HB, g * DKB, DVB), lambda i, j: (i, 0, 0, 0))],
        out_shape=[jax.ShapeDtypeStruct((bsz * t, WB), F32),
                   jax.ShapeDtypeStruct((nb, HB, g * DKB, DVB), F32)],
        scratch_shapes=[pltpu.VMEM((HB, g * DKB, DVB), F32)],
        compiler_params=_cparams(("parallel", "arbitrary")),
    )(q, k, v, gb, p, s0r, gn)
    s_out = s_out.reshape(nb, HB, g, DKB, DVB).transpose(0, 2, 1, 3, 4).reshape(bsz, HB, DKB, DVB)
    return y, s_out


def _rope(x, cos, s1, s2):
    half = ROPE_DIM // 2
    return x * cos + pltpu.roll(x, LANES - half, 1) * s1 + pltpu.roll(x, half, 1) * s2


def _attnprep_body(p_ref, ci_ref, cos_ref, s1_ref, s2_ref, bd_ref, qn_ref, kn_ref,
                   q_ref, k_ref, v_ref, qi_ref, kiw_ref):
    cos, s1, s2 = cos_ref[...], s1_ref[...], s2_ref[...]
    bd = bd_ref[...]
    nch = WC // LANES
    for c in range(2 * nch):
        x = p_ref[:, c * LANES:(c + 1) * LANES]
        gain = (qn_ref if c < nch else kn_ref)[...]
        n = x * lax.rsqrt(_segsum(x * x, bd) * (1.0 / DH) + EPS) * gain
        rot = _rope(n, cos, s1, s2)
        if c < nch:
            q_ref[:, c * LANES:(c + 1) * LANES] = (rot * (DH ** -0.5)).astype(BF16)
        else:
            k_ref[:, (c - nch) * LANES:(c - nch + 1) * LANES] = rot
    v_ref[...] = p_ref[:, 2 * WC:3 * WC]
    for c in range(HI * DI // LANES):
        x = ci_ref[:, c * LANES:(c + 1) * LANES]
        qi_ref[:, c * LANES:(c + 1) * LANES] = _rope(x, cos, s1, s2).astype(BF16)
    x = ci_ref[:, HI * DI:HI * DI + LANES]
    lane = lax.broadcasted_iota(I32, x.shape, 1)
    is_ki = lane < DI
    kiw_ref[...] = _rope(x, jnp.where(is_ki, cos, 1.0), jnp.where(is_ki, s1, 0.0), jnp.where(is_ki, s2, 0.0))


def _attn_prep(p, row0, m, r, tabs, bd, qn, kn):
    nper = tabs[0].shape[0] // r
    tmap = lambda i: (i % nper, 0)
    assert row0 % r == 0 and m % r == 0 and tabs[0].shape[0] % r == 0
    base = row0 // r
    rows = lambda i: (i, 0)
    return pl.pallas_call(
        _attnprep_body,
        grid=(m // r,),
        in_specs=[pl.BlockSpec((r, 3 * WC), lambda i: (base + i, OFF_CQKV // (3 * WC))),
                  pl.BlockSpec((r, 384), lambda i: (base + i, OFF_CI // 384)),
                  pl.BlockSpec((r, LANES), tmap), pl.BlockSpec((r, LANES), tmap), pl.BlockSpec((r, LANES), tmap),
                  pl.BlockSpec((LANES, LANES), lambda i: (0, 0)),
                  pl.BlockSpec((1, LANES), lambda i: (0, 0)),
                  pl.BlockSpec((1, LANES), lambda i: (0, 0))],
        out_specs=[pl.BlockSpec((r, WC), rows), pl.BlockSpec((r, WC), rows), pl.BlockSpec((r, WC), rows),
                   pl.BlockSpec((r, HI * DI), rows), pl.BlockSpec((r, LANES), rows)],
        out_shape=[jax.ShapeDtypeStruct((m, WC), BF16), jax.ShapeDtypeStruct((m, WC), F32),
                   jax.ShapeDtypeStruct((m, WC), F32), jax.ShapeDtypeStruct((m, HI * DI), BF16),
                   jax.ShapeDtypeStruct((m, LANES), F32)],
        compiler_params=_cparams(("parallel",)),
    )(p, p, *tabs, bd, qn, kn)


def _score_keys(qi_heads, wi, kib, visible):
    sc = None
    for h in range(HI):
        term = wi[:, h:h + 1] * jnp.maximum(_dot_nt(qi_heads[h], kib), 0.0)
        sc = term if sc is None else sc + term
    sc = jnp.where(sc == 0.0, 0.0, sc)
    bits = lax.bitcast_convert_type(sc, I32)
    key = bits ^ ((bits >> 31) & 0x7FFFFFFF)
    return jnp.where(visible, key, INT_MIN)


def _kth_largest_key(count_ge, rows, topk):
    base = jnp.where(count_ge(jnp.zeros((rows, 1), I32)) >= topk, 0, INT_MIN).astype(I32)

    def bit_step(i, base):
        cand = base + (jnp.int32(1) << (30 - i))
        return jnp.where(count_ge(cand) >= topk, cand, base)

    return lax.fori_loop(0, 31, bit_step, base)


def _select_tile(key, vstar, need, running, tri, visible):
    eq = key == vstar
    pref = _dot(eq.astype(BF16), tri)
    sel = ((key > vstar) | (eq & (running + pref <= need))) & visible
    return sel, running + pref[:, LANES - 1:LANES]


def _attn_prompt_body(q_ref, qi_ref, kiw_ref, k_ref, v_ref, ki_ref, tri_ref, o_ref,
                      key_scr, acc_scr, m_scr, l_scr, *, tq, topk):
    qblk = pl.program_id(1)
    nkb = (qblk * tq) // LANES + tq // LANES
    row_pos = qblk * tq + lax.broadcasted_iota(I32, (tq, 1), 0)
    col0 = lax.broadcasted_iota(I32, (1, LANES), 1)
    wi = kiw_ref[:, LANE_W:LANE_W + HI] * ((HI * DI) ** -0.5)
    qi = qi_ref[...]
    qi_heads = [qi[:, h * DI:(h + 1) * DI] for h in range(HI)]

    def fill(j, carry):
        kib = ki_ref[pl.ds(pl.multiple_of(j * LANES, LANES), LANES), :]
        key_scr[j] = _score_keys(qi_heads, wi, kib, (col0 + j * LANES) <= row_pos)
        return carry

    lax.fori_loop(0, nkb, fill, 0)

    def count_ge(cand):
        def body(j, acc):
            return acc + jnp.where(key_scr[j] >= cand, 1.0, 0.0)
        acc = lax.fori_loop(0, nkb, body, jnp.zeros((tq, LANES), F32))
        return jnp.sum(acc, axis=1, keepdims=True)

    vstar = _kth_largest_key(count_ge, tq, topk)
    need = topk - count_ge(vstar + 1)

    lane = lax.broadcasted_iota(I32, (tq, LANES), 1)
    q = q_ref[...]
    zero = jnp.zeros((), BF16)
    q_heads = []
    for h in range(HC):
        qc = q[:, (h // 2) * LANES:(h // 2 + 1) * LANES]
        q_heads.append(jnp.where((lane // DH) == (h % 2), qc, zero))
    m_scr[...] = jnp.full_like(m_scr, NEG_BIG)
    l_scr[...] = jnp.zeros_like(l_scr)
    acc_scr[...] = jnp.zeros_like(acc_scr)
    tri = tri_ref[...]

    def sweep(j, running):
        visible = (col0 + j * LANES) <= row_pos
        sel, running = _select_tile(key_scr[j], vstar, need, running, tri, visible)
        off = pl.multiple_of(j * LANES, LANES)
        kb = k_ref[pl.ds(off, LANES), :]
        vb = v_ref[pl.ds(off, LANES), :]
        for h in range(HC):
            c = h // 2
            s = jnp.where(sel, _dot_nt(q_heads[h], kb[:, c * LANES:(c + 1) * LANES]), -jnp.inf)
            m_old = m_scr[h]
            m_new = jnp.maximum(m_old, jnp.max(s, axis=1, keepdims=True))
            p = jnp.exp(s - m_new)
            alpha = jnp.exp(m_old - m_new)
            l_scr[h] = alpha * l_scr[h] + jnp.sum(p, axis=1, keepdims=True)
            acc_scr[h] = alpha * acc_scr[h] + _dot(p.astype(BF16), vb[:, c * LANES:(c + 1) * LANES])
            m_scr[h] = m_new
        return running

    lax.fori_loop(0, nkb, sweep, jnp.zeros((tq, 1), F32))
    for c in range(HC // 2):
        lo = acc_scr[2 * c] / l_scr[2 * c]
        hi = acc_scr[2 * c + 1] / l_scr[2 * c + 1]
        o_ref[:, c * LANES:(c + 1) * LANES] = jnp.where(lane < DH, lo, hi)


def _attn_prompt(q, qi, kiw, kbf, vbf, kibf, tri, bsz, s_len):
    tq = LANES
    nq = s_len // tq
    topk = min(TOPK_MAX, s_len // 4)
    k3 = kbf.reshape(bsz, s_len, WC)
    v3 = vbf.reshape(bsz, s_len, WC)
    ki3 = kibf.reshape(bsz, s_len, DI)
    rows = lambda b, i: (b * nq + i, 0)
    return pl.pallas_call(
        functools.partial(_attn_prompt_body, tq=tq, topk=topk),
        grid=(bsz, nq),
        in_specs=[pl.BlockSpec((tq, WC), rows),
                  pl.BlockSpec((tq, HI * DI), rows),
                  pl.BlockSpec((tq, LANES), rows),
                  pl.BlockSpec((None, s_len, WC), lambda b, i: (b, 0, 0)),
                  pl.BlockSpec((None, s_len, WC), lambda b, i: (b, 0, 0)),
                  pl.BlockSpec((None, s_len, DI), lambda b, i: (b, 0, 0)),
                  pl.BlockSpec((LANES, LANES), lambda b, i: (0, 0))],
        out_specs=pl.BlockSpec((tq, WC), rows),
        out_shape=jax.ShapeDtypeStruct((bsz * s_len, WC), F32),
        scratch_shapes=[pltpu.VMEM((s_len // LANES, tq, LANES), I32),
                        pltpu.VMEM((HC, tq, LANES), F32),
                        pltpu.VMEM((HC, tq, 1), F32),
                        pltpu.VMEM((HC, tq, 1), F32)],
        compiler_params=_cparams(("parallel", "arbitrary")),
    )(q, qi, kiw, k3, v3, ki3, tri)


def _idx_sample_body(pt_ref, qi_ref, kiw_ref, page_ref, tri_ref, sel_ref, key_scr, knew_scr, *, t, npg, topk):
    p = pl.program_id(1)
    wi = kiw_ref[:, LANE_W:LANE_W + HI] * ((HI * DI) ** -0.5)
    qi = qi_ref[...]
    qi_heads = [qi[:, h * DI:(h + 1) * DI] for h in range(HI)]

    @pl.when(p < npg)
    def _():
        kib = page_ref[...].astype(BF16)
        key_scr[p] = _score_keys(qi_heads, wi, kib, jnp.ones((t, LANES), jnp.bool_))

    @pl.when(p == npg)
    def _():
        knew_scr[...] = jnp.zeros_like(knew_scr)
        knew_scr[0:t, :] = kiw_ref[:, 0:DI]
        row = lax.broadcasted_iota(I32, (t, LANES), 0)
        col = lax.broadcasted_iota(I32, (t, LANES), 1)
        vis_new = col <= row
        key_scr[npg] = _score_keys(qi_heads, wi, knew_scr[...].astype(BF16), vis_new)

        def count_ge(cand):
            acc = jnp.zeros((t, LANES), F32)
            for j in range(npg + 1):
                acc = acc + jnp.where(key_scr[j] >= cand, 1.0, 0.0)
            return jnp.sum(acc, axis=1, keepdims=True)

        vstar = _kth_largest_key(count_ge, t, topk)
        need = topk - count_ge(vstar + 1)
        running = jnp.zeros((t, 1), F32)
        tri = tri_ref[...]
        for j in range(npg + 1):
            vis = vis_new if j == npg else jnp.ones((t, LANES), jnp.bool_)
            sel, running = _select_tile(key_scr[j], vstar, need, running, tri, vis)
            sel_ref[j] = jnp.where(sel, 0.0, -jnp.inf)


def _idx_sample(page_table, qi, kiw, cache_kidx, layer, tri, row0, bsz, t):
    npg = page_table.shape[1]
    page = cache_kidx.shape[2]
    assert page == LANES and row0 % t == 0
    topk = min(TOPK_MAX, (npg * page + t) // 4)
    base = row0 // t
    grid_spec = pltpu.PrefetchScalarGridSpec(
        num_scalar_prefetch=1,
        grid=(bsz, npg + 1),
        in_specs=[pl.BlockSpec((t, HI * DI), lambda b, p, pt: (base + b, 0)),
                  pl.BlockSpec((t, LANES), lambda b, p, pt: (base + b, 0)),
                  pl.BlockSpec((None, None, page, DI),
                               lambda b, p, pt: (layer, pt[b, jnp.minimum(p, npg - 1)], 0, 0)),
                  pl.BlockSpec((LANES, LANES), lambda b, p, pt: (0, 0))],
        out_specs=pl.BlockSpec((None, npg + 1, t, LANES), lambda b, p, pt: (b, 0, 0, 0)),
        scratch_shapes=[pltpu.VMEM((npg + 1, t, LANES), I32), pltpu.VMEM((LANES, DI), F32)])
    return pl.pallas_call(
        functools.partial(_idx_sample_body, t=t, npg=npg, topk=topk),
        grid_spec=grid_spec,
        out_shape=jax.ShapeDtypeStruct((bsz, npg + 1, t, LANES), F32),
        compiler_params=_cparams(("parallel", "arbitrary")),
    )(page_table, qi, kiw, cache_kidx, tri)


def _attn_sample_body(pt_ref, q_ref, sel_ref, kpage_ref, vpage_ref, knew_ref, vnew_ref, o_ref,
                      kn_scr, vn_scr, acc_scr, m_scr, l_scr, *, t, npg):
    p = pl.program_id(1)
    lane = lax.broadcasted_iota(I32, (t, LANES), 1)

    @pl.when(p == 0)
    def _():
        m_scr[...] = jnp.full_like(m_scr, NEG_BIG)
        l_scr[...] = jnp.zeros_like(l_scr)
        acc_scr[...] = jnp.zeros_like(acc_scr)

    def step(kb, vb):
        bias = sel_ref[...]
        q = q_ref[...]
        zero = jnp.zeros((), BF16)
        for h in range(HC):
            c = h // 2
            qh = jnp.where((lane // DH) == (h % 2), q[:, c * LANES:(c + 1) * LANES], zero)
            s = _dot_nt(qh, kb[:, c * LANES:(c + 1) * LANES]) + bias
            m_old = m_scr[h]
            m_new = jnp.maximum(m_old, jnp.max(s, axis=1, keepdims=True))
            pr = jnp.exp(s - m_new)
            alpha = jnp.exp(m_old - m_new)
            l_scr[h] = alpha * l_scr[h] + jnp.sum(pr, axis=1, keepdims=True)
            acc_scr[h] = alpha * acc_scr[h] + _dot(pr.astype(BF16), vb[:, c * LANES:(c + 1) * LANES])
            m_scr[h] = m_new

    @pl.when(p < npg)
    def _():
        step(kpage_ref[...].astype(BF16), vpage_ref[...].astype(BF16))

    @pl.when(p == npg)
    def _():
        kn_scr[...] = jnp.zeros_like(kn_scr)
        vn_scr[...] = jnp.zeros_like(vn_scr)
        kn_scr[0:t, :] = knew_ref[...]
        vn_scr[0:t, :] = vnew_ref[...]
        step(kn_scr[...].astype(BF16), vn_scr[...].astype(BF16))
        for c in range(HC // 2):
            lo = acc_scr[2 * c] / l_scr[2 * c]
            hi = acc_scr[2 * c + 1] / l_scr[2 * c + 1]
            o_ref[:, c * LANES:(c + 1) * LANES] = jnp.where(lane < DH, lo, hi)


def _attn_sample(page_table, q, sel, cache_k, cache_v, knew, vnew, layer, row0, bsz, t):
    npg = page_table.shape[1]
    page = cache_k.shape[2]
    base = row0 // t
    pmap = lambda b, p, pt: (layer, pt[b, jnp.minimum(p, npg - 1)], 0, 0)
    grid_spec = pltpu.PrefetchScalarGridSpec(
        num_scalar_prefetch=1,
        grid=(bsz, npg + 1),
        in_specs=[pl.BlockSpec((t, WC), lambda b, p, pt: (base + b, 0)),
                  pl.BlockSpec((None, None, t, LANES), lambda b, p, pt: (b, p, 0, 0)),
                  pl.BlockSpec((None, None, page, WC), pmap),
                  pl.BlockSpec((None, None, page, WC), pmap),
                  pl.BlockSpec((t, WC), lambda b, p, pt: (b, 0)),
                  pl.BlockSpec((t, WC), lambda b, p, pt: (b, 0))],
        out_specs=pl.BlockSpec((t, WC), lambda b, p, pt: (b, 0)),
        scratch_shapes=[pltpu.VMEM((LANES, WC), F32), pltpu.VMEM((LANES, WC), F32),
                        pltpu.VMEM((HC, t, LANES), F32), pltpu.VMEM((HC, t, 1), F32),
                        pltpu.VMEM((HC, t, 1), F32)])
    return pl.pallas_call(
        functools.partial(_attn_sample_body, t=t, npg=npg),
        grid_spec=grid_spec,
        out_shape=jax.ShapeDtypeStruct((bsz * t, WC), F32),
        compiler_params=_cparams(("parallel", "arbitrary")),
    )(page_table, q, sel, cache_k, cache_v, knew, vnew)


def _rope_tables(pos):
    half = ROPE_DIM // 2
    inv_freq = ROPE_THETA ** (-jnp.arange(half, dtype=F32) / half)
    ang = pos.astype(F32)[:, None] * inv_freq[None, :]
    cos, sin = jnp.cos(ang), jnp.sin(ang)
    n = pos.shape[0]
    one = jnp.ones((n, DH - ROPE_DIM), F32)
    zero_h = jnp.zeros((n, half), F32)
    zero_r = jnp.zeros((n, DH - ROPE_DIM), F32)
    c64 = jnp.concatenate([cos, cos, one], axis=1)
    s1 = jnp.concatenate([-sin, zero_h, zero_r], axis=1)
    s2 = jnp.concatenate([zero_h, sin, zero_r], axis=1)
    rep = LANES // DH
    return tuple(jnp.tile(a, (1, rep)) for a in (c64, s1, s2))


def _reorder_w_in(w_in):
    sizes = (WA, WA, WA, QKV_B, WB, HB, HB, 3 * WC, HI * DI, DI, HI)
    offs = [0]
    for s in sizes:
        offs.append(offs[-1] + s)
    seg = lambda i: w_in[:, :, offs[i]:offs[i + 1]]
    pad = jnp.zeros(w_in.shape[:2] + (LANES - DI - HI - 2 * HB,), w_in.dtype)
    order = [seg(3), seg(7), seg(0), seg(1), seg(2), seg(4), seg(8), seg(9), seg(10), seg(5), seg(6), pad]
    return jnp.concatenate(order, axis=-1).astype(BF16)


def _pad_state(prev):
    return jnp.pad(prev, ((0, 0), (SUBLANES - prev.shape[1], 0), (0, 0)))


def kernel(x_prompt, x_sample, state_conv_a, state_gdn_conv, state_gdn, cache_k, cache_v, cache_kidx, page_table, ffn1_norm, ffn1_w_gate, ffn1_w_up, ffn1_w_down, mix_norm, w_in, conv_a_w, gdn_conv_w, gdn_a_log, gdn_dt_bias, gdn_out_norm, attn_q_norm, attn_k_norm, w_out, ffn2_norm, ffn2_w_gate, ffn2_w_up, ffn2_w_down):
    depth = w_in.shape[0]
    bp, tp, d = x_prompt.shape
    bs, ts, _ = x_sample.shape
    mp, ms = bp * tp, bs * ts
    npool, page = cache_k.shape[1], cache_k.shape[2]
    past = page_table.shape[1] * page
    assert ts == SUBLANES and tp % LANES == 0 and mp % 256 == 0

    x = jnp.concatenate([x_prompt.reshape(mp, d), x_sample.reshape(ms, d)], axis=0)
    w_in_r = _reorder_w_in(w_in)
    bf = lambda a: a.astype(BF16)
    f1g, f1u, f1d = bf(ffn1_w_gate), bf(ffn1_w_up), bf(ffn1_w_down)
    f2g, f2u, f2d = bf(ffn2_w_gate), bf(ffn2_w_up), bf(ffn2_w_down)
    w_out_b = bf(w_out)
    ck = cache_k.reshape(depth, npool, page, WC)
    cv = cache_v.reshape(depth, npool, page, WC)

    li = jnp.arange(LANES)
    bd = ((li[:, None] // DH) == (li[None, :] // DH)).astype(BF16)
    tri = (li[:, None] <= li[None, :]).astype(BF16)
    r_p = _pick(tp, (256, 128))
    r_s = _pick(ms, (128, 64, 32, 16, 8))
    tabs_p = _rope_tables(jnp.arange(tp))
    tabs_s = _rope_tables(jnp.tile(past + jnp.arange(ts), r_s // ts))
    zero_a = jnp.zeros((bp, SUBLANES, WA), F32)
    zero_b = jnp.zeros((bp, SUBLANES, QKV_B), F32)
    zero_s = jnp.zeros((bp, HB, DKB, DVB), F32)
    lane_pad = lambda v, off: jnp.zeros((1, LANES), F32).at[0, off:off + v.shape[0]].set(v)

    outs = [[] for _ in range(12)]
    for l in range(depth):
        x = _ffn(x, ffn1_norm[l][None], f1g[l], f1u[l], f1d[l])
        p = _inproj(x, mix_norm[l][None], w_in_r[l])
        avec = lane_pad(jnp.exp(gdn_a_log[l].astype(F32)), LANE_GA)
        dtb = lane_pad(gdn_dt_bias[l].astype(F32), LANE_GA)
        gn = gdn_out_norm[l][None].astype(F32)
        qn = jnp.tile(attn_q_norm[l].astype(F32), LANES // DH)[None]
        kn = jnp.tile(attn_k_norm[l].astype(F32), LANES // DH)[None]

        ya, yb, yc = [], [], []
        groups = ((0, bp, tp, zero_a, zero_b, zero_s), (mp, bs, ts, _pad_state(state_conv_a[l]),
                                                       _pad_state(state_gdn_conv[l]), state_gdn[l]))
        for gi, (row0, bsz, t, prev_a, prev_b, s0) in enumerate(groups):
            m = bsz * t
            y_a, st_a = _mixer_a(p, row0, bsz, t, conv_a_w[l], prev_a)
            qg, kg, vg, gbg, st_b = _gdn_prep(p, row0, bsz, t, gdn_conv_w[l], prev_b, bd, avec, dtb)
            y_b, s_new = _gdn(qg, kg, vg, gbg, p, row0, bsz, t, s0, gn)
            if gi == 0:
                qc, kc, vc, qi, kiw = _attn_prep(p, row0, m, r_p, tabs_p, bd, qn, kn)
                y_c = _attn_prompt(qc, qi, kiw, bf(kc), bf(vc), bf(kiw[:, :DI]), tri, bsz, t)
            else:
                qc, kc, vc, qi, kiw = _attn_prep(p, row0, m, r_s, tabs_s, bd, qn, kn)
                sel = _idx_sample(page_table, qi, kiw, cache_kidx, l, tri, 0, bsz, t)
                y_c = _attn_sample(page_table, qc, sel, ck, cv, kc, vc, l, 0, bsz, t)
            ya.append(y_a)
            yb.append(y_b)
            yc.append(y_c)
            res = (st_a[:, SUBLANES - (CONV_A - 1):], st_b[:, SUBLANES - (GDN_CONV - 1):], s_new,
                   kc.reshape(bsz, t, HC, DH), vc.reshape(bsz, t, HC, DH), kiw[:, :DI].reshape(bsz, t, DI))
            for i, r in enumerate(res):
                outs[gi * 6 + i].append(r)
        x = _outproj(x, jnp.concatenate(ya), jnp.concatenate(yb), jnp.concatenate(yc), w_out_b[l])
        x = _ffn(x, ffn2_norm[l][None], f2g[l], f2u[l], f2d[l])

    y_prompt = x[:mp].reshape(bp, tp, d)
    y_sample = x[mp:].reshape(bs, ts, d)
    return (y_prompt, y_sample) + tuple(jnp.stack(o) for o in outs)
```

```python
import functools
import math

import jax
import jax.numpy as jnp
from jax import lax
from jax.experimental import pallas as pl
from jax.experimental.pallas import tpu as pltpu

F32 = jnp.float32
BF16 = jnp.bfloat16
I32 = jnp.int32

EPS = 1e-6
LANES = 128
SUBLANES = 8
VMEM_LIMIT = 56 * 1024 * 1024

A_GROUPS = 4
CONV_A = 3
HB = 6
DKB = 64
DVB = 64
GDN_CONV = 4
GDN_ROWS = 64
HC = 6
DH = 64
HI = 4
DI = 64
TOPK_MAX = 256
ROPE_DIM = DH // 4
ROPE_THETA = 500000.0
INT_MIN = -2 ** 31
COUNT_ROWS = 4 * SUBLANES
BITS_PER_CHECK = 4
NEG_BIG = -1e30

WA = 256
QKV_B = 2 * HB * DKB + HB * DVB
WB = HB * DVB
WC = HC * DH
OFF_BQKV = 0
OFF_CQKV = QKV_B
OFF_A = 2 * QKV_B
OFF_BZ = OFF_A + 3 * WA
OFF_CI = OFF_BZ + WB
N_P = OFF_CI + 384
LANE_W = DI
LANE_GA = DI + HI
LANE_GB = DI + HI + HB


def _pick(n, cands):
    for c in cands:
        if n % c == 0:
            return c
    raise ValueError(f"no tile for {n}")


def _cparams(sem):
    return pltpu.CompilerParams(dimension_semantics=sem, vmem_limit_bytes=VMEM_LIMIT)


def _dot(a, b):
    return jnp.dot(a, b, preferred_element_type=F32)


def _dot_nt(a, b):
    return lax.dot_general(a, b, (((1,), (1,)), ((), ())), preferred_element_type=F32)


def _dot_hi(a, b):
    return jnp.dot(a, b, preferred_element_type=F32, precision=lax.Precision.HIGHEST)


def _dot_nt_hi(a, b):
    return lax.dot_general(a, b, (((1,), (1,)), ((), ())), preferred_element_type=F32,
                           precision=lax.Precision.HIGHEST)


def _dot3(a, b):
    ah = a.astype(BF16)
    al = (a - ah.astype(F32)).astype(BF16)
    bh = b.astype(BF16)
    bl = (b - bh.astype(F32)).astype(BF16)
    return _dot(ah, bh) + _dot(ah, bl) + _dot(al, bh)


def _segsum(x, bd):
    hi = x.astype(BF16)
    lo = (x - hi.astype(F32)).astype(BF16)
    return _dot(hi, bd) + _dot(lo, bd)


def _sigmoid(x):
    return 1.0 / (1.0 + jnp.exp(-x))


def _silu(x):
    return x * _sigmoid(x)


def _softplus(x):
    return jnp.maximum(x, 0.0) + jnp.log(1.0 + jnp.exp(-jnp.abs(x)))


def _ffn_body(x_ref, g_ref, wg_ref, wu_ref, wd_ref, o_ref, h_ref, acc_ref):
    j = pl.program_id(1)

    @pl.when(j == 0)
    def _():
        x = x_ref[...]
        ms = jnp.mean(x * x, axis=-1, keepdims=True)
        h_ref[...] = (x * lax.rsqrt(ms + EPS) * g_ref[...]).astype(BF16)
        acc_ref[...] = jnp.zeros_like(acc_ref)

    h = h_ref[...]
    a = _dot(h, wg_ref[...])
    b = _dot(h, wu_ref[...])
    t = (_silu(a) * b).astype(BF16)
    acc_ref[...] += _dot(t, wd_ref[...])

    @pl.when(j == pl.num_programs(1) - 1)
    def _():
        o_ref[...] = x_ref[...] + 0.5 * acc_ref[...]


def _ffn(x, g, wg, wu, wd):
    m, d = x.shape
    f = wg.shape[1]
    tm = _pick(m, (512, 256, 128, 64, 32, 16, 8))
    tf = _pick(f, (256, 128))
    return pl.pallas_call(
        _ffn_body,
        grid=(m // tm, f // tf),
        in_specs=[pl.BlockSpec((tm, d), lambda i, j: (i, 0)),
                  pl.BlockSpec((1, d), lambda i, j: (0, 0)),
                  pl.BlockSpec((d, tf), lambda i, j: (0, j)),
                  pl.BlockSpec((d, tf), lambda i, j: (0, j)),
                  pl.BlockSpec((tf, d), lambda i, j: (j, 0))],
        out_specs=pl.BlockSpec((tm, d), lambda i, j: (i, 0)),
        out_shape=jax.ShapeDtypeStruct((m, d), F32),
        scratch_shapes=[pltpu.VMEM((tm, d), BF16), pltpu.VMEM((tm, d), F32)],
        compiler_params=_cparams(("parallel", "arbitrary")),
    )(x, g, wg, wu, wd)


def _inproj_body(x_ref, g_ref, w_ref, o_ref, h_ref):
    @pl.when(pl.program_id(1) == 0)
    def _():
        x = x_ref[...]
        ms = jnp.mean(x * x, axis=-1, keepdims=True)
        h_ref[...] = (x * lax.rsqrt(ms + EPS) * g_ref[...]).astype(BF16)

    o_ref[...] = _dot(h_ref[...], w_ref[...])


def _inproj(x, g, w):
    m, d = x.shape
    n = w.shape[1]
    tm = _pick(m, (512, 256, 128, 64, 32, 16, 8))
    tn = _pick(n, (768, 384, 128))
    return pl.pallas_call(
        _inproj_body,
        grid=(m // tm, n // tn),
        in_specs=[pl.BlockSpec((tm, d), lambda i, j: (i, 0)),
                  pl.BlockSpec((1, d), lambda i, j: (0, 0)),
                  pl.BlockSpec((d, tn), lambda i, j: (0, j))],
        out_specs=pl.BlockSpec((tm, tn), lambda i, j: (i, j)),
        out_shape=jax.ShapeDtypeStruct((m, n), F32),
        scratch_shapes=[pltpu.VMEM((tm, d), BF16)],
        compiler_params=_cparams(("parallel", "arbitrary")),
    )(x, g, w)


def _outproj_body(x_ref, ya_ref, yb_ref, yc_ref, w_ref, o_ref):
    acc = _dot(ya_ref[...].astype(BF16), w_ref[0:WA, :])
    acc += _dot(yb_ref[...].astype(BF16), w_ref[WA:WA + WB, :])
    acc += _dot(yc_ref[...].astype(BF16), w_ref[WA + WB:WA + WB + WC, :])
    o_ref[...] = x_ref[...] + acc


def _outproj(x, ya, yb, yc, w):
    m, d = x.shape
    tm = _pick(m, (512, 256, 128, 64, 32, 16, 8))
    return pl.pallas_call(
        _outproj_body,
        grid=(m // tm,),
        in_specs=[pl.BlockSpec((tm, d), lambda i: (i, 0)),
                  pl.BlockSpec((tm, WA), lambda i: (i, 0)),
                  pl.BlockSpec((tm, WB), lambda i: (i, 0)),
                  pl.BlockSpec((tm, WC), lambda i: (i, 0)),
                  pl.BlockSpec(w.shape, lambda i: (0, 0))],
        out_specs=pl.BlockSpec((tm, d), lambda i: (i, 0)),
        out_shape=jax.ShapeDtypeStruct((m, d), F32),
        compiler_params=_cparams(("parallel",)),
    )(x, ya, yb, yc, w)


def _conv_taps(u, w_ref, prev_ref, ext_ref, tt, width):
    it = pl.program_id(1)

    @pl.when(it == 0)
    def _():
        ext_ref[:, 0:SUBLANES, :] = prev_ref[...]

    @pl.when(it > 0)
    def _():
        ext_ref[:, 0:SUBLANES, :] = ext_ref[:, tt:tt + SUBLANES, :]

    ext_ref[:, SUBLANES:SUBLANES + tt, :] = u
    acc = None
    for s in range(width):
        term = ext_ref[:, SUBLANES - s:SUBLANES - s + tt, :] * w_ref[width - 1 - s:width - s, :]
        acc = term if acc is None else acc + term
    return acc, ext_ref[:, tt:tt + SUBLANES, :]


def _group_tiles(bsz, t):
    if t >= 128:
        return 1, _pick(t, (256, 128))
    return _pick(bsz, (16, 8, 4, 2, 1)), t


def _mixa_body(p_ref, w_ref, prev_ref, y_ref, st_ref, ext_ref, *, bb, tt):
    pa = p_ref[...]
    u = (pa[:, WA:2 * WA] * pa[:, 2 * WA:3 * WA]).reshape(bb, tt, WA)
    conv, tail = _conv_taps(u, w_ref, prev_ref, ext_ref, tt, CONV_A)
    y_ref[...] = pa[:, 0:WA] * conv.reshape(bb * tt, WA)
    st_ref[...] = tail


def _mixer_a(p, row0, bsz, t, w, prev8):
    bb, tt = _group_tiles(bsz, t)
    r = bb * tt
    nb, nt = bsz // bb, t // tt
    base = row0 // r
    assert row0 % r == 0
    return pl.pallas_call(
        functools.partial(_mixa_body, bb=bb, tt=tt),
        grid=(nb, nt),
        in_specs=[pl.BlockSpec((r, 3 * WA), lambda i, j: (base + i * nt + j, OFF_A // (3 * WA))),
                  pl.BlockSpec(w.shape, lambda i, j: (0, 0)),
                  pl.BlockSpec((bb, SUBLANES, WA), lambda i, j: (i, 0, 0))],
        out_specs=[pl.BlockSpec((r, WA), lambda i, j: (i * nt + j, 0)),
                   pl.BlockSpec((bb, SUBLANES, WA), lambda i, j: (i, 0, 0))],
        out_shape=[jax.ShapeDtypeStruct((bsz * t, WA), F32),
                   jax.ShapeDtypeStruct((bsz, SUBLANES, WA), F32)],
        scratch_shapes=[pltpu.VMEM((bb, tt + SUBLANES, WA), F32)],
        compiler_params=_cparams(("parallel", "arbitrary")),
    )(p, w, prev8)


def _gdnprep_body(p_ref, ci_ref, w_ref, prev_ref, bd_ref, avec_ref, dtb_ref,
                  q_ref, k_ref, v_ref, gb_ref, st_ref, ext_ref, *, bb, tt):
    r = bb * tt
    x = p_ref[...].reshape(bb, tt, QKV_B)
    conv, tail = _conv_taps(x, w_ref, prev_ref, ext_ref, tt, GDN_CONV)
    st_ref[...] = tail
    a = _silu(conv.reshape(r, QKV_B))
    bd = bd_ref[...]
    nq = HB * DKB // LANES
    for c in range(2 * nq):
        xc = a[:, c * LANES:(c + 1) * LANES]
        n = xc * lax.rsqrt(_segsum(xc * xc, bd) + EPS)
        if c < nq:
            q_ref[:, c * LANES:(c + 1) * LANES] = n * (DKB ** -0.5)
        else:
            k_ref[:, (c - nq) * LANES:(c - nq + 1) * LANES] = n
    v_ref[...] = a[:, 2 * HB * DKB:]
    raw = ci_ref[...]
    g = -avec_ref[...] * _softplus(raw + dtb_ref[...])
    lane = lax.broadcasted_iota(I32, raw.shape, 1)
    gb_ref[...] = jnp.where((lane >= LANE_GB) & (lane < LANE_GB + HB), _sigmoid(raw), g)


def _gdn_prep(p, row0, bsz, t, w, prev8, bd, avec, dtb):
    bb, tt = _group_tiles(bsz, t)
    r = bb * tt
    nb, nt = bsz // bb, t // tt
    base = row0 // r
    assert row0 % r == 0
    m = bsz * t
    rows = lambda i, j: (i * nt + j, 0)
    return pl.pallas_call(
        functools.partial(_gdnprep_body, bb=bb, tt=tt),
        grid=(nb, nt),
        in_specs=[pl.BlockSpec((r, QKV_B), lambda i, j: (base + i * nt + j, OFF_BQKV // QKV_B)),
                  pl.BlockSpec((r, LANES), lambda i, j: (base + i * nt + j, (OFF_CI + 2 * LANES) // LANES)),
                  pl.BlockSpec(w.shape, lambda i, j: (0, 0)),
                  pl.BlockSpec((bb, SUBLANES, QKV_B), lambda i, j: (i, 0, 0)),
                  pl.BlockSpec((LANES, LANES), lambda i, j: (0, 0)),
                  pl.BlockSpec((1, LANES), lambda i, j: (0, 0)),
                  pl.BlockSpec((1, LANES), lambda i, j: (0, 0))],
        out_specs=[pl.BlockSpec((r, HB * DKB), rows),
                   pl.BlockSpec((r, HB * DKB), rows),
                   pl.BlockSpec((r, WB), rows),
                   pl.BlockSpec((r, LANES), rows),
                   pl.BlockSpec((bb, SUBLANES, QKV_B), lambda i, j: (i, 0, 0))],
        out_shape=[jax.ShapeDtypeStruct((m, HB * DKB), F32),
                   jax.ShapeDtypeStruct((m, HB * DKB), F32),
                   jax.ShapeDtypeStruct((m, WB), F32),
                   jax.ShapeDtypeStruct((m, LANES), F32),
                   jax.ShapeDtypeStruct((bsz, SUBLANES, QKV_B), F32)],
        scratch_shapes=[pltpu.VMEM((bb, tt + SUBLANES, QKV_B), F32)],
        compiler_params=_cparams(("parallel", "arbitrary")),
    )(p, p, w, prev8, bd, avec, dtb)


def _gdn_body(q_ref, k_ref, v_ref, gb_ref, z_ref, s0_ref, gn_ref, y_ref, sout_ref, s_scr, *, c, g):
    r = 2 * GDN_ROWS
    n = 2 * g
    it = pl.program_id(1)

    @pl.when(it == 0)
    def _():
        s_scr[...] = s0_ref[...]

    ri = lax.broadcasted_iota(I32, (r, r), 0)
    ci = lax.broadcasted_iota(I32, (r, r), 1)
    same = (ri // c) == (ci // c)
    incl = same & (ci <= ri)
    strict = same & (ci < ri)
    eye = (ri == ci).astype(F32)
    lastsel = (ci == (ri // c) * c + (c - 1)).astype(F32)
    gb = jnp.concatenate([gb_ref[...]] * 2, axis=0)
    gcum = _dot_hi(incl.astype(F32), gb)
    glast = _dot_hi(lastsel, gcum)
    lane = lax.broadcasted_iota(I32, (r, LANES), 1)
    lane0 = (lane == 0).astype(F32)
    upper = lax.broadcasted_iota(I32, (r, 1), 0) < GDN_ROWS
    er = lax.broadcasted_iota(I32, (r, n * DKB), 0)
    ec = lax.broadcasted_iota(I32, (r, n * DKB), 1)
    emask = (er // c) == (ec // DKB)
    tr = lax.broadcasted_iota(I32, (DKB, n * DKB), 0)
    tc = lax.broadcasted_iota(I32, (DKB, n * DKB), 1)
    tile_mat = ((tc % DKB) == tr).astype(BF16)
    xr = lax.broadcasted_iota(I32, (n * DKB, r), 0)
    xc = lax.broadcasted_iota(I32, (n * DKB, r), 1)
    emask_t = (xr // DKB) == (xc // c)
    sr = lax.broadcasted_iota(I32, (n * DKB, DKB), 0)
    sc = lax.broadcasted_iota(I32, (n * DKB, DKB), 1)
    tile_mat_t = ((sr % DKB) == sc).astype(BF16)
    esel = (xc == (xr // DKB) * c + (c - 1)).astype(F32)
    gend = _dot_hi(esel, gcum)
    s_upper = lax.broadcasted_iota(I32, (n * DKB, 1), 0) < g * DKB
    n_sq = max(0, int(math.ceil(math.log2(c))) - 1)

    pairs = range(HB // 2)

    def stack(ref, width):
        return [jnp.concatenate([ref[:, 2 * i * width:(2 * i + 1) * width],
                                 ref[:, (2 * i + 1) * width:(2 * i + 2) * width]], axis=0) for i in pairs]

    def pick(mat, rows_upper, base):
        return [jnp.where(rows_upper, mat[:, base + 2 * i:base + 2 * i + 1], mat[:, base + 2 * i + 1:base + 2 * i + 2])
                for i in pairs]

    q, k, v, zz = stack(q_ref, DKB), stack(k_ref, DKB), stack(v_ref, DVB), stack(z_ref, DVB)
    gcol = pick(gcum, upper, LANE_GA)
    glc = pick(glast, upper, LANE_GA)
    beta = pick(gb, upper, LANE_GB)
    srow = [jnp.exp(x) for x in pick(gend, s_upper, LANE_GA)]
    grow = [_dot_nt_hi(lane0, jnp.where(lane == 0, gcol[i], 0.0)) for i in pairs]
    decay = [jnp.exp(jnp.where(incl, gcol[i] - grow[i], -jnp.inf)) for i in pairs]
    kb = [k[i] * beta[i] for i in pairs]
    a_mat = [_dot_nt(kb[i], k[i]) * jnp.where(strict, decay[i], 0.0) for i in pairs]
    pw = [-a_mat[i] for i in pairs]
    t_mat = [eye + pw[i] for i in pairs]
    for _ in range(n_sq):
        pw = [_dot3(pw[i], pw[i]) for i in pairs]
        t_mat = [t_mat[i] + _dot3(t_mat[i], pw[i]) for i in pairs]
    u = [_dot(t_mat[i], v[i] * beta[i]) for i in pairs]
    wk = [_dot(t_mat[i], kb[i] * jnp.exp(gcol[i])) for i in pairs]
    qk = [_dot_nt(q[i], k[i]) * decay[i] for i in pairs]
    s = [s_scr[2 * i:2 * i + 2].reshape(n * DKB, DVB) for i in pairs]
    wk_e = [jnp.where(emask, _dot(wk[i].astype(BF16), tile_mat), 0.0) for i in pairs]
    qg_e = [jnp.where(emask, _dot((q[i] * jnp.exp(gcol[i])).astype(BF16), tile_mat), 0.0) for i in pairs]
    kd_t = [jnp.where(emask_t, _dot_nt(tile_mat_t, (k[i] * jnp.exp(glc[i] - gcol[i])).astype(BF16)), 0.0)
            for i in pairs]
    v_new = [u[i] - _dot(wk_e[i], s[i]) for i in pairs]
    out = [_dot(qg_e[i], s[i]) + _dot(qk[i], v_new[i]) for i in pairs]
    s_new = [s[i] * srow[i] + _dot(kd_t[i], v_new[i]) for i in pairs]
    for i in pairs:
        s_scr[2 * i:2 * i + 2] = s_new[i].reshape(2, g * DKB, DVB)
        ms = jnp.mean(out[i] * out[i], axis=-1, keepdims=True)
        y = out[i] * lax.rsqrt(ms + EPS) * gn_ref[...] * _silu(zz[i])
        y_ref[:, 2 * i * DVB:(2 * i + 1) * DVB] = y[0:GDN_ROWS]
        y_ref[:, (2 * i + 1) * DVB:(2 * i + 2) * DVB] = y[GDN_ROWS:]

    @pl.when(it == pl.num_programs(1) - 1)
    def _():
        sout_ref[...] = s_scr[...]


def _gdn(q, k, v, gb, p, row0, bsz, t, s0, gn):
    c = min(GDN_ROWS, t)
    g = GDN_ROWS // c
    assert t % c == 0 and bsz % g == 0 and row0 % GDN_ROWS == 0
    nb, nt = bsz // g, t // c
    base = row0 // GDN_ROWS
    rows = lambda i, j: (i * nt + j, 0)
    s0r = s0.reshape(nb, g, HB, DKB, DVB).transpose(0, 2, 1, 3, 4).reshape(nb, HB, g * DKB, DVB)
    y, s_out = pl.pallas_call(
        functools.partial(_gdn_body, c=c, g=g),
        grid=(nb, nt),
        in_specs=[pl.BlockSpec((GDN_ROWS, HB * DKB), rows),
                  pl.BlockSpec((GDN_ROWS, HB * DKB), rows),
                  pl.BlockSpec((GDN_ROWS, WB), rows),
                  pl.BlockSpec((GDN_ROWS, LANES), rows),
                  pl.BlockSpec((GDN_ROWS, WB), lambda i, j: (base + i * nt + j, OFF_BZ // WB)),
                  pl.BlockSpec((None, HB, g * DKB, DVB), lambda i, j: (i, 0, 0, 0)),
                  pl.BlockSpec((1, DVB), lambda i, j: (0, 0))],
        out_specs=[pl.BlockSpec((GDN_ROWS, WB), rows),
                   pl.BlockSpec((None, HB, g * DKB, DVB), lambda i, j: (i, 0, 0, 0))],
        out_shape=[jax.ShapeDtypeStruct((bsz * t, WB), F32),
                   jax.ShapeDtypeStruct((nb, HB, g * DKB, DVB), F32)],
        scratch_shapes=[pltpu.VMEM((HB, g * DKB, DVB), F32)],
        compiler_params=_cparams(("parallel", "arbitrary")),
    )(q, k, v, gb, p, s0r, gn)
    s_out = s_out.reshape(nb, HB, g, DKB, DVB).transpose(0, 2, 1, 3, 4).reshape(bsz, HB, DKB, DVB)
    return y, s_out


def _rope(x, cos, s1, s2):
    half = ROPE_DIM // 2
    return x * cos + pltpu.roll(x, LANES - half, 1) * s1 + pltpu.roll(x, half, 1) * s2


def _attnprep_body(p_ref, ci_ref, cos_ref, s1_ref, s2_ref, bd_ref, qn_ref, kn_ref,
                   q_ref, k_ref, v_ref, qi_ref, kiw_ref):
    cos, s1, s2 = cos_ref[...], s1_ref[...], s2_ref[...]
    bd = bd_ref[...]
    nch = WC // LANES
    for c in range(2 * nch):
        x = p_ref[:, c * LANES:(c + 1) * LANES]
        gain = (qn_ref if c < nch else kn_ref)[...]
        n = x * lax.rsqrt(_segsum(x * x, bd) * (1.0 / DH) + EPS) * gain
        rot = _rope(n, cos, s1, s2)
        if c < nch:
            q_ref[:, c * LANES:(c + 1) * LANES] = (rot * (DH ** -0.5)).astype(BF16)
        else:
            k_ref[:, (c - nch) * LANES:(c - nch + 1) * LANES] = rot
    v_ref[...] = p_ref[:, 2 * WC:3 * WC]
    for c in range(HI * DI // LANES):
        x = ci_ref[:, c * LANES:(c + 1) * LANES]
        qi_ref[:, c * LANES:(c + 1) * LANES] = _rope(x, cos, s1, s2).astype(BF16)
    x = ci_ref[:, HI * DI:HI * DI + LANES]
    lane = lax.broadcasted_iota(I32, x.shape, 1)
    is_ki = lane < DI
    kiw_ref[...] = _rope(x, jnp.where(is_ki, cos, 1.0), jnp.where(is_ki, s1, 0.0), jnp.where(is_ki, s2, 0.0))


def _attn_prep(p, row0, m, r, tabs, bd, qn, kn):
    nper = tabs[0].shape[0] // r
    tmap = lambda i: (i % nper, 0)
    assert row0 % r == 0 and m % r == 0 and tabs[0].shape[0] % r == 0
    base = row0 // r
    rows = lambda i: (i, 0)
    return pl.pallas_call(
        _attnprep_body,
        grid=(m // r,),
        in_specs=[pl.BlockSpec((r, 3 * WC), lambda i: (base + i, OFF_CQKV // (3 * WC))),
                  pl.BlockSpec((r, 384), lambda i: (base + i, OFF_CI // 384)),
                  pl.BlockSpec((r, LANES), tmap), pl.BlockSpec((r, LANES), tmap), pl.BlockSpec((r, LANES), tmap),
                  pl.BlockSpec((LANES, LANES), lambda i: (0, 0)),
                  pl.BlockSpec((1, LANES), lambda i: (0, 0)),
                  pl.BlockSpec((1, LANES), lambda i: (0, 0))],
        out_specs=[pl.BlockSpec((r, WC), rows), pl.BlockSpec((r, WC), rows), pl.BlockSpec((r, WC), rows),
                   pl.BlockSpec((r, HI * DI), rows), pl.BlockSpec((r, LANES), rows)],
        out_shape=[jax.ShapeDtypeStruct((m, WC), BF16), jax.ShapeDtypeStruct((m, WC), F32),
                   jax.ShapeDtypeStruct((m, WC), F32), jax.ShapeDtypeStruct((m, HI * DI), BF16),
                   jax.ShapeDtypeStruct((m, LANES), F32)],
        compiler_params=_cparams(("parallel",)),
    )(p, p, *tabs, bd, qn, kn)


def _attnprep_t_body(p_ref, ci_ref, cos_ref, s1_ref, s2_ref, bd_ref, qn_ref, kn_ref,
                     k_ref, v_ref, kiw_ref, kb_ref, kib_ref, qt_ref, qit_ref, wt_ref, vt_ref):
    cos, s1, s2 = cos_ref[...], s1_ref[...], s2_ref[...]
    bd = bd_ref[...]
    nch = WC // LANES
    r = p_ref.shape[0]
    for c in range(2 * nch):
        x = p_ref[:, c * LANES:(c + 1) * LANES]
        gain = (qn_ref if c < nch else kn_ref)[...]
        n = x * lax.rsqrt(_segsum(x * x, bd) * (1.0 / DH) + EPS) * gain
        rot = _rope(n, cos, s1, s2)
        if c < nch:
            qt_ref[c * LANES:(c + 1) * LANES, :] = (rot * (DH ** -0.5)).T.astype(BF16)
        else:
            k_ref[:, (c - nch) * LANES:(c - nch + 1) * LANES] = rot
            kb_ref[:, (c - nch) * LANES:(c - nch + 1) * LANES] = rot.astype(BF16)
    v = p_ref[:, 2 * WC:3 * WC]
    v_ref[...] = v
    kt = vt_ref.shape[-1]
    for t in range(r // kt):
        for c in range(nch):
            vt_ref[t, c * LANES:(c + 1) * LANES, :] = v[t * kt:(t + 1) * kt, c * LANES:(c + 1) * LANES].T.astype(BF16)
    for c in range(HI * DI // LANES):
        x = ci_ref[:, c * LANES:(c + 1) * LANES]
        qit_ref[c * LANES:(c + 1) * LANES, :] = _rope(x, cos, s1, s2).T.astype(BF16)
    x = ci_ref[:, HI * DI:HI * DI + LANES]
    lane = lax.broadcasted_iota(I32, x.shape, 1)
    is_ki = lane < DI
    kiw = _rope(x, jnp.where(is_ki, cos, 1.0), jnp.where(is_ki, s1, 0.0), jnp.where(is_ki, s2, 0.0))
    kiw_ref[...] = kiw
    kib_ref[...] = kiw[:, 0:DI].astype(BF16)
    wt_ref[...] = kiw.T[LANE_W:LANE_W + SUBLANES, :]


def _attn_prep_t(p, bsz, t, r, kt, tabs, bd, qn, kn):
    nt = t // r
    m = bsz * t
    assert t % r == 0 and r % kt == 0 and kt % LANES == 0
    rows = lambda b, i: (b * nt + i, 0)
    tmap = lambda b, i: (i, 0)
    cst = lambda b, i: (0, 0)
    return pl.pallas_call(
        _attnprep_t_body,
        grid=(bsz, nt),
        in_specs=[pl.BlockSpec((r, 3 * WC), lambda b, i: (b * nt + i, OFF_CQKV // (3 * WC))),
                  pl.BlockSpec((r, 384), lambda b, i: (b * nt + i, OFF_CI // 384)),
                  pl.BlockSpec((r, LANES), tmap), pl.BlockSpec((r, LANES), tmap), pl.BlockSpec((r, LANES), tmap),
                  pl.BlockSpec((LANES, LANES), cst),
                  pl.BlockSpec((1, LANES), cst),
                  pl.BlockSpec((1, LANES), cst)],
        out_specs=[pl.BlockSpec((r, WC), rows), pl.BlockSpec((r, WC), rows), pl.BlockSpec((r, LANES), rows),
                   pl.BlockSpec((r, WC), rows), pl.BlockSpec((r, DI), rows),
                   pl.BlockSpec((None, WC, r), lambda b, i: (b, 0, i)),
                   pl.BlockSpec((None, HI * DI, r), lambda b, i: (b, 0, i)),
                   pl.BlockSpec((None, SUBLANES, r), lambda b, i: (b, 0, i)),
                   pl.BlockSpec((None, r // kt, WC, kt), lambda b, i: (b, i, 0, 0))],
        out_shape=[jax.ShapeDtypeStruct((m, WC), F32), jax.ShapeDtypeStruct((m, WC), F32),
                   jax.ShapeDtypeStruct((m, LANES), F32),
                   jax.ShapeDtypeStruct((m, WC), BF16), jax.ShapeDtypeStruct((m, DI), BF16),
                   jax.ShapeDtypeStruct((bsz, WC, t), BF16),
                   jax.ShapeDtypeStruct((bsz, HI * DI, t), BF16),
                   jax.ShapeDtypeStruct((bsz, SUBLANES, t), F32),
                   jax.ShapeDtypeStruct((bsz, t // kt, WC, kt), BF16)],
        compiler_params=_cparams(("parallel", "parallel")),
    )(p, p, *tabs, bd, qn, kn)


def _float_key(sc):
    sc = jnp.where(sc == 0.0, 0.0, sc)
    bits = lax.bitcast_convert_type(sc, I32)
    return bits ^ ((bits >> 31) & 0x7FFFFFFF)


def _kth_largest_key(count_ge, shape, topk):
    c0 = count_ge(jnp.zeros(shape, I32))
    base = jnp.where(c0 >= topk, 0, INT_MIN).astype(I32)
    done = jnp.where(c0 == topk, 1.0, 0.0)

    def cond(st):
        return (st[0] >= 0) & (st[3] > 0.0)

    def body(st):
        top, base, done, _ = st
        for k in range(BITS_PER_CHECK):
            bit = jnp.where(top - k >= 0, jnp.int32(1) << jnp.maximum(top - k, 0), 0)
            cand = jnp.where(done > 0.0, base, base + bit)
            cnt = count_ge(cand)
            base = jnp.where(cnt >= topk, cand, base)
            done = jnp.where(cnt == topk, 1.0, done)
        return top - BITS_PER_CHECK, base, done, jnp.sum(1.0 - done)

    return lax.while_loop(cond, body, (jnp.int32(30), base, done, jnp.sum(1.0 - done)))[1]


def _select_tile(key, vstar, need, running, tri, visible):
    eq = key == vstar
    pref = _dot(eq.astype(BF16), tri)
    sel = (key > vstar) | (eq & (running + pref <= need))
    if visible is not None:
        sel = sel & visible
    return sel, running + pref[:, LANES - 1:LANES]


def _attn_prompt_body(qt_ref, qit_ref, wt_ref, k_ref, vt_ref, ki_ref, ltri_ref, o_ref,
                      key_scr, acc_scr, m_scr, l_scr, raw_scr, s_scr, p_scr, a_scr, bias_scr, *, tq, kt, topk):
    qblk = pl.program_id(1)
    nkb = (qblk * tq + tq + kt - 1) // kt
    qpos = qblk * tq + lax.broadcasted_iota(I32, (1, tq), 1)
    krow = lax.broadcasted_iota(I32, (kt, 1), 0)
    w = wt_ref[...] * ((HI * DI) ** -0.5)
    qit = qit_ref[...]
    qi_cat = jnp.concatenate([qit[h * DI:(h + 1) * DI, :] for h in range(HI)], axis=1)

    def key_tile(j):
        return pl.ds(pl.multiple_of(jnp.minimum(j, nkb - 1) * kt, kt), kt)

    def idx_dot(j):
        return _dot(ki_ref[key_tile(j), :], qi_cat)

    npairs_kt = (nkb + 1) // 2

    def fill_tile(j, cur):
        raw_scr[1 - cur] = idx_dot(j + 1)
        sc = None
        for h in range(HI):
            term = w[h:h + 1, :] * jnp.maximum(raw_scr[cur, :, h * tq:(h + 1) * tq], 0.0)
            sc = term if sc is None else sc + term
        key_scr[j] = jnp.where((krow + j * kt) <= qpos, _float_key(sc), INT_MIN)

    def fill(i, carry):
        fill_tile(2 * i, 0)
        fill_tile(2 * i + 1, 1)
        return carry

    raw_scr[0] = idx_dot(0)
    lax.fori_loop(0, npairs_kt, fill, 0)

    def count_ge(cand):
        def body(j, acc):
            ind = jnp.where(key_scr[j] >= cand, 1.0, 0.0)
            return acc + jnp.sum(ind.reshape(kt // COUNT_ROWS, COUNT_ROWS, tq), axis=0)
        acc = lax.fori_loop(0, nkb, body, jnp.zeros((COUNT_ROWS, tq), F32))
        return jnp.sum(acc, axis=0, keepdims=True)

    vstar = _kth_largest_key(count_ge, (1, tq), topk)
    need = topk - count_ge(vstar + 1)

    half =lax.broadcasted_iota(I32, (LANES, tq), 0) // DH
    q_pairs = []
    for c in range(HC // 2):
        qc = qt_ref[c * LANES:(c + 1) * LANES, :].astype(F32)
        q_pairs.append(jnp.concatenate([jnp.where(half == 0, qc, 0.0), jnp.where(half == 1, qc, 0.0)],
                                       axis=1).astype(BF16))
    m_scr[...] = jnp.full_like(m_scr, NEG_BIG)
    l_scr[...] = jnp.zeros_like(l_scr)
    acc_scr[...] = jnp.zeros_like(acc_scr)
    ltri = ltri_ref[...]

    npair = HC // 2

    def store_scores(j, slot):
        kb = k_ref[key_tile(j), :]
        for c in range(npair):
            s_scr[slot, c] = _dot(kb[:, c * LANES:(c + 1) * LANES], q_pairs[c])

    def store_bias(j, slot, running):
        key = key_scr[jnp.minimum(j, nkb - 1)]
        eq = key == vstar
        pref = _dot(ltri, eq.astype(BF16))
        sel = ((key > vstar) | (eq & (running + pref <= need))) & ((krow + j * kt) <= qpos)
        bias_scr[slot] = jnp.where(sel, 0.0, -jnp.inf)
        return running + pref[kt - 1:kt, :]

    def apply_pv(j, slot):
        vtb = vt_ref[jnp.clip(j, 0, nkb - 1)]
        for c in range(npair):
            pv = _dot(vtb[c * LANES:(c + 1) * LANES, :], p_scr[slot, c])
            for hh in range(2):
                h = 2 * c + hh
                acc_scr[h * DH:(h + 1) * DH, :] = (a_scr[slot, h:h + 1, :] * acc_scr[h * DH:(h + 1) * DH, :]
                                                   + pv[hh * DH:(hh + 1) * DH, hh * tq:(hh + 1) * tq])

    def sweep_tile(j, cur, running):
        nxt = 1 - cur
        apply_pv(j - 1, nxt)
        store_scores(j + 1, nxt)
        running = store_bias(j + 1, nxt, running)
        bias = bias_scr[cur]
        for c in range(npair):
            pc = []
            for hh in range(2):
                h = 2 * c + hh
                s = s_scr[cur, c, :, hh * tq:(hh + 1) * tq] + bias
                m_old = m_scr[h:h + 1, :]
                m_new = jnp.maximum(m_old, jnp.max(s, axis=0, keepdims=True))
                p = jnp.exp(s - m_new)
                alpha = jnp.exp(m_old - m_new)
                l_scr[h:h + 1, :] = alpha * l_scr[h:h + 1, :] + jnp.sum(p, axis=0, keepdims=True)
                m_scr[h:h + 1, :] = m_new
                a_scr[cur, h:h + 1, :] = alpha
                pc.append(p.astype(BF16))
            p_scr[cur, c] = jnp.concatenate(pc, axis=1)
        return running

    def sweep(i, running):
        return sweep_tile(2 * i + 1, 1, sweep_tile(2 * i, 0, running))

    store_scores(0, 0)
    p_scr[1] = jnp.zeros_like(p_scr[1])
    a_scr[1] = jnp.ones_like(a_scr[1])
    lax.fori_loop(0, npairs_kt, sweep, store_bias(0, 0, jnp.zeros((1, tq), F32)))
    apply_pv(2 * npairs_kt - 1, 1)
    for c in range(HC // 2):
        ot = jnp.concatenate([acc_scr[h * DH:(h + 1) * DH, :] / l_scr[h:h + 1, :] for h in (2 * c, 2 * c + 1)],
                             axis=0)
        o_ref[:, c * LANES:(c + 1) * LANES] = ot.T


def _attn_prompt(qt, qit, wt, kbf, vt, kibf, ltri, bsz, s_len):
    tq = LANES
    kt = vt.shape[-1]
    nq = s_len // tq
    topk = min(TOPK_MAX, s_len // 4)
    assert s_len % kt == 0 and kt % tq == 0
    k3 = kbf.reshape(bsz, s_len, WC)
    ki3 = kibf.reshape(bsz, s_len, DI)
    return pl.pallas_call(
        functools.partial(_attn_prompt_body, tq=tq, kt=kt, topk=topk),
        grid=(bsz, nq),
        in_specs=[pl.BlockSpec((None, WC, tq), lambda b, i: (b, 0, i)),
                  pl.BlockSpec((None, HI * DI, tq), lambda b, i: (b, 0, i)),
                  pl.BlockSpec((None, SUBLANES, tq), lambda b, i: (b, 0, i)),
                  pl.BlockSpec((None, s_len, WC), lambda b, i: (b, 0, 0)),
                  pl.BlockSpec((None, s_len // kt, WC, kt), lambda b, i: (b, 0, 0, 0)),
                  pl.BlockSpec((None, s_len, DI), lambda b, i: (b, 0, 0)),
                  pl.BlockSpec((kt, kt), lambda b, i: (0, 0))],
        out_specs=pl.BlockSpec((tq, WC), lambda b, i: (b * nq + i, 0)),
        out_shape=jax.ShapeDtypeStruct((bsz * s_len, WC), F32),
        scratch_shapes=[pltpu.VMEM((s_len // kt, kt, tq), I32),
                        pltpu.VMEM((WC, tq), F32),
                        pltpu.VMEM((SUBLANES, tq), F32),
                        pltpu.VMEM((SUBLANES, tq), F32),
                        pltpu.VMEM((2, kt, HI * tq), F32),
                        pltpu.VMEM((2, HC // 2, kt, 2 * tq), F32),
                        pltpu.VMEM((2, HC // 2, kt, 2 * tq), BF16),
                        pltpu.VMEM((2, SUBLANES, tq), F32),
                        pltpu.VMEM((2, kt, tq), F32)],
        compiler_params=_cparams(("parallel", "arbitrary")),
    )(qt, qit, wt, k3, vt, ki3, ltri)


def _attn_sample_body(pt_ref, q_ref, qi_ref, kiw_ref, knew_ref, vnew_ref, tri_ref, *refs, t, npg, topk):
    kidx_refs, k_refs, v_refs = refs[0:npg], refs[npg:2 * npg], refs[2 * npg:3 * npg]
    o_ref, s_scr, kipad_scr, kpad_scr, vpad_scr = refs[3 * npg:]
    nt = npg + 1
    qi = qi_ref[...]
    qi_stack = jnp.concatenate([qi[:, h * DI:(h + 1) * DI] for h in range(HI)], axis=0)
    w = kiw_ref[:, LANE_W:LANE_W + HI] * ((HI * DI) ** -0.5)
    w_stack = jnp.concatenate([w[:, h:h + 1] for h in range(HI)], axis=0)

    def score_keys(kib):
        rel = jnp.maximum(_dot_nt(qi_stack, kib), 0.0) * w_stack
        sc = rel[0:t]
        for h in range(1, HI):
            sc = sc + rel[h * t:(h + 1) * t]
        return _float_key(sc)

    keys = [score_keys(kidx_refs[j][...].astype(BF16)) for j in range(npg)]
    kipad_scr[...] = jnp.zeros_like(kipad_scr)
    kipad_scr[0:t, :] = kiw_ref[:, 0:DI]
    vis_new = lax.broadcasted_iota(I32, (t, LANES), 1) <= lax.broadcasted_iota(I32, (t, LANES), 0)
    keys.append(jnp.where(vis_new, score_keys(kipad_scr[...].astype(BF16)), INT_MIN))

    def count_ge(cand):
        acc = jnp.zeros((t, LANES), F32)
        for key in keys:
            acc = acc + jnp.where(key >= cand, 1.0, 0.0)
        return jnp.sum(acc, axis=1, keepdims=True)

    vstar = _kth_largest_key(count_ge, (t, 1), topk)
    need = topk - count_ge(vstar + 1)

    kpad_scr[...] = jnp.zeros_like(kpad_scr)
    vpad_scr[...] = jnp.zeros_like(vpad_scr)
    kpad_scr[0:t, :] = knew_ref[...]
    vpad_scr[0:t, :] = vnew_ref[...]
    head_of_lane = lax.broadcasted_iota(I32, (t, WC), 1) // DH
    q = q_ref[...].astype(F32)
    q_stack = jnp.concatenate([jnp.where(head_of_lane == h, q, 0.0) for h in range(HC)], axis=0).astype(BF16)
    tri = tri_ref[...]
    running = jnp.zeros((t, 1), F32)
    m_acc = jnp.full((HC * t, LANES), NEG_BIG, F32)
    for j in range(nt):
        sel, running = _select_tile(keys[j], vstar, need, running, tri, vis_new if j == npg else None)
        bias = jnp.where(sel, 0.0, -jnp.inf)
        kb = (k_refs[j] if j < npg else kpad_scr)[...].astype(BF16)
        s = _dot_nt(q_stack, kb) + jnp.concatenate([bias] * HC, axis=0)
        s_scr[j] = s
        m_acc = jnp.maximum(m_acc, s)
    m = jnp.max(m_acc, axis=1, keepdims=True)
    l_acc = jnp.zeros((HC * t, LANES), F32)
    acc = jnp.zeros((HC * t, WC), F32)
    for j in range(nt):
        p = jnp.exp(s_scr[j] - m)
        l_acc = l_acc + p
        vb = (v_refs[j] if j < npg else vpad_scr)[...].astype(BF16)
        acc = acc + _dot(p.astype(BF16), vb)
    o = acc / jnp.sum(l_acc, axis=1, keepdims=True)
    out = jnp.where(head_of_lane == 0, o[0:t], 0.0)
    for h in range(1, HC):
        out = out + jnp.where(head_of_lane == h, o[h * t:(h + 1) * t], 0.0)
    o_ref[...] = out


def _attn_sample(page_table, q, qi, kiw, knew, vnew, tri, cache_kidx, cache_k, cache_v, layer, bsz, t):
    npg = page_table.shape[1]
    page = cache_k.shape[2]
    assert page == LANES
    topk = min(TOPK_MAX, (npg * page + t) // 4)
    row = lambda b, pt: (b, 0)

    def page_map(j):
        return lambda b, pt: (layer, pt[b, j], 0, 0)

    in_specs = [pl.BlockSpec((t, WC), row), pl.BlockSpec((t, HI * DI), row), pl.BlockSpec((t, LANES), row),
                pl.BlockSpec((t, WC), row), pl.BlockSpec((t, WC), row),
                pl.BlockSpec((LANES, LANES), lambda b, pt: (0, 0))]
    in_specs += [pl.BlockSpec((None, None, page, DI), page_map(j)) for j in range(npg)]
    in_specs += [pl.BlockSpec((None, None, page, WC), page_map(j)) for j in range(npg)]
    in_specs += [pl.BlockSpec((None, None, page, WC), page_map(j)) for j in range(npg)]
    grid_spec = pltpu.PrefetchScalarGridSpec(
        num_scalar_prefetch=1,
        grid=(bsz,),
        in_specs=in_specs,
        out_specs=pl.BlockSpec((t, WC), row),
        scratch_shapes=[pltpu.VMEM((npg + 1, HC * t, LANES), F32), pltpu.VMEM((LANES, DI), F32),
                        pltpu.VMEM((LANES, WC), F32), pltpu.VMEM((LANES, WC), F32)])
    return pl.pallas_call(
        functools.partial(_attn_sample_body, t=t, npg=npg, topk=topk),
        grid_spec=grid_spec,
        out_shape=jax.ShapeDtypeStruct((bsz * t, WC), F32),
        compiler_params=_cparams(("parallel",)),
    )(page_table, q, qi, kiw, knew, vnew, tri,
      *([cache_kidx] * npg), *([cache_k] * npg), *([cache_v] * npg))


def _rope_tables(pos):
    half = ROPE_DIM // 2
    inv_freq = ROPE_THETA ** (-jnp.arange(half, dtype=F32) / half)
    ang = pos.astype(F32)[:, None] * inv_freq[None, :]
    cos, sin = jnp.cos(ang), jnp.sin(ang)
    n = pos.shape[0]
    one = jnp.ones((n, DH - ROPE_DIM), F32)
    zero_h = jnp.zeros((n, half), F32)
    zero_r = jnp.zeros((n, DH - ROPE_DIM), F32)
    c64 = jnp.concatenate([cos, cos, one], axis=1)
    s1 = jnp.concatenate([-sin, zero_h, zero_r], axis=1)
    s2 = jnp.concatenate([zero_h, sin, zero_r], axis=1)
    rep = LANES // DH
    return tuple(jnp.tile(a, (1, rep)) for a in (c64, s1, s2))


def _reorder_w_in(w_in):
    sizes = (WA, WA, WA, QKV_B, WB, HB, HB, 3 * WC, HI * DI, DI, HI)
    offs = [0]
    for s in sizes:
        offs.append(offs[-1] + s)
    seg = lambda i: w_in[:, :, offs[i]:offs[i + 1]]
    pad = jnp.zeros(w_in.shape[:2] + (LANES - DI - HI - 2 * HB,), w_in.dtype)
    order = [seg(3), seg(7), seg(0), seg(1), seg(2), seg(4), seg(8), seg(9), seg(10), seg(5), seg(6), pad]
    return jnp.concatenate(order, axis=-1).astype(BF16)


def _pad_state(prev):
    return jnp.pad(prev, ((0, 0), (SUBLANES - prev.shape[1], 0), (0, 0)))


def kernel(x_prompt, x_sample, state_conv_a, state_gdn_conv, state_gdn, cache_k, cache_v, cache_kidx, page_table, ffn1_norm, ffn1_w_gate, ffn1_w_up, ffn1_w_down, mix_norm, w_in, conv_a_w, gdn_conv_w, gdn_a_log, gdn_dt_bias, gdn_out_norm, attn_q_norm, attn_k_norm, w_out, ffn2_norm, ffn2_w_gate, ffn2_w_up, ffn2_w_down):
    depth = w_in.shape[0]
    bp, tp, d = x_prompt.shape
    bs, ts, _ = x_sample.shape
    mp, ms = bp * tp, bs * ts
    npool, page = cache_k.shape[1], cache_k.shape[2]
    past = page_table.shape[1] * page
    assert ts == SUBLANES and tp % LANES == 0 and mp % 256 == 0

    x = jnp.concatenate([x_prompt.reshape(mp, d), x_sample.reshape(ms, d)], axis=0)
    w_in_r = _reorder_w_in(w_in)
    bf = lambda a: a.astype(BF16)
    f1g, f1u, f1d = bf(ffn1_w_gate), bf(ffn1_w_up), bf(ffn1_w_down)
    f2g, f2u, f2d = bf(ffn2_w_gate), bf(ffn2_w_up), bf(ffn2_w_down)
    w_out_b = bf(w_out)
    ck = cache_k.reshape(depth, npool, page, WC)
    cv = cache_v.reshape(depth, npool, page, WC)

    li = jnp.arange(LANES)
    bd = ((li[:, None] // DH) == (li[None, :] // DH)).astype(BF16)
    tri = (li[:, None] <= li[None, :]).astype(BF16)
    kt = r_p = _pick(tp, (256, 128))
    lk = jnp.arange(kt)
    ltri = (lk[None, :] <= lk[:, None]).astype(BF16)
    r_s =_pick(ms, (128, 64, 32, 16, 8))
    tabs_p = _rope_tables(jnp.arange(tp))
    tabs_s = _rope_tables(jnp.tile(past + jnp.arange(ts), r_s // ts))
    zero_a = jnp.zeros((bp, SUBLANES, WA), F32)
    zero_b = jnp.zeros((bp, SUBLANES, QKV_B), F32)
    zero_s = jnp.zeros((bp, HB, DKB, DVB), F32)
    lane_pad = lambda v, off: jnp.zeros((1, LANES), F32).at[0, off:off + v.shape[0]].set(v)

    outs = [[] for _ in range(12)]
    for l in range(depth):
        x = _ffn(x, ffn1_norm[l][None], f1g[l], f1u[l], f1d[l])
        p = _inproj(x, mix_norm[l][None], w_in_r[l])
        avec = lane_pad(jnp.exp(gdn_a_log[l].astype(F32)), LANE_GA)
        dtb = lane_pad(gdn_dt_bias[l].astype(F32), LANE_GA)
        gn = gdn_out_norm[l][None].astype(F32)
        qn = jnp.tile(attn_q_norm[l].astype(F32), LANES // DH)[None]
        kn = jnp.tile(attn_k_norm[l].astype(F32), LANES // DH)[None]

        ya, yb, yc = [], [], []
        groups = ((0, bp, tp, zero_a, zero_b, zero_s), (mp, bs, ts, _pad_state(state_conv_a[l]),
                                                       _pad_state(state_gdn_conv[l]), state_gdn[l]))
        for gi, (row0, bsz, t, prev_a, prev_b, s0) in enumerate(groups):
            m = bsz * t
            y_a, st_a = _mixer_a(p, row0, bsz, t, conv_a_w[l], prev_a)
            qg, kg, vg, gbg, st_b = _gdn_prep(p, row0, bsz, t, gdn_conv_w[l], prev_b, bd, avec, dtb)
            y_b, s_new = _gdn(qg, kg, vg, gbg, p, row0, bsz, t, s0, gn)
            if gi == 0:
                kc, vc, kiw, kbf, kibf, qt, qit, wt, vt = _attn_prep_t(p, bsz, t, r_p, kt, tabs_p, bd, qn, kn)
                y_c = _attn_prompt(qt, qit, wt, kbf, vt, kibf, ltri, bsz, t)
            else:
                qc, kc, vc, qi, kiw = _attn_prep(p, row0, m, r_s, tabs_s, bd, qn, kn)
                y_c = _attn_sample(page_table, qc, qi, kiw, kc, vc, tri, cache_kidx, ck, cv, l, bsz, t)
            ya.append(y_a)
            yb.append(y_b)
            yc.append(y_c)
            res = (st_a[:, SUBLANES - (CONV_A - 1):], st_b[:, SUBLANES - (GDN_CONV - 1):], s_new,
                   kc.reshape(bsz, t, HC, DH), vc.reshape(bsz, t, HC, DH), kiw[:, :DI].reshape(bsz, t, DI))
            for i, r in enumerate(res):
                outs[gi * 6 + i].append(r)
        x = _outproj(x, jnp.concatenate(ya), jnp.concatenate(yb), jnp.concatenate(yc), w_out_b[l])
        x = _ffn(x, ffn2_norm[l][None], f2g[l], f2u[l], f2d[l])

    y_prompt = x[:mp].reshape(bp, tp, d)
    y_sample = x[mp:].reshape(bs, ts, d)
    return (y_prompt, y_sample) + tuple(jnp.stack(o) for o in outs)
```

```python
import functools
import math

import jax
import jax.numpy as jnp
from jax import lax
from jax.experimental import pallas as pl
from jax.experimental.pallas import tpu as pltpu

F32 = jnp.float32
BF16 = jnp.bfloat16
I32 = jnp.int32

EPS = 1e-6
LANES = 128
SUBLANES = 8
VMEM_LIMIT = 56 * 1024 * 1024

A_GROUPS = 4
CONV_A = 3
HB = 6
DKB = 64
DVB = 64
GDN_CONV = 4
GDN_ROWS = 64
HC = 6
DH = 64
HI = 4
DI = 64
TOPK_MAX = 256
ROPE_DIM = DH // 4
ROPE_THETA = 500000.0
INT_MIN = -2 ** 31
COUNT_ROWS = 4 * SUBLANES
FLOAT_STEPS = 16
NEG_BIG = -1e30

WA = 256
QKV_B = 2 * HB * DKB + HB * DVB
WB = HB * DVB
WC = HC * DH
OFF_BQKV = 0
OFF_CQKV = QKV_B
OFF_A = 2 * QKV_B
OFF_BZ = OFF_A + 3 * WA
OFF_CI = OFF_BZ + WB
N_P = OFF_CI + 384
LANE_W = DI
LANE_GA = DI + HI
LANE_GB = DI + HI + HB


def _pick(n, cands):
    for c in cands:
        if n % c == 0:
            return c
    raise ValueError(f"no tile for {n}")


def _cparams(sem):
    return pltpu.CompilerParams(dimension_semantics=sem, vmem_limit_bytes=VMEM_LIMIT)


def _dot(a, b):
    return jnp.dot(a, b, preferred_element_type=F32)


def _dot_nt(a, b):
    return lax.dot_general(a, b, (((1,), (1,)), ((), ())), preferred_element_type=F32)


def _dot_hi(a, b):
    return jnp.dot(a, b, preferred_element_type=F32, precision=lax.Precision.HIGHEST)


def _dot_nt_hi(a, b):
    return lax.dot_general(a, b, (((1,), (1,)), ((), ())), preferred_element_type=F32,
                           precision=lax.Precision.HIGHEST)


def _dot3(a, b):
    ah = a.astype(BF16)
    al = (a - ah.astype(F32)).astype(BF16)
    bh = b.astype(BF16)
    bl = (b - bh.astype(F32)).astype(BF16)
    return _dot(ah, bh) + _dot(ah, bl) + _dot(al, bh)


def _segsum(x, bd):
    hi = x.astype(BF16)
    lo = (x - hi.astype(F32)).astype(BF16)
    return _dot(hi, bd) + _dot(lo, bd)


def _sigmoid(x):
    return 1.0 / (1.0 + jnp.exp(-x))


def _silu(x):
    return x * _sigmoid(x)


def _softplus(x):
    return jnp.maximum(x, 0.0) + jnp.log(1.0 + jnp.exp(-jnp.abs(x)))


def _ffn_body(x_ref, g_ref, wg_ref, wu_ref, wd_ref, o_ref, h_ref, acc_ref):
    j = pl.program_id(1)

    @pl.when(j == 0)
    def _():
        x = x_ref[...]
        ms = jnp.mean(x * x, axis=-1, keepdims=True)
        h_ref[...] = (x * lax.rsqrt(ms + EPS) * g_ref[...]).astype(BF16)
        acc_ref[...] = jnp.zeros_like(acc_ref)

    h = h_ref[...]
    a = _dot(h, wg_ref[...])
    b = _dot(h, wu_ref[...])
    t = (_silu(a) * b).astype(BF16)
    acc_ref[...] += _dot(t, wd_ref[...])

    @pl.when(j == pl.num_programs(1) - 1)
    def _():
        o_ref[...] = x_ref[...] + 0.5 * acc_ref[...]


def _ffn(x, g, wg, wu, wd):
    m, d = x.shape
    f = wg.shape[1]
    tm = _pick(m, (512, 256, 128, 64, 32, 16, 8))
    tf = _pick(f, (1408, 256, 128))
    return pl.pallas_call(
        _ffn_body,
        grid=(m // tm, f // tf),
        in_specs=[pl.BlockSpec((tm, d), lambda i, j: (i, 0)),
                  pl.BlockSpec((1, d), lambda i, j: (0, 0)),
                  pl.BlockSpec((d, tf), lambda i, j: (0, j)),
                  pl.BlockSpec((d, tf), lambda i, j: (0, j)),
                  pl.BlockSpec((tf, d), lambda i, j: (j, 0))],
        out_specs=pl.BlockSpec((tm, d), lambda i, j: (i, 0)),
        out_shape=jax.ShapeDtypeStruct((m, d), F32),
        scratch_shapes=[pltpu.VMEM((tm, d), BF16), pltpu.VMEM((tm, d), F32)],
        compiler_params=_cparams(("parallel", "arbitrary")),
    )(x, g, wg, wu, wd)


def _inproj_body(x_ref, g_ref, w_ref, o_ref, h_ref):
    @pl.when(pl.program_id(1) == 0)
    def _():
        x = x_ref[...]
        ms = jnp.mean(x * x, axis=-1, keepdims=True)
        h_ref[...] = (x * lax.rsqrt(ms + EPS) * g_ref[...]).astype(BF16)

    o_ref[...] = _dot(h_ref[...], w_ref[...])


def _inproj(x, g, w):
    m, d = x.shape
    n = w.shape[1]
    tm = _pick(m, (512, 256, 128, 64, 32, 16, 8))
    tn = _pick(n, (1920, 768, 384, 128))
    return pl.pallas_call(
        _inproj_body,
        grid=(m // tm, n // tn),
        in_specs=[pl.BlockSpec((tm, d), lambda i, j: (i, 0)),
                  pl.BlockSpec((1, d), lambda i, j: (0, 0)),
                  pl.BlockSpec((d, tn), lambda i, j: (0, j))],
        out_specs=pl.BlockSpec((tm, tn), lambda i, j: (i, j)),
        out_shape=jax.ShapeDtypeStruct((m, n), F32),
        scratch_shapes=[pltpu.VMEM((tm, d), BF16)],
        compiler_params=_cparams(("parallel", "arbitrary")),
    )(x, g, w)


def _outproj_body(x_ref, ya_ref, yb_ref, yc_ref, w_ref, o_ref):
    acc = _dot(ya_ref[...].astype(BF16), w_ref[0:WA, :])
    acc += _dot(yb_ref[...].astype(BF16), w_ref[WA:WA + WB, :])
    acc += _dot(yc_ref[...].astype(BF16), w_ref[WA + WB:WA + WB + WC, :])
    o_ref[...] = x_ref[...] + acc


def _outproj(x, ya, yb, yc, w):
    m, d = x.shape
    tm = _pick(m, (512, 256, 128, 64, 32, 16, 8))
    return pl.pallas_call(
        _outproj_body,
        grid=(m // tm,),
        in_specs=[pl.BlockSpec((tm, d), lambda i: (i, 0)),
                  pl.BlockSpec((tm, WA), lambda i: (i, 0)),
                  pl.BlockSpec((tm, WB), lambda i: (i, 0)),
                  pl.BlockSpec((tm, WC), lambda i: (i, 0)),
                  pl.BlockSpec(w.shape, lambda i: (0, 0))],
        out_specs=pl.BlockSpec((tm, d), lambda i: (i, 0)),
        out_shape=jax.ShapeDtypeStruct((m, d), F32),
        compiler_params=_cparams(("parallel",)),
    )(x, ya, yb, yc, w)


def _conv_taps(u, w_ref, prev_ref, ext_ref, tt, width):
    it = pl.program_id(1)

    @pl.when(it == 0)
    def _():
        ext_ref[:, 0:SUBLANES, :] = prev_ref[...]

    @pl.when(it > 0)
    def _():
        ext_ref[:, 0:SUBLANES, :] = ext_ref[:, tt:tt + SUBLANES, :]

    ext_ref[:, SUBLANES:SUBLANES + tt, :] = u
    acc = None
    for s in range(width):
        term = ext_ref[:, SUBLANES - s:SUBLANES - s + tt, :] * w_ref[width - 1 - s:width - s, :]
        acc = term if acc is None else acc + term
    return acc, ext_ref[:, tt:tt + SUBLANES, :]


def _group_tiles(bsz, t):
    if t >= 128:
        return 1, _pick(t, (256, 128))
    return _pick(bsz, (16, 8, 4, 2, 1)), t


def _mixa_body(p_ref, w_ref, prev_ref, y_ref, st_ref, ext_ref, *, bb, tt):
    pa = p_ref[...]
    u = (pa[:, WA:2 * WA] * pa[:, 2 * WA:3 * WA]).reshape(bb, tt, WA)
    conv, tail = _conv_taps(u, w_ref, prev_ref, ext_ref, tt, CONV_A)
    y_ref[...] = pa[:, 0:WA] * conv.reshape(bb * tt, WA)
    st_ref[...] = tail


def _mixer_a(p, row0, bsz, t, w, prev8):
    bb, tt = _group_tiles(bsz, t)
    r = bb * tt
    nb, nt = bsz // bb, t // tt
    base = row0 // r
    assert row0 % r == 0
    return pl.pallas_call(
        functools.partial(_mixa_body, bb=bb, tt=tt),
        grid=(nb, nt),
        in_specs=[pl.BlockSpec((r, 3 * WA), lambda i, j: (base + i * nt + j, OFF_A // (3 * WA))),
                  pl.BlockSpec(w.shape, lambda i, j: (0, 0)),
                  pl.BlockSpec((bb, SUBLANES, WA), lambda i, j: (i, 0, 0))],
        out_specs=[pl.BlockSpec((r, WA), lambda i, j: (i * nt + j, 0)),
                   pl.BlockSpec((bb, SUBLANES, WA), lambda i, j: (i, 0, 0))],
        out_shape=[jax.ShapeDtypeStruct((bsz * t, WA), F32),
                   jax.ShapeDtypeStruct((bsz, SUBLANES, WA), F32)],
        scratch_shapes=[pltpu.VMEM((bb, tt + SUBLANES, WA), F32)],
        compiler_params=_cparams(("parallel", "arbitrary")),
    )(p, w, prev8)


def _gdnprep_body(p_ref, ci_ref, w_ref, prev_ref, bd_ref, avec_ref, dtb_ref,
                  q_ref, k_ref, v_ref, gb_ref, st_ref, ext_ref, *, bb, tt):
    r = bb * tt
    x = p_ref[...].reshape(bb, tt, QKV_B)
    conv, tail = _conv_taps(x, w_ref, prev_ref, ext_ref, tt, GDN_CONV)
    st_ref[...] = tail
    a = _silu(conv.reshape(r, QKV_B))
    bd = bd_ref[...]
    nq = HB * DKB // LANES
    for c in range(2 * nq):
        xc = a[:, c * LANES:(c + 1) * LANES]
        n = xc * lax.rsqrt(_segsum(xc * xc, bd) + EPS)
        if c < nq:
            q_ref[:, c * LANES:(c + 1) * LANES] = n * (DKB ** -0.5)
        else:
            k_ref[:, (c - nq) * LANES:(c - nq + 1) * LANES] = n
    v_ref[...] = a[:, 2 * HB * DKB:]
    raw = ci_ref[...]
    g = -avec_ref[...] * _softplus(raw + dtb_ref[...])
    lane = lax.broadcasted_iota(I32, raw.shape, 1)
    gb_ref[...] = jnp.where((lane >= LANE_GB) & (lane < LANE_GB + HB), _sigmoid(raw), g)


def _gdn_prep(p, row0, bsz, t, w, prev8, bd, avec, dtb):
    bb, tt = _group_tiles(bsz, t)
    r = bb * tt
    nb, nt = bsz // bb, t // tt
    base = row0 // r
    assert row0 % r == 0
    m = bsz * t
    rows = lambda i, j: (i * nt + j, 0)
    return pl.pallas_call(
        functools.partial(_gdnprep_body, bb=bb, tt=tt),
        grid=(nb, nt),
        in_specs=[pl.BlockSpec((r, QKV_B), lambda i, j: (base + i * nt + j, OFF_BQKV // QKV_B)),
                  pl.BlockSpec((r, LANES), lambda i, j: (base + i * nt + j, (OFF_CI + 2 * LANES) // LANES)),
                  pl.BlockSpec(w.shape, lambda i, j: (0, 0)),
                  pl.BlockSpec((bb, SUBLANES, QKV_B), lambda i, j: (i, 0, 0)),
                  pl.BlockSpec((LANES, LANES), lambda i, j: (0, 0)),
                  pl.BlockSpec((1, LANES), lambda i, j: (0, 0)),
                  pl.BlockSpec((1, LANES), lambda i, j: (0, 0))],
        out_specs=[pl.BlockSpec((r, HB * DKB), rows),
                   pl.BlockSpec((r, HB * DKB), rows),
                   pl.BlockSpec((r, WB), rows),
                   pl.BlockSpec((r, LANES), rows),
                   pl.BlockSpec((bb, SUBLANES, QKV_B), lambda i, j: (i, 0, 0))],
        out_shape=[jax.ShapeDtypeStruct((m, HB * DKB), F32),
                   jax.ShapeDtypeStruct((m, HB * DKB), F32),
                   jax.ShapeDtypeStruct((m, WB), F32),
                   jax.ShapeDtypeStruct((m, LANES), F32),
                   jax.ShapeDtypeStruct((bsz, SUBLANES, QKV_B), F32)],
        scratch_shapes=[pltpu.VMEM((bb, tt + SUBLANES, QKV_B), F32)],
        compiler_params=_cparams(("parallel", "arbitrary")),
    )(p, p, w, prev8, bd, avec, dtb)


def _gdn_body(q_ref, k_ref, v_ref, gb_ref, z_ref, s0_ref, gn_ref, y_ref, sout_ref, s_scr, *, c, g):
    r = 2 * GDN_ROWS
    n = 2 * g
    it = pl.program_id(1)

    @pl.when(it == 0)
    def _():
        s_scr[...] = s0_ref[...]

    ri = lax.broadcasted_iota(I32, (r, r), 0)
    ci = lax.broadcasted_iota(I32, (r, r), 1)
    same = (ri // c) == (ci // c)
    incl = same & (ci <= ri)
    strict = same & (ci < ri)
    eye = (ri == ci).astype(F32)
    lastsel = (ci == (ri // c) * c + (c - 1)).astype(F32)
    gb = jnp.concatenate([gb_ref[...]] * 2, axis=0)
    gcum = _dot_hi(incl.astype(F32), gb)
    glast = _dot_hi(lastsel, gcum)
    lane = lax.broadcasted_iota(I32, (r, LANES), 1)
    lane0 = (lane == 0).astype(F32)
    upper = lax.broadcasted_iota(I32, (r, 1), 0) < GDN_ROWS
    er = lax.broadcasted_iota(I32, (r, n * DKB), 0)
    ec = lax.broadcasted_iota(I32, (r, n * DKB), 1)
    emask = (er // c) == (ec // DKB)
    tr = lax.broadcasted_iota(I32, (DKB, n * DKB), 0)
    tc = lax.broadcasted_iota(I32, (DKB, n * DKB), 1)
    tile_mat = ((tc % DKB) == tr).astype(BF16)
    xr = lax.broadcasted_iota(I32, (n * DKB, r), 0)
    xc = lax.broadcasted_iota(I32, (n * DKB, r), 1)
    emask_t = (xr // DKB) == (xc // c)
    sr = lax.broadcasted_iota(I32, (n * DKB, DKB), 0)
    sc = lax.broadcasted_iota(I32, (n * DKB, DKB), 1)
    tile_mat_t = ((sr % DKB) == sc).astype(BF16)
    esel = (xc == (xr // DKB) * c + (c - 1)).astype(F32)
    gend = _dot_hi(esel, gcum)
    s_upper = lax.broadcasted_iota(I32, (n * DKB, 1), 0) < g * DKB
    n_sq = max(0, int(math.ceil(math.log2(c))) - 1)

    pairs = range(HB // 2)

    def stack(ref, width):
        return [jnp.concatenate([ref[:, 2 * i * width:(2 * i + 1) * width],
                                 ref[:, (2 * i + 1) * width:(2 * i + 2) * width]], axis=0) for i in pairs]

    def pick(mat, rows_upper, base):
        return [jnp.where(rows_upper, mat[:, base + 2 * i:base + 2 * i + 1], mat[:, base + 2 * i + 1:base + 2 * i + 2])
                for i in pairs]

    q, k, v, zz = stack(q_ref, DKB), stack(k_ref, DKB), stack(v_ref, DVB), stack(z_ref, DVB)
    gcol = pick(gcum, upper, LANE_GA)
    glc = pick(glast, upper, LANE_GA)
    beta = pick(gb, upper, LANE_GB)
    srow = [jnp.exp(x) for x in pick(gend, s_upper, LANE_GA)]
    grow = [_dot_nt_hi(lane0, jnp.where(lane == 0, gcol[i], 0.0)) for i in pairs]
    decay = [jnp.exp(jnp.where(incl, gcol[i] - grow[i], -jnp.inf)) for i in pairs]
    kb = [k[i] * beta[i] for i in pairs]
    a_mat = [_dot_nt(kb[i], k[i]) * jnp.where(strict, decay[i], 0.0) for i in pairs]
    pw = [-a_mat[i] for i in pairs]
    t_mat = [eye + pw[i] for i in pairs]
    for _ in range(n_sq):
        pw = [_dot3(pw[i], pw[i]) for i in pairs]
        t_mat = [t_mat[i] + _dot3(t_mat[i], pw[i]) for i in pairs]
    u = [_dot(t_mat[i], v[i] * beta[i]) for i in pairs]
    wk = [_dot(t_mat[i], kb[i] * jnp.exp(gcol[i])) for i in pairs]
    qk = [_dot_nt(q[i], k[i]) * decay[i] for i in pairs]
    s = [s_scr[2 * i:2 * i + 2].reshape(n * DKB, DVB) for i in pairs]
    wk_e = [jnp.where(emask, _dot(wk[i].astype(BF16), tile_mat), 0.0) for i in pairs]
    qg_e = [jnp.where(emask, _dot((q[i] * jnp.exp(gcol[i])).astype(BF16), tile_mat), 0.0) for i in pairs]
    kd_t = [jnp.where(emask_t, _dot_nt(tile_mat_t, (k[i] * jnp.exp(glc[i] - gcol[i])).astype(BF16)), 0.0)
            for i in pairs]
    v_new = [u[i] - _dot(wk_e[i], s[i]) for i in pairs]
    out = [_dot(qg_e[i], s[i]) + _dot(qk[i], v_new[i]) for i in pairs]
    s_new = [s[i] * srow[i] + _dot(kd_t[i], v_new[i]) for i in pairs]
    for i in pairs:
        s_scr[2 * i:2 * i + 2] = s_new[i].reshape(2, g * DKB, DVB)
        ms = jnp.mean(out[i] * out[i], axis=-1, keepdims=True)
        y = out[i] * lax.rsqrt(ms + EPS) * gn_ref[...] * _silu(zz[i])
        y_ref[:, 2 * i * DVB:(2 * i + 1) * DVB] = y[0:GDN_ROWS]
        y_ref[:, (2 * i + 1) * DVB:(2 * i + 2) * DVB] = y[GDN_ROWS:]

    @pl.when(it == pl.num_programs(1) - 1)
    def _():
        sout_ref[...] = s_scr[...]


def _gdn(q, k, v, gb, p, row0, bsz, t, s0, gn):
    c = min(GDN_ROWS, t)
    g = GDN_ROWS // c
    assert t % c == 0 and bsz % g == 0 and row0 % GDN_ROWS == 0
    nb, nt = bsz // g, t // c
    base = row0 // GDN_ROWS
    rows = lambda i, j: (i * nt + j, 0)
    s0r = s0.reshape(nb, g, HB, DKB, DVB).transpose(0, 2, 1, 3, 4).reshape(nb, HB, g * DKB, DVB)
    y, s_out = pl.pallas_call(
        functools.partial(_gdn_body, c=c, g=g),
        grid=(nb, nt),
        in_specs=[pl.BlockSpec((GDN_ROWS, HB * DKB), rows),
                  pl.BlockSpec((GDN_ROWS, HB * DKB), rows),
                  pl.BlockSpec((GDN_ROWS, WB), rows),
                  pl.BlockSpec((GDN_ROWS, LANES), rows),
                  pl.BlockSpec((GDN_ROWS, WB), lambda i, j: (base + i * nt + j, OFF_BZ // WB)),
                  pl.BlockSpec((None, HB, g * DKB, DVB), lambda i, j: (i, 0, 0, 0)),
                  pl.BlockSpec((1, DVB), lambda i, j: (0, 0))],
        out_specs=[pl.BlockSpec((GDN_ROWS, WB), rows),
                   pl.BlockSpec((None, HB, g * DKB, DVB), lambda i, j: (i, 0, 0, 0))],
        out_shape=[jax.ShapeDtypeStruct((bsz * t, WB), F32),
                   jax.ShapeDtypeStruct((nb, HB, g * DKB, DVB), F32)],
        scratch_shapes=[pltpu.VMEM((HB, g * DKB, DVB), F32)],
        compiler_params=_cparams(("parallel", "arbitrary")),
    )(q, k, v, gb, p, s0r, gn)
    s_out = s_out.reshape(nb, HB, g, DKB, DVB).transpose(0, 2, 1, 3, 4).reshape(bsz, HB, DKB, DVB)
    return y, s_out


def _rope(x, cos, s1, s2):
    half = ROPE_DIM // 2
    return x * cos + pltpu.roll(x, LANES - half, 1) * s1 + pltpu.roll(x, half, 1) * s2


def _attnprep_body(p_ref, ci_ref, cos_ref, s1_ref, s2_ref, bd_ref, qn_ref, kn_ref,
                   q_ref, k_ref, v_ref, qi_ref, kiw_ref):
    cos, s1, s2 = cos_ref[...], s1_ref[...], s2_ref[...]
    bd = bd_ref[...]
    nch = WC // LANES
    for c in range(2 * nch):
        x = p_ref[:, c * LANES:(c + 1) * LANES]
        gain = (qn_ref if c < nch else kn_ref)[...]
        n = x * lax.rsqrt(_segsum(x * x, bd) * (1.0 / DH) + EPS) * gain
        rot = _rope(n, cos, s1, s2)
        if c < nch:
            q_ref[:, c * LANES:(c + 1) * LANES] = (rot * (DH ** -0.5)).astype(BF16)
        else:
            k_ref[:, (c - nch) * LANES:(c - nch + 1) * LANES] = rot
    v_ref[...] = p_ref[:, 2 * WC:3 * WC]
    for c in range(HI * DI // LANES):
        x = ci_ref[:, c * LANES:(c + 1) * LANES]
        qi_ref[:, c * LANES:(c + 1) * LANES] = _rope(x, cos, s1, s2).astype(BF16)
    x = ci_ref[:, HI * DI:HI * DI + LANES]
    lane = lax.broadcasted_iota(I32, x.shape, 1)
    is_ki = lane < DI
    kiw_ref[...] = _rope(x, jnp.where(is_ki, cos, 1.0), jnp.where(is_ki, s1, 0.0), jnp.where(is_ki, s2, 0.0))


def _attn_prep(p, row0, m, r, tabs, bd, qn, kn):
    nper = tabs[0].shape[0] // r
    tmap = lambda i: (i % nper, 0)
    assert row0 % r == 0 and m % r == 0 and tabs[0].shape[0] % r == 0
    base = row0 // r
    rows = lambda i: (i, 0)
    return pl.pallas_call(
        _attnprep_body,
        grid=(m // r,),
        in_specs=[pl.BlockSpec((r, 3 * WC), lambda i: (base + i, OFF_CQKV // (3 * WC))),
                  pl.BlockSpec((r, 384), lambda i: (base + i, OFF_CI // 384)),
                  pl.BlockSpec((r, LANES), tmap), pl.BlockSpec((r, LANES), tmap), pl.BlockSpec((r, LANES), tmap),
                  pl.BlockSpec((LANES, LANES), lambda i: (0, 0)),
                  pl.BlockSpec((1, LANES), lambda i: (0, 0)),
                  pl.BlockSpec((1, LANES), lambda i: (0, 0))],
        out_specs=[pl.BlockSpec((r, WC), rows), pl.BlockSpec((r, WC), rows), pl.BlockSpec((r, WC), rows),
                   pl.BlockSpec((r, HI * DI), rows), pl.BlockSpec((r, LANES), rows)],
        out_shape=[jax.ShapeDtypeStruct((m, WC), BF16), jax.ShapeDtypeStruct((m, WC), F32),
                   jax.ShapeDtypeStruct((m, WC), F32), jax.ShapeDtypeStruct((m, HI * DI), BF16),
                   jax.ShapeDtypeStruct((m, LANES), F32)],
        compiler_params=_cparams(("parallel",)),
    )(p, p, *tabs, bd, qn, kn)


def _attnprep_t_body(p_ref, ci_ref, cos_ref, s1_ref, s2_ref, bd_ref, qn_ref, kn_ref,
                     k_ref, v_ref, kiw_ref, kb_ref, kib_ref, qt_ref, qit_ref, wt_ref, vt_ref):
    cos, s1, s2 = cos_ref[...], s1_ref[...], s2_ref[...]
    bd = bd_ref[...]
    nch = WC // LANES
    r = p_ref.shape[0]
    for c in range(2 * nch):
        x = p_ref[:, c * LANES:(c + 1) * LANES]
        gain = (qn_ref if c < nch else kn_ref)[...]
        n = x * lax.rsqrt(_segsum(x * x, bd) * (1.0 / DH) + EPS) * gain
        rot = _rope(n, cos, s1, s2)
        if c < nch:
            qt_ref[c * LANES:(c + 1) * LANES, :] = (rot * (DH ** -0.5)).T.astype(BF16)
        else:
            k_ref[:, (c - nch) * LANES:(c - nch + 1) * LANES] = rot
            kb_ref[:, (c - nch) * LANES:(c - nch + 1) * LANES] = rot.astype(BF16)
    v = p_ref[:, 2 * WC:3 * WC]
    v_ref[...] = v
    kt = vt_ref.shape[-1]
    for t in range(r // kt):
        for c in range(nch):
            vt_ref[t, c * LANES:(c + 1) * LANES, :] = v[t * kt:(t + 1) * kt, c * LANES:(c + 1) * LANES].T.astype(BF16)
    for c in range(HI * DI // LANES):
        x = ci_ref[:, c * LANES:(c + 1) * LANES]
        qit_ref[c * LANES:(c + 1) * LANES, :] = _rope(x, cos, s1, s2).T.astype(BF16)
    x = ci_ref[:, HI * DI:HI * DI + LANES]
    lane = lax.broadcasted_iota(I32, x.shape, 1)
    is_ki = lane < DI
    kiw = _rope(x, jnp.where(is_ki, cos, 1.0), jnp.where(is_ki, s1, 0.0), jnp.where(is_ki, s2, 0.0))
    kiw_ref[...] = kiw
    kib_ref[...] = kiw[:, 0:DI].astype(BF16)
    wt_ref[...] = kiw.T[LANE_W:LANE_W + SUBLANES, :]


def _attn_prep_t(p, bsz, t, r, kt, tabs, bd, qn, kn):
    nt = t // r
    m = bsz * t
    assert t % r == 0 and r % kt == 0 and kt % LANES == 0
    rows = lambda b, i: (b * nt + i, 0)
    tmap = lambda b, i: (i, 0)
    cst = lambda b, i: (0, 0)
    return pl.pallas_call(
        _attnprep_t_body,
        grid=(bsz, nt),
        in_specs=[pl.BlockSpec((r, 3 * WC), lambda b, i: (b * nt + i, OFF_CQKV // (3 * WC))),
                  pl.BlockSpec((r, 384), lambda b, i: (b * nt + i, OFF_CI // 384)),
                  pl.BlockSpec((r, LANES), tmap), pl.BlockSpec((r, LANES), tmap), pl.BlockSpec((r, LANES), tmap),
                  pl.BlockSpec((LANES, LANES), cst),
                  pl.BlockSpec((1, LANES), cst),
                  pl.BlockSpec((1, LANES), cst)],
        out_specs=[pl.BlockSpec((r, WC), rows), pl.BlockSpec((r, WC), rows), pl.BlockSpec((r, LANES), rows),
                   pl.BlockSpec((r, WC), rows), pl.BlockSpec((r, DI), rows),
                   pl.BlockSpec((None, WC, r), lambda b, i: (b, 0, i)),
                   pl.BlockSpec((None, HI * DI, r), lambda b, i: (b, 0, i)),
                   pl.BlockSpec((None, SUBLANES, r), lambda b, i: (b, 0, i)),
                   pl.BlockSpec((None, r // kt, WC, kt), lambda b, i: (b, i, 0, 0))],
        out_shape=[jax.ShapeDtypeStruct((m, WC), F32), jax.ShapeDtypeStruct((m, WC), F32),
                   jax.ShapeDtypeStruct((m, LANES), F32),
                   jax.ShapeDtypeStruct((m, WC), BF16), jax.ShapeDtypeStruct((m, DI), BF16),
                   jax.ShapeDtypeStruct((bsz, WC, t), BF16),
                   jax.ShapeDtypeStruct((bsz, HI * DI, t), BF16),
                   jax.ShapeDtypeStruct((bsz, SUBLANES, t), F32),
                   jax.ShapeDtypeStruct((bsz, t // kt, WC, kt), BF16)],
        compiler_params=_cparams(("parallel", "parallel")),
    )(p, p, *tabs, bd, qn, kn)


def _float_key(sc):
    bits = lax.bitcast_convert_type(sc, I32)
    return bits ^ ((bits >> 31) & 0x7FFFFFFF)


def _key_float(key):
    return lax.bitcast_convert_type(key ^ ((key >> 31) & 0x7FFFFFFF), F32)


def _kth_largest_key(count_ge, topk, smin, smax, nvis):
    lo = _float_key(smin) - 1
    hi = _float_key(smax) + 2
    short = nvis < topk
    vstar = jnp.where(short, INT_MIN, lo).astype(I32)
    done = jnp.where(short | (nvis == topk) | (hi - lo == 1), 1.0, 0.0)

    def cond(st):
        return (st[0] < FLOAT_STEPS + 34) & (st[-1] > 0.0)

    def body(st):
        it, lo, hi, c_lo, c_hi, vstar, done, _ = st
        for k in range(2):
            if k == 0:
                l_lo, l_hi = jnp.log(c_lo), jnp.log(jnp.maximum(c_hi, 0.5))
                frac = (l_lo - math.log(topk)) / (l_lo - l_hi)
            else:
                frac = 0.5
            f_lo, f_hi = _key_float(lo), _key_float(hi)
            by_score = _float_key(f_lo + (f_hi - f_lo) * frac)
            by_key = (lo >> 1) + (hi >> 1) + (lo & hi & 1)
            cand = jnp.clip(jnp.where(it < FLOAT_STEPS, by_score, by_key), lo + 1, hi - 1)
            cand = jnp.where(done > 0.0, vstar, cand)
            cnt = count_ge(cand)
            ge = cnt >= topk
            lo, c_lo = jnp.where(ge, cand, lo), jnp.where(ge, cnt, c_lo)
            hi, c_hi = jnp.where(ge, hi, cand), jnp.where(ge, c_hi, cnt)
            found = jnp.where(cnt == topk, cand, lo)
            stop = (done == 0.0) & ((cnt == topk) | (hi - lo == 1))
            vstar = jnp.where(stop, found, vstar)
            done = jnp.where(stop, 1.0, done)
        return it + 2, lo, hi, c_lo, c_hi, vstar, done, jnp.sum(1.0 - done)

    init = (jnp.int32(0), lo, hi, nvis, jnp.zeros_like(nvis), vstar, done, jnp.sum(1.0 - done))
    return lax.while_loop(cond, body, init)[5]


def _select_tile(key, vstar, need, running, tri, visible):
    eq = key == vstar
    pref = _dot(eq.astype(BF16), tri)
    sel = (key > vstar) | (eq & (running + pref <= need))
    if visible is not None:
        sel = sel & visible
    return sel, running + pref[:, LANES - 1:LANES]


def _attn_prompt_body(qt_ref, qit_ref, wt_ref, k_ref, vt_ref, ki_ref, ltri_ref, o_ref,
                      key_scr, acc_scr, m_scr, l_scr, raw_scr, s_scr, p_scr, a_scr, bias_scr, *, tq, kt, topk):
    qblk = pl.program_id(1)
    nkb = (qblk * tq + tq + kt - 1) // kt
    qpos = qblk * tq + lax.broadcasted_iota(I32, (1, tq), 1)
    krow = lax.broadcasted_iota(I32, (kt, 1), 0)
    w = wt_ref[...] * ((HI * DI) ** -0.5)
    qit = qit_ref[...]
    qi_cat = jnp.concatenate([qit[h * DI:(h + 1) * DI, :] for h in range(HI)], axis=1)

    def key_tile(j):
        return pl.ds(pl.multiple_of(jnp.minimum(j, nkb - 1) * kt, kt), kt)

    def idx_dot(j):
        return _dot(ki_ref[key_tile(j), :], qi_cat)

    npairs_kt = (nkb + 1) // 2

    def fill_tile(j, cur, carry):
        raw_scr[1 - cur] = idx_dot(j + 1)
        sc = None
        for h in range(HI):
            term = w[h:h + 1, :] * jnp.maximum(raw_scr[cur, :, h * tq:(h + 1) * tq], 0.0)
            sc = term if sc is None else sc + term
        visible = (krow + j * kt) <= qpos
        key_scr[j] = jnp.where(visible, _float_key(sc), INT_MIN)
        fold = lambda a: a.reshape(kt // SUBLANES, SUBLANES, tq)
        return (jnp.maximum(carry[0], jnp.max(fold(jnp.where(visible, sc, -jnp.inf)), axis=0)),
                jnp.minimum(carry[1], jnp.min(fold(jnp.where(visible, sc, jnp.inf)), axis=0)))

    def fill(i, carry):
        return fill_tile(2 * i + 1, 1, fill_tile(2 * i, 0, carry))

    raw_scr[0] = idx_dot(0)
    smax, smin = lax.fori_loop(0, npairs_kt, fill, (jnp.full((SUBLANES, tq), -jnp.inf, F32),
                                                    jnp.full((SUBLANES, tq), jnp.inf, F32)))
    smax = jnp.max(smax, axis=0, keepdims=True)
    smin = jnp.min(smin, axis=0, keepdims=True)

    def count_ge(cand):
        def body(j, acc):
            ind = jnp.where(key_scr[j] >= cand, 1.0, 0.0)
            return acc + jnp.sum(ind.reshape(kt // COUNT_ROWS, COUNT_ROWS, tq), axis=0)
        acc = lax.fori_loop(0, nkb, body, jnp.zeros((COUNT_ROWS, tq), F32))
        return jnp.sum(acc, axis=0, keepdims=True)

    vstar = _kth_largest_key(count_ge, topk, smin, smax, (qpos + 1).astype(F32))
    need = topk - count_ge(vstar + 1)

    half =lax.broadcasted_iota(I32, (LANES, tq), 0) // DH
    q_pairs = []
    for c in range(HC // 2):
        qc = qt_ref[c * LANES:(c + 1) * LANES, :].astype(F32)
        q_pairs.append(jnp.concatenate([jnp.where(half == 0, qc, 0.0), jnp.where(half == 1, qc, 0.0)],
                                       axis=1).astype(BF16))
    m_scr[...] = jnp.full_like(m_scr, NEG_BIG)
    l_scr[...] = jnp.zeros_like(l_scr)
    acc_scr[...] = jnp.zeros_like(acc_scr)
    ltri = ltri_ref[...]

    npair = HC // 2

    def store_scores(j, slot):
        kb = k_ref[key_tile(j), :]
        for c in range(npair):
            s_scr[slot, c] = _dot(kb[:, c * LANES:(c + 1) * LANES], q_pairs[c])

    def store_bias(j, slot, running):
        key = key_scr[jnp.minimum(j, nkb - 1)]
        eq = key == vstar
        pref = _dot(ltri, eq.astype(BF16))
        sel = ((key > vstar) | (eq & (running + pref <= need))) & ((krow + j * kt) <= qpos)
        bias_scr[slot] = jnp.where(sel, 0.0, -jnp.inf)
        return running + pref[kt - 1:kt, :]

    def apply_pv(j, slot):
        vtb = vt_ref[jnp.clip(j, 0, nkb - 1)]
        for c in range(npair):
            pv = _dot(vtb[c * LANES:(c + 1) * LANES, :], p_scr[slot, c])
            for hh in range(2):
                h = 2 * c + hh
                acc_scr[h * DH:(h + 1) * DH, :] = (a_scr[slot, h:h + 1, :] * acc_scr[h * DH:(h + 1) * DH, :]
                                                   + pv[hh * DH:(hh + 1) * DH, hh * tq:(hh + 1) * tq])

    def sweep_tile(j, cur, running):
        nxt = 1 - cur
        apply_pv(j - 1, nxt)
        store_scores(j + 1, nxt)
        running = store_bias(j + 1, nxt, running)
        bias = bias_scr[cur]
        for c in range(npair):
            pc = []
            for hh in range(2):
                h = 2 * c + hh
                s = s_scr[cur, c, :, hh * tq:(hh + 1) * tq] + bias
                m_old = m_scr[h:h + 1, :]
                m_new = jnp.maximum(m_old, jnp.max(s, axis=0, keepdims=True))
                p = jnp.exp(s - m_new)
                alpha = jnp.exp(m_old - m_new)
                l_scr[h:h + 1, :] = alpha * l_scr[h:h + 1, :] + jnp.sum(p, axis=0, keepdims=True)
                m_scr[h:h + 1, :] = m_new
                a_scr[cur, h:h + 1, :] = alpha
                pc.append(p.astype(BF16))
            p_scr[cur, c] = jnp.concatenate(pc, axis=1)
        return running

    def sweep(i, running):
        return sweep_tile(2 * i + 1, 1, sweep_tile(2 * i, 0, running))

    store_scores(0, 0)
    p_scr[1] = jnp.zeros_like(p_scr[1])
    a_scr[1] = jnp.ones_like(a_scr[1])
    lax.fori_loop(0, npairs_kt, sweep, store_bias(0, 0, jnp.zeros((1, tq), F32)))
    apply_pv(2 * npairs_kt - 1, 1)
    for c in range(HC // 2):
        ot = jnp.concatenate([acc_scr[h * DH:(h + 1) * DH, :] / l_scr[h:h + 1, :] for h in (2 * c, 2 * c + 1)],
                             axis=0)
        o_ref[:, c * LANES:(c + 1) * LANES] = ot.T


def _attn_prompt(qt, qit, wt, kbf, vt, kibf, ltri, bsz, s_len):
    tq = LANES
    kt = vt.shape[-1]
    nq = s_len // tq
    topk = min(TOPK_MAX, s_len // 4)
    assert s_len % kt == 0 and kt % tq == 0
    k3 = kbf.reshape(bsz, s_len, WC)
    ki3 = kibf.reshape(bsz, s_len, DI)
    return pl.pallas_call(
        functools.partial(_attn_prompt_body, tq=tq, kt=kt, topk=topk),
        grid=(bsz, nq),
        in_specs=[pl.BlockSpec((None, WC, tq), lambda b, i: (b, 0, i)),
                  pl.BlockSpec((None, HI * DI, tq), lambda b, i: (b, 0, i)),
                  pl.BlockSpec((None, SUBLANES, tq), lambda b, i: (b, 0, i)),
                  pl.BlockSpec((None, s_len, WC), lambda b, i: (b, 0, 0)),
                  pl.BlockSpec((None, s_len // kt, WC, kt), lambda b, i: (b, 0, 0, 0)),
                  pl.BlockSpec((None, s_len, DI), lambda b, i: (b, 0, 0)),
                  pl.BlockSpec((kt, kt), lambda b, i: (0, 0))],
        out_specs=pl.BlockSpec((tq, WC), lambda b, i: (b * nq + i, 0)),
        out_shape=jax.ShapeDtypeStruct((bsz * s_len, WC), F32),
        scratch_shapes=[pltpu.VMEM((s_len // kt, kt, tq), I32),
                        pltpu.VMEM((WC, tq), F32),
                        pltpu.VMEM((SUBLANES, tq), F32),
                        pltpu.VMEM((SUBLANES, tq), F32),
                        pltpu.VMEM((2, kt, HI * tq), F32),
                        pltpu.VMEM((2, HC // 2, kt, 2 * tq), F32),
                        pltpu.VMEM((2, HC // 2, kt, 2 * tq), BF16),
                        pltpu.VMEM((2, SUBLANES, tq), F32),
                        pltpu.VMEM((2, kt, tq), F32)],
        compiler_params=_cparams(("parallel", "arbitrary")),
    )(qt, qit, wt, k3, vt, ki3, ltri)


def _attn_sample_body(pt_ref, q_ref, qi_ref, kiw_ref, knew_ref, vnew_ref, tri_ref, *refs, t, npg, topk):
    kidx_refs, k_refs, v_refs = refs[0:npg], refs[npg:2 * npg], refs[2 * npg:3 * npg]
    o_ref, s_scr, kipad_scr, kpad_scr, vpad_scr = refs[3 * npg:]
    nt = npg + 1
    qi = qi_ref[...]
    qi_stack = jnp.concatenate([qi[:, h * DI:(h + 1) * DI] for h in range(HI)], axis=0)
    w = kiw_ref[:, LANE_W:LANE_W + HI] * ((HI * DI) ** -0.5)
    w_stack = jnp.concatenate([w[:, h:h + 1] for h in range(HI)], axis=0)

    def scores(kib):
        rel = jnp.maximum(_dot_nt(qi_stack, kib), 0.0) * w_stack
        sc = rel[0:t]
        for h in range(1, HI):
            sc = sc + rel[h * t:(h + 1) * t]
        return sc

    sc_tiles = [scores(kidx_refs[j][...].astype(BF16)) for j in range(npg)]
    kipad_scr[...] = jnp.zeros_like(kipad_scr)
    kipad_scr[0:t, :] = kiw_ref[:, 0:DI]
    row = lax.broadcasted_iota(I32, (t, LANES), 0)
    vis_new = lax.broadcasted_iota(I32, (t, LANES), 1) <= row
    sc_new = scores(kipad_scr[...].astype(BF16))
    keys = [_float_key(sc) for sc in sc_tiles] + [jnp.where(vis_new, _float_key(sc_new), INT_MIN)]
    mx, mn = jnp.where(vis_new, sc_new, -jnp.inf), jnp.where(vis_new, sc_new, jnp.inf)
    for sc in sc_tiles:
        mx, mn = jnp.maximum(mx, sc), jnp.minimum(mn, sc)
    smax = jnp.max(mx, axis=1, keepdims=True)
    smin = jnp.min(mn, axis=1, keepdims=True)
    nvis = (npg * LANES + 1 + row[:, 0:1]).astype(F32)

    def count_ge(cand):
        acc = jnp.zeros((t, LANES), F32)
        for key in keys:
            acc = acc + jnp.where(key >= cand, 1.0, 0.0)
        return jnp.sum(acc, axis=1, keepdims=True)

    vstar = _kth_largest_key(count_ge, topk, smin, smax, nvis)
    need = topk - count_ge(vstar + 1)

    kpad_scr[...] = jnp.zeros_like(kpad_scr)
    vpad_scr[...] = jnp.zeros_like(vpad_scr)
    kpad_scr[0:t, :] = knew_ref[...]
    vpad_scr[0:t, :] = vnew_ref[...]
    head_of_lane = lax.broadcasted_iota(I32, (t, WC), 1) // DH
    q = q_ref[...].astype(F32)
    q_stack = jnp.concatenate([jnp.where(head_of_lane == h, q, 0.0) for h in range(HC)], axis=0).astype(BF16)
    tri = tri_ref[...]
    running = jnp.zeros((t, 1), F32)
    m_acc = jnp.full((HC * t, LANES), NEG_BIG, F32)
    for j in range(nt):
        sel, running = _select_tile(keys[j], vstar, need, running, tri, vis_new if j == npg else None)
        bias = jnp.where(sel, 0.0, -jnp.inf)
        kb = (k_refs[j] if j < npg else kpad_scr)[...].astype(BF16)
        s = _dot_nt(q_stack, kb) + jnp.concatenate([bias] * HC, axis=0)
        s_scr[j] = s
        m_acc = jnp.maximum(m_acc, s)
    m = jnp.max(m_acc, axis=1, keepdims=True)
    l_acc = jnp.zeros((HC * t, LANES), F32)
    acc = jnp.zeros((HC * t, WC), F32)
    for j in range(nt):
        p = jnp.exp(s_scr[j] - m)
        l_acc = l_acc + p
        vb = (v_refs[j] if j < npg else vpad_scr)[...].astype(BF16)
        acc = acc + _dot(p.astype(BF16), vb)
    o = acc / jnp.sum(l_acc, axis=1, keepdims=True)
    out = jnp.where(head_of_lane == 0, o[0:t], 0.0)
    for h in range(1, HC):
        out = out + jnp.where(head_of_lane == h, o[h * t:(h + 1) * t], 0.0)
    o_ref[...] = out


def _attn_sample(page_table, q, qi, kiw, knew, vnew, tri, cache_kidx, cache_k, cache_v, layer, bsz, t):
    npg = page_table.shape[1]
    page = cache_k.shape[2]
    assert page == LANES
    topk = min(TOPK_MAX, (npg * page + t) // 4)
    row = lambda b, pt: (b, 0)

    def page_map(j):
        return lambda b, pt: (layer, pt[b, j], 0, 0)

    in_specs = [pl.BlockSpec((t, WC), row), pl.BlockSpec((t, HI * DI), row), pl.BlockSpec((t, LANES), row),
                pl.BlockSpec((t, WC), row), pl.BlockSpec((t, WC), row),
                pl.BlockSpec((LANES, LANES), lambda b, pt: (0, 0))]
    in_specs += [pl.BlockSpec((None, None, page, DI), page_map(j)) for j in range(npg)]
    in_specs += [pl.BlockSpec((None, None, page, WC), page_map(j)) for j in range(npg)]
    in_specs += [pl.BlockSpec((None, None, page, WC), page_map(j)) for j in range(npg)]
    grid_spec = pltpu.PrefetchScalarGridSpec(
        num_scalar_prefetch=1,
        grid=(bsz,),
        in_specs=in_specs,
        out_specs=pl.BlockSpec((t, WC), row),
        scratch_shapes=[pltpu.VMEM((npg + 1, HC * t, LANES), F32), pltpu.VMEM((LANES, DI), F32),
                        pltpu.VMEM((LANES, WC), F32), pltpu.VMEM((LANES, WC), F32)])
    return pl.pallas_call(
        functools.partial(_attn_sample_body, t=t, npg=npg, topk=topk),
        grid_spec=grid_spec,
        out_shape=jax.ShapeDtypeStruct((bsz * t, WC), F32),
        compiler_params=_cparams(("parallel",)),
    )(page_table, q, qi, kiw, knew, vnew, tri,
      *([cache_kidx] * npg), *([cache_k] * npg), *([cache_v] * npg))


def _rope_tables(pos):
    half = ROPE_DIM // 2
    inv_freq = ROPE_THETA ** (-jnp.arange(half, dtype=F32) / half)
    ang = pos.astype(F32)[:, None] * inv_freq[None, :]
    cos, sin = jnp.cos(ang), jnp.sin(ang)
    n = pos.shape[0]
    one = jnp.ones((n, DH - ROPE_DIM), F32)
    zero_h = jnp.zeros((n, half), F32)
    zero_r = jnp.zeros((n, DH - ROPE_DIM), F32)
    c64 = jnp.concatenate([cos, cos, one], axis=1)
    s1 = jnp.concatenate([-sin, zero_h, zero_r], axis=1)
    s2 = jnp.concatenate([zero_h, sin, zero_r], axis=1)
    rep = LANES // DH
    return tuple(jnp.tile(a, (1, rep)) for a in (c64, s1, s2))


def _reorder_w_in(w_in):
    sizes = (WA, WA, WA, QKV_B, WB, HB, HB, 3 * WC, HI * DI, DI, HI)
    offs = [0]
    for s in sizes:
        offs.append(offs[-1] + s)
    seg = lambda i: w_in[:, :, offs[i]:offs[i + 1]]
    pad = jnp.zeros(w_in.shape[:2] + (LANES - DI - HI - 2 * HB,), w_in.dtype)
    order = [seg(3), seg(7), seg(0), seg(1), seg(2), seg(4), seg(8), seg(9), seg(10), seg(5), seg(6), pad]
    return jnp.concatenate(order, axis=-1).astype(BF16)


def _pad_state(prev):
    return jnp.pad(prev, ((0, 0), (SUBLANES - prev.shape[1], 0), (0, 0)))


def kernel(x_prompt, x_sample, state_conv_a, state_gdn_conv, state_gdn, cache_k, cache_v, cache_kidx, page_table, ffn1_norm, ffn1_w_gate, ffn1_w_up, ffn1_w_down, mix_norm, w_in, conv_a_w, gdn_conv_w, gdn_a_log, gdn_dt_bias, gdn_out_norm, attn_q_norm, attn_k_norm, w_out, ffn2_norm, ffn2_w_gate, ffn2_w_up, ffn2_w_down):
    depth = w_in.shape[0]
    bp, tp, d = x_prompt.shape
    bs, ts, _ = x_sample.shape
    mp, ms = bp * tp, bs * ts
    npool, page = cache_k.shape[1], cache_k.shape[2]
    past = page_table.shape[1] * page
    assert ts == SUBLANES and tp % LANES == 0 and mp % 256 == 0

    x = jnp.concatenate([x_prompt.reshape(mp, d), x_sample.reshape(ms, d)], axis=0)
    w_in_r = _reorder_w_in(w_in)
    bf = lambda a: a.astype(BF16)
    f1g, f1u, f1d = bf(ffn1_w_gate), bf(ffn1_w_up), bf(ffn1_w_down)
    f2g, f2u, f2d = bf(ffn2_w_gate), bf(ffn2_w_up), bf(ffn2_w_down)
    w_out_b = bf(w_out)
    ck = bf(cache_k.reshape(depth, npool, page, WC))
    cv = bf(cache_v.reshape(depth, npool, page, WC))

    li = jnp.arange(LANES)
    bd = ((li[:, None] // DH) == (li[None, :] // DH)).astype(BF16)
    tri = (li[:, None] <= li[None, :]).astype(BF16)
    kt = r_p = _pick(tp, (256, 128))
    lk = jnp.arange(kt)
    ltri = (lk[None, :] <= lk[:, None]).astype(BF16)
    r_s =_pick(ms, (128, 64, 32, 16, 8))
    tabs_p = _rope_tables(jnp.arange(tp))
    tabs_s = _rope_tables(jnp.tile(past + jnp.arange(ts), r_s // ts))
    zero_a = jnp.zeros((bp, SUBLANES, WA), F32)
    zero_b = jnp.zeros((bp, SUBLANES, QKV_B), F32)
    zero_s = jnp.zeros((bp, HB, DKB, DVB), F32)
    lane_pad = lambda v, off: jnp.zeros((1, LANES), F32).at[0, off:off + v.shape[0]].set(v)

    outs = [[] for _ in range(12)]
    for l in range(depth):
        x = _ffn(x, ffn1_norm[l][None], f1g[l], f1u[l], f1d[l])
        p = _inproj(x, mix_norm[l][None], w_in_r[l])
        avec = lane_pad(jnp.exp(gdn_a_log[l].astype(F32)), LANE_GA)
        dtb = lane_pad(gdn_dt_bias[l].astype(F32), LANE_GA)
        gn = gdn_out_norm[l][None].astype(F32)
        qn = jnp.tile(attn_q_norm[l].astype(F32), LANES // DH)[None]
        kn = jnp.tile(attn_k_norm[l].astype(F32), LANES // DH)[None]

        ya, yb, yc = [], [], []
        groups = ((0, bp, tp, zero_a, zero_b, zero_s), (mp, bs, ts, _pad_state(state_conv_a[l]),
                                                       _pad_state(state_gdn_conv[l]), state_gdn[l]))
        for gi, (row0, bsz, t, prev_a, prev_b, s0) in enumerate(groups):
            m = bsz * t
            y_a, st_a = _mixer_a(p, row0, bsz, t, conv_a_w[l], prev_a)
            qg, kg, vg, gbg, st_b = _gdn_prep(p, row0, bsz, t, gdn_conv_w[l], prev_b, bd, avec, dtb)
            y_b, s_new = _gdn(qg, kg, vg, gbg, p, row0, bsz, t, s0, gn)
            if gi == 0:
                kc, vc, kiw, kbf, kibf, qt, qit, wt, vt = _attn_prep_t(p, bsz, t, r_p, kt, tabs_p, bd, qn, kn)
                y_c = _attn_prompt(qt, qit, wt, kbf, vt, kibf, ltri, bsz, t)
            else:
                qc, kc, vc, qi, kiw = _attn_prep(p, row0, m, r_s, tabs_s, bd, qn, kn)
                y_c = _attn_sample(page_table, qc, qi, kiw, kc, vc, tri, cache_kidx, ck, cv, l, bsz, t)
            ya.append(y_a)
            yb.append(y_b)
            yc.append(y_c)
            res = (st_a[:, SUBLANES - (CONV_A - 1):], st_b[:, SUBLANES - (GDN_CONV - 1):], s_new,
                   kc.reshape(bsz, t, HC, DH), vc.reshape(bsz, t, HC, DH), kiw[:, :DI].reshape(bsz, t, DI))
            for i, r in enumerate(res):
                outs[gi * 6 + i].append(r)
        x = _outproj(x, jnp.concatenate(ya), jnp.concatenate(yb), jnp.concatenate(yc), w_out_b[l])
        x = _ffn(x, ffn2_norm[l][None], f2g[l], f2u[l], f2d[l])

    y_prompt = x[:mp].reshape(bp, tp, d)
    y_sample = x[mp:].reshape(bs, ts, d)
    return (y_prompt, y_sample) + tuple(jnp.stack(o) for o in outs)
```

```python
import functools
import math

import jax
import jax.numpy as jnp
from jax import lax
from jax.experimental import pallas as pl
from jax.experimental.pallas import tpu as pltpu

F32 = jnp.float32
BF16 = jnp.bfloat16
I32 = jnp.int32

EPS = 1e-6
LANES = 128
SUBLANES = 8
VMEM_LIMIT = 56 * 1024 * 1024

A_GROUPS = 4
CONV_A = 3
HB = 6
DKB = 64
DVB = 64
GDN_CONV = 4
GDN_ROWS = 64
HC = 6
DH = 64
HI = 4
DI = 64
TOPK_MAX = 256
ROPE_DIM = DH // 4
ROPE_THETA = 500000.0
INT_MIN = -2 ** 31
COUNT_ROWS = 4 * SUBLANES
NEG_BIG = -1e30

WA = 256
QKV_B = 2 * HB * DKB + HB * DVB
WB = HB * DVB
WC = HC * DH
OFF_BQKV = 0
OFF_CQKV = QKV_B
OFF_A = 2 * QKV_B
OFF_BZ = OFF_A + 3 * WA
OFF_CI = OFF_BZ + WB
N_P = OFF_CI + 384
LANE_W = DI
LANE_GA = DI + HI
LANE_GB = DI + HI + HB


def _pick(n, cands):
    for c in cands:
        if n % c == 0:
            return c
    raise ValueError(f"no tile for {n}")


def _cparams(sem):
    return pltpu.CompilerParams(dimension_semantics=sem, vmem_limit_bytes=VMEM_LIMIT)


def _dot(a, b):
    return jnp.dot(a, b, preferred_element_type=F32)


def _dot_nt(a, b):
    return lax.dot_general(a, b, (((1,), (1,)), ((), ())), preferred_element_type=F32)


def _dot_hi(a, b):
    return jnp.dot(a, b, preferred_element_type=F32, precision=lax.Precision.HIGHEST)


def _dot_nt_hi(a, b):
    return lax.dot_general(a, b, (((1,), (1,)), ((), ())), preferred_element_type=F32,
                           precision=lax.Precision.HIGHEST)


def _dot3(a, b):
    ah = a.astype(BF16)
    al = (a - ah.astype(F32)).astype(BF16)
    bh = b.astype(BF16)
    bl = (b - bh.astype(F32)).astype(BF16)
    return _dot(ah, bh) + _dot(ah, bl) + _dot(al, bh)


def _segsum(x, bd):
    hi = x.astype(BF16)
    lo = (x - hi.astype(F32)).astype(BF16)
    return _dot(hi, bd) + _dot(lo, bd)


def _sigmoid(x):
    return 1.0 / (1.0 + jnp.exp(-x))


def _silu(x):
    return x * _sigmoid(x)


def _softplus(x):
    return jnp.maximum(x, 0.0) + jnp.log(1.0 + jnp.exp(-jnp.abs(x)))


def _ffn_body(x_ref, g_ref, wg_ref, wu_ref, wd_ref, o_ref, h_ref, acc_ref):
    j = pl.program_id(1)

    @pl.when(j == 0)
    def _():
        x = x_ref[...]
        ms = jnp.mean(x * x, axis=-1, keepdims=True)
        h_ref[...] = (x * lax.rsqrt(ms + EPS) * g_ref[...]).astype(BF16)
        acc_ref[...] = jnp.zeros_like(acc_ref)

    h = h_ref[...]
    a = _dot(h, wg_ref[...])
    b = _dot(h, wu_ref[...])
    t = (_silu(a) * b).astype(BF16)
    acc_ref[...] += _dot(t, wd_ref[...])

    @pl.when(j == pl.num_programs(1) - 1)
    def _():
        o_ref[...] = x_ref[...] + 0.5 * acc_ref[...]


def _ffn(x, g, wg, wu, wd):
    m, d = x.shape
    f = wg.shape[1]
    tm = _pick(m, (512, 256, 128, 64, 32, 16, 8))
    tf = _pick(f, (1408, 256, 128))
    return pl.pallas_call(
        _ffn_body,
        grid=(m // tm, f // tf),
        in_specs=[pl.BlockSpec((tm, d), lambda i, j: (i, 0)),
                  pl.BlockSpec((1, d), lambda i, j: (0, 0)),
                  pl.BlockSpec((d, tf), lambda i, j: (0, j)),
                  pl.BlockSpec((d, tf), lambda i, j: (0, j)),
                  pl.BlockSpec((tf, d), lambda i, j: (j, 0))],
        out_specs=pl.BlockSpec((tm, d), lambda i, j: (i, 0)),
        out_shape=jax.ShapeDtypeStruct((m, d), F32),
        scratch_shapes=[pltpu.VMEM((tm, d), BF16), pltpu.VMEM((tm, d), F32)],
        compiler_params=_cparams(("parallel", "arbitrary")),
    )(x, g, wg, wu, wd)


def _inproj_body(x_ref, g_ref, w_ref, o_ref, h_ref):
    @pl.when(pl.program_id(1) == 0)
    def _():
        x = x_ref[...]
        ms = jnp.mean(x * x, axis=-1, keepdims=True)
        h_ref[...] = (x * lax.rsqrt(ms + EPS) * g_ref[...]).astype(BF16)

    o_ref[...] = _dot(h_ref[...], w_ref[...])


def _inproj(x, g, w):
    m, d = x.shape
    n = w.shape[1]
    tm = _pick(m, (512, 256, 128, 64, 32, 16, 8))
    tn = _pick(n, (1920, 768, 384, 128))
    return pl.pallas_call(
        _inproj_body,
        grid=(m // tm, n // tn),
        in_specs=[pl.BlockSpec((tm, d), lambda i, j: (i, 0)),
                  pl.BlockSpec((1, d), lambda i, j: (0, 0)),
                  pl.BlockSpec((d, tn), lambda i, j: (0, j))],
        out_specs=pl.BlockSpec((tm, tn), lambda i, j: (i, j)),
        out_shape=jax.ShapeDtypeStruct((m, n), F32),
        scratch_shapes=[pltpu.VMEM((tm, d), BF16)],
        compiler_params=_cparams(("parallel", "arbitrary")),
    )(x, g, w)


def _outproj_body(x_ref, ya_ref, yb_ref, yc_ref, w_ref, o_ref):
    acc = _dot(ya_ref[...].astype(BF16), w_ref[0:WA, :])
    acc += _dot(yb_ref[...].astype(BF16), w_ref[WA:WA + WB, :])
    acc += _dot(yc_ref[...].astype(BF16), w_ref[WA + WB:WA + WB + WC, :])
    o_ref[...] = x_ref[...] + acc


def _outproj(x, ya, yb, yc, w):
    m, d = x.shape
    tm = _pick(m, (512, 256, 128, 64, 32, 16, 8))
    return pl.pallas_call(
        _outproj_body,
        grid=(m // tm,),
        in_specs=[pl.BlockSpec((tm, d), lambda i: (i, 0)),
                  pl.BlockSpec((tm, WA), lambda i: (i, 0)),
                  pl.BlockSpec((tm, WB), lambda i: (i, 0)),
                  pl.BlockSpec((tm, WC), lambda i: (i, 0)),
                  pl.BlockSpec(w.shape, lambda i: (0, 0))],
        out_specs=pl.BlockSpec((tm, d), lambda i: (i, 0)),
        out_shape=jax.ShapeDtypeStruct((m, d), F32),
        compiler_params=_cparams(("parallel",)),
    )(x, ya, yb, yc, w)


def _conv_taps(u, w_ref, prev_ref, ext_ref, tt, width):
    it = pl.program_id(1)

    @pl.when(it == 0)
    def _():
        ext_ref[:, 0:SUBLANES, :] = prev_ref[...]

    @pl.when(it > 0)
    def _():
        ext_ref[:, 0:SUBLANES, :] = ext_ref[:, tt:tt + SUBLANES, :]

    ext_ref[:, SUBLANES:SUBLANES + tt, :] = u
    acc = None
    for s in range(width):
        term = ext_ref[:, SUBLANES - s:SUBLANES - s + tt, :] * w_ref[width - 1 - s:width - s, :]
        acc = term if acc is None else acc + term
    return acc, ext_ref[:, tt:tt + SUBLANES, :]


def _group_tiles(bsz, t):
    if t >= 128:
        return 1, _pick(t, (256, 128))
    return _pick(bsz, (16, 8, 4, 2, 1)), t


def _mixa_body(p_ref, w_ref, prev_ref, y_ref, st_ref, ext_ref, *, bb, tt):
    pa = p_ref[...]
    u = (pa[:, WA:2 * WA] * pa[:, 2 * WA:3 * WA]).reshape(bb, tt, WA)
    conv, tail = _conv_taps(u, w_ref, prev_ref, ext_ref, tt, CONV_A)
    y_ref[...] = pa[:, 0:WA] * conv.reshape(bb * tt, WA)
    st_ref[...] = tail


def _mixer_a(p, row0, bsz, t, w, prev8):
    bb, tt = _group_tiles(bsz, t)
    r = bb * tt
    nb, nt = bsz // bb, t // tt
    base = row0 // r
    assert row0 % r == 0
    return pl.pallas_call(
        functools.partial(_mixa_body, bb=bb, tt=tt),
        grid=(nb, nt),
        in_specs=[pl.BlockSpec((r, 3 * WA), lambda i, j: (base + i * nt + j, OFF_A // (3 * WA))),
                  pl.BlockSpec(w.shape, lambda i, j: (0, 0)),
                  pl.BlockSpec((bb, SUBLANES, WA), lambda i, j: (i, 0, 0))],
        out_specs=[pl.BlockSpec((r, WA), lambda i, j: (i * nt + j, 0)),
                   pl.BlockSpec((bb, SUBLANES, WA), lambda i, j: (i, 0, 0))],
        out_shape=[jax.ShapeDtypeStruct((bsz * t, WA), F32),
                   jax.ShapeDtypeStruct((bsz, SUBLANES, WA), F32)],
        scratch_shapes=[pltpu.VMEM((bb, tt + SUBLANES, WA), F32)],
        compiler_params=_cparams(("parallel", "arbitrary")),
    )(p, w, prev8)


def _gdnprep_body(p_ref, ci_ref, w_ref, prev_ref, bd_ref, avec_ref, dtb_ref,
                  q_ref, k_ref, v_ref, gb_ref, st_ref, ext_ref, *, bb, tt):
    r = bb * tt
    x = p_ref[...].reshape(bb, tt, QKV_B)
    conv, tail = _conv_taps(x, w_ref, prev_ref, ext_ref, tt, GDN_CONV)
    st_ref[...] = tail
    a = _silu(conv.reshape(r, QKV_B))
    bd = bd_ref[...]
    nq = HB * DKB // LANES
    for c in range(2 * nq):
        xc = a[:, c * LANES:(c + 1) * LANES]
        n = xc * lax.rsqrt(_segsum(xc * xc, bd) + EPS)
        if c < nq:
            q_ref[:, c * LANES:(c + 1) * LANES] = n * (DKB ** -0.5)
        else:
            k_ref[:, (c - nq) * LANES:(c - nq + 1) * LANES] = n
    v_ref[...] = a[:, 2 * HB * DKB:]
    raw = ci_ref[...]
    g = -avec_ref[...] * _softplus(raw + dtb_ref[...])
    lane = lax.broadcasted_iota(I32, raw.shape, 1)
    gb_ref[...] = jnp.where((lane >= LANE_GB) & (lane < LANE_GB + HB), _sigmoid(raw), g)


def _gdn_prep(p, row0, bsz, t, w, prev8, bd, avec, dtb):
    bb, tt = _group_tiles(bsz, t)
    r = bb * tt
    nb, nt = bsz // bb, t // tt
    base = row0 // r
    assert row0 % r == 0
    m = bsz * t
    rows = lambda i, j: (i * nt + j, 0)
    return pl.pallas_call(
        functools.partial(_gdnprep_body, bb=bb, tt=tt),
        grid=(nb, nt),
        in_specs=[pl.BlockSpec((r, QKV_B), lambda i, j: (base + i * nt + j, OFF_BQKV // QKV_B)),
                  pl.BlockSpec((r, LANES), lambda i, j: (base + i * nt + j, (OFF_CI + 2 * LANES) // LANES)),
                  pl.BlockSpec(w.shape, lambda i, j: (0, 0)),
                  pl.BlockSpec((bb, SUBLANES, QKV_B), lambda i, j: (i, 0, 0)),
                  pl.BlockSpec((LANES, LANES), lambda i, j: (0, 0)),
                  pl.BlockSpec((1, LANES), lambda i, j: (0, 0)),
                  pl.BlockSpec((1, LANES), lambda i, j: (0, 0))],
        out_specs=[pl.BlockSpec((r, HB * DKB), rows),
                   pl.BlockSpec((r, HB * DKB), rows),
                   pl.BlockSpec((r, WB), rows),
                   pl.BlockSpec((r, LANES), rows),
                   pl.BlockSpec((bb, SUBLANES, QKV_B), lambda i, j: (i, 0, 0))],
        out_shape=[jax.ShapeDtypeStruct((m, HB * DKB), F32),
                   jax.ShapeDtypeStruct((m, HB * DKB), F32),
                   jax.ShapeDtypeStruct((m, WB), F32),
                   jax.ShapeDtypeStruct((m, LANES), F32),
                   jax.ShapeDtypeStruct((bsz, SUBLANES, QKV_B), F32)],
        scratch_shapes=[pltpu.VMEM((bb, tt + SUBLANES, QKV_B), F32)],
        compiler_params=_cparams(("parallel", "arbitrary")),
    )(p, p, w, prev8, bd, avec, dtb)


def _gdn_body(q_ref, k_ref, v_ref, gb_ref, z_ref, s0_ref, gn_ref, y_ref, sout_ref, s_scr, *, c, g, nch):
    r = 2 * GDN_ROWS
    n = 2 * g
    it = pl.program_id(1)

    @pl.when(it == 0)
    def _():
        s_scr[...] = s0_ref[...]

    ri = lax.broadcasted_iota(I32, (r, r), 0)
    ci = lax.broadcasted_iota(I32, (r, r), 1)
    same = (ri // c) == (ci // c)
    incl = same & (ci <= ri)
    strict = same & (ci < ri)
    eye = (ri == ci).astype(F32)
    lastsel = (ci == (ri // c) * c + (c - 1)).astype(F32)
    chunks = range(nch)
    rows_of = lambda ch: slice(ch * GDN_ROWS, (ch + 1) * GDN_ROWS)
    gb = [jnp.concatenate([gb_ref[rows_of(ch), :]] * 2, axis=0) for ch in chunks]
    gcum = [_dot_hi(incl.astype(F32), gb[ch]) for ch in chunks]
    glast = [_dot_hi(lastsel, gcum[ch]) for ch in chunks]
    lane = lax.broadcasted_iota(I32, (r, LANES), 1)
    lane0 = (lane == 0).astype(F32)
    upper = lax.broadcasted_iota(I32, (r, 1), 0) < GDN_ROWS
    er = lax.broadcasted_iota(I32, (r, n * DKB), 0)
    ec = lax.broadcasted_iota(I32, (r, n * DKB), 1)
    emask = (er // c) == (ec // DKB)
    tr = lax.broadcasted_iota(I32, (DKB, n * DKB), 0)
    tc = lax.broadcasted_iota(I32, (DKB, n * DKB), 1)
    tile_mat = ((tc % DKB) == tr).astype(BF16)
    xr = lax.broadcasted_iota(I32, (n * DKB, r), 0)
    xc = lax.broadcasted_iota(I32, (n * DKB, r), 1)
    emask_t = (xr // DKB) == (xc // c)
    sr = lax.broadcasted_iota(I32, (n * DKB, DKB), 0)
    sc = lax.broadcasted_iota(I32, (n * DKB, DKB), 1)
    tile_mat_t = ((sr % DKB) == sc).astype(BF16)
    esel = (xc == (xr // DKB) * c + (c - 1)).astype(F32)
    gend = [_dot_hi(esel, gcum[ch]) for ch in chunks]
    s_upper = lax.broadcasted_iota(I32, (n * DKB, 1), 0) < g * DKB
    n_sq = max(0, int(math.ceil(math.log2(c))) - 1)

    npair = HB // 2
    items = [(ch, i) for ch in chunks for i in range(npair)]
    each = lambda f: [f(n_, ch, i) for n_, (ch, i) in enumerate(items)]

    def stack(ref, width):
        return each(lambda n_, ch, i: jnp.concatenate([ref[rows_of(ch), 2 * i * width:(2 * i + 1) * width],
                                                       ref[rows_of(ch), (2 * i + 1) * width:(2 * i + 2) * width]], axis=0))

    def pick(mats, rows_upper, base):
        return each(lambda n_, ch, i: jnp.where(rows_upper, mats[ch][:, base + 2 * i:base + 2 * i + 1],
                                                mats[ch][:, base + 2 * i + 1:base + 2 * i + 2]))

    q, k, v, zz = stack(q_ref, DKB), stack(k_ref, DKB), stack(v_ref, DVB), stack(z_ref, DVB)
    gcol = pick(gcum, upper, LANE_GA)
    glc = pick(glast, upper, LANE_GA)
    beta = pick(gb, upper, LANE_GB)
    srow = [jnp.exp(x) for x in pick(gend, s_upper, LANE_GA)]
    grow = each(lambda n_, ch, i: _dot_nt_hi(lane0, jnp.where(lane == 0, gcol[n_], 0.0)))
    decay = each(lambda n_, ch, i: jnp.exp(jnp.where(incl, gcol[n_] - grow[n_], -jnp.inf)))
    kb = each(lambda n_, ch, i: k[n_] * beta[n_])
    a_mat = each(lambda n_, ch, i: _dot_nt(kb[n_], k[n_]) * jnp.where(strict, decay[n_], 0.0))
    pw = [-a for a in a_mat]
    t_mat = [eye + x for x in pw]
    for _ in range(n_sq):
        pw = [_dot3(x, x) for x in pw]
        t_mat = each(lambda n_, ch, i: t_mat[n_] + _dot3(t_mat[n_], pw[n_]))
    u = each(lambda n_, ch, i: _dot(t_mat[n_], v[n_] * beta[n_]))
    wk = each(lambda n_, ch, i: _dot(t_mat[n_], kb[n_] * jnp.exp(gcol[n_])))
    qk = each(lambda n_, ch, i: _dot_nt(q[n_], k[n_]) * decay[n_])
    wk_e = each(lambda n_, ch, i: jnp.where(emask, _dot(wk[n_].astype(BF16), tile_mat), 0.0))
    qg_e = each(lambda n_, ch, i: jnp.where(emask, _dot((q[n_] * jnp.exp(gcol[n_])).astype(BF16), tile_mat), 0.0))
    kd_t = each(lambda n_, ch, i: jnp.where(
        emask_t, _dot_nt(tile_mat_t, (k[n_] * jnp.exp(glc[n_] - gcol[n_])).astype(BF16)), 0.0))
    s = [s_scr[2 * i:2 * i + 2].reshape(n * DKB, DVB) for i in range(npair)]
    for ch in chunks:
        sel = [ch * npair + i for i in range(npair)]
        v_new = [u[n_] - _dot(wk_e[n_], s[i]) for i, n_ in enumerate(sel)]
        out = [_dot(qg_e[n_], s[i]) + _dot(qk[n_], v_new[i]) for i, n_ in enumerate(sel)]
        s = [s[i] * srow[n_] + _dot(kd_t[n_], v_new[i]) for i, n_ in enumerate(sel)]
        for i, n_ in enumerate(sel):
            ms = jnp.mean(out[i] * out[i], axis=-1, keepdims=True)
            y = out[i] * lax.rsqrt(ms + EPS) * gn_ref[...] * _silu(zz[n_])
            y_ref[rows_of(ch), 2 * i * DVB:(2 * i + 1) * DVB] = y[0:GDN_ROWS]
            y_ref[rows_of(ch), (2 * i + 1) * DVB:(2 * i + 2) * DVB] = y[GDN_ROWS:]
    for i in range(npair):
        s_scr[2 * i:2 * i + 2] = s[i].reshape(2, g * DKB, DVB)

    @pl.when(it == pl.num_programs(1) - 1)
    def _():
        sout_ref[...] = s_scr[...]


def _gdn(q, k, v, gb, p, row0, bsz, t, s0, gn):
    c = min(GDN_ROWS, t)
    g = GDN_ROWS // c
    nch = 2 if (g == 1 and (t // c) % 2 == 0) else 1
    rb = nch * GDN_ROWS
    assert t % (c * nch) == 0 and bsz % g == 0 and row0 % rb == 0
    nb, nt = bsz // g, t // (c * nch)
    base = row0 // rb
    rows = lambda i, j: (i * nt + j, 0)
    s0r = s0.reshape(nb, g, HB, DKB, DVB).transpose(0, 2, 1, 3, 4).reshape(nb, HB, g * DKB, DVB)
    y, s_out = pl.pallas_call(
        functools.partial(_gdn_body, c=c, g=g, nch=nch),
        grid=(nb, nt),
        in_specs=[pl.BlockSpec((rb, HB * DKB), rows),
                  pl.BlockSpec((rb, HB * DKB), rows),
                  pl.BlockSpec((rb, WB), rows),
                  pl.BlockSpec((rb, LANES), rows),
                  pl.BlockSpec((rb, WB), lambda i, j: (base + i * nt + j, OFF_BZ // WB)),
                  pl.BlockSpec((None, HB, g * DKB, DVB), lambda i, j: (i, 0, 0, 0)),
                  pl.BlockSpec((1, DVB), lambda i, j: (0, 0))],
        out_specs=[pl.BlockSpec((rb, WB), rows),
                   pl.BlockSpec((None, HB, g * DKB, DVB), lambda i, j: (i, 0, 0, 0))],
        out_shape=[jax.ShapeDtypeStruct((bsz * t, WB), F32),
                   jax.ShapeDtypeStruct((nb, HB, g * DKB, DVB), F32)],
        scratch_shapes=[pltpu.VMEM((HB, g * DKB, DVB), F32)],
        compiler_params=_cparams(("parallel", "arbitrary")),
    )(q, k, v, gb, p, s0r, gn)
    s_out = s_out.reshape(nb, HB, g, DKB, DVB).transpose(0, 2, 1, 3, 4).reshape(bsz, HB, DKB, DVB)
    return y, s_out


def _rope(x, cos, s1, s2):
    half = ROPE_DIM // 2
    return x * cos + pltpu.roll(x, LANES - half, 1) * s1 + pltpu.roll(x, half, 1) * s2


def _attnprep_body(p_ref, ci_ref, cos_ref, s1_ref, s2_ref, bd_ref, qn_ref, kn_ref,
                   q_ref, k_ref, v_ref, qi_ref, kiw_ref):
    cos, s1, s2 = cos_ref[...], s1_ref[...], s2_ref[...]
    bd = bd_ref[...]
    nch = WC // LANES
    for c in range(2 * nch):
        x = p_ref[:, c * LANES:(c + 1) * LANES]
        gain = (qn_ref if c < nch else kn_ref)[...]
        n = x * lax.rsqrt(_segsum(x * x, bd) * (1.0 / DH) + EPS) * gain
        rot = _rope(n, cos, s1, s2)
        if c < nch:
            q_ref[:, c * LANES:(c + 1) * LANES] = (rot * (DH ** -0.5)).astype(BF16)
        else:
            k_ref[:, (c - nch) * LANES:(c - nch + 1) * LANES] = rot
    v_ref[...] = p_ref[:, 2 * WC:3 * WC]
    for c in range(HI * DI // LANES):
        x = ci_ref[:, c * LANES:(c + 1) * LANES]
        qi_ref[:, c * LANES:(c + 1) * LANES] = _rope(x, cos, s1, s2).astype(BF16)
    x = ci_ref[:, HI * DI:HI * DI + LANES]
    lane = lax.broadcasted_iota(I32, x.shape, 1)
    is_ki = lane < DI
    kiw_ref[...] = _rope(x, jnp.where(is_ki, cos, 1.0), jnp.where(is_ki, s1, 0.0), jnp.where(is_ki, s2, 0.0))


def _attn_prep(p, row0, m, r, tabs, bd, qn, kn):
    nper = tabs[0].shape[0] // r
    tmap = lambda i: (i % nper, 0)
    assert row0 % r == 0 and m % r == 0 and tabs[0].shape[0] % r == 0
    base = row0 // r
    rows = lambda i: (i, 0)
    return pl.pallas_call(
        _attnprep_body,
        grid=(m // r,),
        in_specs=[pl.BlockSpec((r, 3 * WC), lambda i: (base + i, OFF_CQKV // (3 * WC))),
                  pl.BlockSpec((r, 384), lambda i: (base + i, OFF_CI // 384)),
                  pl.BlockSpec((r, LANES), tmap), pl.BlockSpec((r, LANES), tmap), pl.BlockSpec((r, LANES), tmap),
                  pl.BlockSpec((LANES, LANES), lambda i: (0, 0)),
                  pl.BlockSpec((1, LANES), lambda i: (0, 0)),
                  pl.BlockSpec((1, LANES), lambda i: (0, 0))],
        out_specs=[pl.BlockSpec((r, WC), rows), pl.BlockSpec((r, WC), rows), pl.BlockSpec((r, WC), rows),
                   pl.BlockSpec((r, HI * DI), rows), pl.BlockSpec((r, LANES), rows)],
        out_shape=[jax.ShapeDtypeStruct((m, WC), BF16), jax.ShapeDtypeStruct((m, WC), F32),
                   jax.ShapeDtypeStruct((m, WC), F32), jax.ShapeDtypeStruct((m, HI * DI), BF16),
                   jax.ShapeDtypeStruct((m, LANES), F32)],
        compiler_params=_cparams(("parallel",)),
    )(p, p, *tabs, bd, qn, kn)


def _attnprep_t_body(p_ref, ci_ref, cos_ref, s1_ref, s2_ref, bd_ref, qn_ref, kn_ref,
                     k_ref, v_ref, kiw_ref, kb_ref, kib_ref, qt_ref, qit_ref, wt_ref, vt_ref):
    cos, s1, s2 = cos_ref[...], s1_ref[...], s2_ref[...]
    bd = bd_ref[...]
    nch = WC // LANES
    r = p_ref.shape[0]
    for c in range(2 * nch):
        x = p_ref[:, c * LANES:(c + 1) * LANES]
        gain = (qn_ref if c < nch else kn_ref)[...]
        n = x * lax.rsqrt(_segsum(x * x, bd) * (1.0 / DH) + EPS) * gain
        rot = _rope(n, cos, s1, s2)
        if c < nch:
            qt_ref[c * LANES:(c + 1) * LANES, :] = (rot * (DH ** -0.5)).T.astype(BF16)
        else:
            k_ref[:, (c - nch) * LANES:(c - nch + 1) * LANES] = rot
            kb_ref[:, (c - nch) * LANES:(c - nch + 1) * LANES] = rot.astype(BF16)
    v = p_ref[:, 2 * WC:3 * WC]
    v_ref[...] = v
    kt = vt_ref.shape[-1]
    for t in range(r // kt):
        for c in range(nch):
            vt_ref[t, c * LANES:(c + 1) * LANES, :] = v[t * kt:(t + 1) * kt, c * LANES:(c + 1) * LANES].T.astype(BF16)
    for c in range(HI * DI // LANES):
        x = ci_ref[:, c * LANES:(c + 1) * LANES]
        qit_ref[c * LANES:(c + 1) * LANES, :] = _rope(x, cos, s1, s2).T.astype(BF16)
    x = ci_ref[:, HI * DI:HI * DI + LANES]
    lane = lax.broadcasted_iota(I32, x.shape, 1)
    is_ki = lane < DI
    kiw = _rope(x, jnp.where(is_ki, cos, 1.0), jnp.where(is_ki, s1, 0.0), jnp.where(is_ki, s2, 0.0))
    kiw_ref[...] = kiw
    kib_ref[...] = kiw[:, 0:DI].astype(BF16)
    wt_ref[...] = kiw.T[LANE_W:LANE_W + SUBLANES, :]


def _attn_prep_t(p, bsz, t, r, kt, tabs, bd, qn, kn):
    nt = t // r
    m = bsz * t
    assert t % r == 0 and r % kt == 0 and kt % LANES == 0
    rows = lambda b, i: (b * nt + i, 0)
    tmap = lambda b, i: (i, 0)
    cst = lambda b, i: (0, 0)
    return pl.pallas_call(
        _attnprep_t_body,
        grid=(bsz, nt),
        in_specs=[pl.BlockSpec((r, 3 * WC), lambda b, i: (b * nt + i, OFF_CQKV // (3 * WC))),
                  pl.BlockSpec((r, 384), lambda b, i: (b * nt + i, OFF_CI // 384)),
                  pl.BlockSpec((r, LANES), tmap), pl.BlockSpec((r, LANES), tmap), pl.BlockSpec((r, LANES), tmap),
                  pl.BlockSpec((LANES, LANES), cst),
                  pl.BlockSpec((1, LANES), cst),
                  pl.BlockSpec((1, LANES), cst)],
        out_specs=[pl.BlockSpec((r, WC), rows), pl.BlockSpec((r, WC), rows), pl.BlockSpec((r, LANES), rows),
                   pl.BlockSpec((r, WC), rows), pl.BlockSpec((r, DI), rows),
                   pl.BlockSpec((None, WC, r), lambda b, i: (b, 0, i)),
                   pl.BlockSpec((None, HI * DI, r), lambda b, i: (b, 0, i)),
                   pl.BlockSpec((None, SUBLANES, r), lambda b, i: (b, 0, i)),
                   pl.BlockSpec((None, r // kt, WC, kt), lambda b, i: (b, i, 0, 0))],
        out_shape=[jax.ShapeDtypeStruct((m, WC), F32), jax.ShapeDtypeStruct((m, WC), F32),
                   jax.ShapeDtypeStruct((m, LANES), F32),
                   jax.ShapeDtypeStruct((m, WC), BF16), jax.ShapeDtypeStruct((m, DI), BF16),
                   jax.ShapeDtypeStruct((bsz, WC, t), BF16),
                   jax.ShapeDtypeStruct((bsz, HI * DI, t), BF16),
                   jax.ShapeDtypeStruct((bsz, SUBLANES, t), F32),
                   jax.ShapeDtypeStruct((bsz, t // kt, WC, kt), BF16)],
        compiler_params=_cparams(("parallel", "parallel")),
    )(p, p, *tabs, bd, qn, kn)


def _float_key(sc):
    bits = lax.bitcast_convert_type(sc, I32)
    return bits ^ ((bits >> 31) & 0x7FFFFFFF)


def _kth_largest_key(count_ge, shape, topk):
    base = jnp.where(count_ge(jnp.zeros(shape, I32)) >= topk, 0, INT_MIN).astype(I32)

    def bit_step(i, base):
        cand = base + (jnp.int32(1) << (30 - i))
        return jnp.where(count_ge(cand) >= topk, cand, base)

    return lax.fori_loop(0, 31, bit_step, base)


def _select_tile(key, vstar, need, running, tri, visible):
    eq = key == vstar
    pref = _dot(eq.astype(BF16), tri)
    sel = (key > vstar) | (eq & (running + pref <= need))
    if visible is not None:
        sel = sel & visible
    return sel, running + pref[:, LANES - 1:LANES]


def _attn_prompt_body(qt_ref, qit_ref, wt_ref, k_ref, vt_ref, ki_ref, ltri_ref, o_ref,
                      key_scr, acc_scr, m_scr, l_scr, raw_scr, s_scr, p_scr, a_scr, bias_scr, *, tq, kt, topk):
    qblk = pl.program_id(1)
    nkb = (qblk * tq + tq + kt - 1) // kt
    qpos = qblk * tq + lax.broadcasted_iota(I32, (1, tq), 1)
    krow = lax.broadcasted_iota(I32, (kt, 1), 0)
    w = wt_ref[...] * ((HI * DI) ** -0.5)
    qit = qit_ref[...]
    qi_cat = jnp.concatenate([qit[h * DI:(h + 1) * DI, :] for h in range(HI)], axis=1)

    def key_tile(j):
        return pl.ds(pl.multiple_of(jnp.minimum(j, nkb - 1) * kt, kt), kt)

    def idx_dot(j):
        return _dot(ki_ref[key_tile(j), :], qi_cat)

    npairs_kt = (nkb + 1) // 2

    def fill_tile(j, cur):
        raw_scr[1 - cur] = idx_dot(j + 1)
        sc = None
        for h in range(HI):
            term = w[h:h + 1, :] * jnp.maximum(raw_scr[cur, :, h * tq:(h + 1) * tq], 0.0)
            sc = term if sc is None else sc + term
        key_scr[j] = jnp.where((krow + j * kt) <= qpos, _float_key(sc), INT_MIN)

    def fill(i, carry):
        fill_tile(2 * i, 0)
        fill_tile(2 * i + 1, 1)
        return carry

    raw_scr[0] = idx_dot(0)
    lax.fori_loop(0, npairs_kt, fill, 0)

    def count_ge(cand):
        def body(j, acc):
            ind = jnp.where(key_scr[j] >= cand, 1.0, 0.0)
            return acc + jnp.sum(ind.reshape(kt // COUNT_ROWS, COUNT_ROWS, tq), axis=0)
        acc = lax.fori_loop(0, nkb, body, jnp.zeros((COUNT_ROWS, tq), F32))
        return jnp.sum(acc, axis=0, keepdims=True)

    vstar = _kth_largest_key(count_ge, (1, tq), topk)
    need = topk - count_ge(vstar + 1)

    half =lax.broadcasted_iota(I32, (LANES, tq), 0) // DH
    q_pairs = []
    for c in range(HC // 2):
        qc = qt_ref[c * LANES:(c + 1) * LANES, :].astype(F32)
        q_pairs.append(jnp.concatenate([jnp.where(half == 0, qc, 0.0), jnp.where(half == 1, qc, 0.0)],
                                       axis=1).astype(BF16))
    m_scr[...] = jnp.full_like(m_scr, NEG_BIG)
    l_scr[...] = jnp.zeros_like(l_scr)
    acc_scr[...] = jnp.zeros_like(acc_scr)
    ltri = ltri_ref[...]

    npair = HC // 2

    def store_scores(j, slot):
        kb = k_ref[key_tile(j), :]
        for c in range(npair):
            s_scr[slot, c] = _dot(kb[:, c * LANES:(c + 1) * LANES], q_pairs[c])

    def store_bias(j, slot, running):
        key = key_scr[jnp.minimum(j, nkb - 1)]
        eq = key == vstar
        pref = _dot(ltri, eq.astype(BF16))
        sel = ((key > vstar) | (eq & (running + pref <= need))) & ((krow + j * kt) <= qpos)
        bias_scr[slot] = jnp.where(sel, 0.0, -jnp.inf)
        return running + pref[kt - 1:kt, :]

    def apply_pv(j, slot):
        vtb = vt_ref[jnp.clip(j, 0, nkb - 1)]
        for c in range(npair):
            pv = _dot(vtb[c * LANES:(c + 1) * LANES, :], p_scr[slot, c])
            for hh in range(2):
                h = 2 * c + hh
                acc_scr[h * DH:(h + 1) * DH, :] = (a_scr[slot, h:h + 1, :] * acc_scr[h * DH:(h + 1) * DH, :]
                                                   + pv[hh * DH:(hh + 1) * DH, hh * tq:(hh + 1) * tq])

    def sweep_tile(j, cur, running):
        nxt = 1 - cur
        apply_pv(j - 1, nxt)
        store_scores(j + 1, nxt)
        running = store_bias(j + 1, nxt, running)
        bias = bias_scr[cur]
        for c in range(npair):
            pc = []
            for hh in range(2):
                h = 2 * c + hh
                s = s_scr[cur, c, :, hh * tq:(hh + 1) * tq] + bias
                m_old = m_scr[h:h + 1, :]
                m_new = jnp.maximum(m_old, jnp.max(s, axis=0, keepdims=True))
                p = jnp.exp(s - m_new)
                alpha = jnp.exp(m_old - m_new)
                l_scr[h:h + 1, :] = alpha * l_scr[h:h + 1, :] + jnp.sum(p, axis=0, keepdims=True)
                m_scr[h:h + 1, :] = m_new
                a_scr[cur, h:h + 1, :] = alpha
                pc.append(p.astype(BF16))
            p_scr[cur, c] = jnp.concatenate(pc, axis=1)
        return running

    def sweep(i, running):
        return sweep_tile(2 * i + 1, 1, sweep_tile(2 * i, 0, running))

    store_scores(0, 0)
    p_scr[1] = jnp.zeros_like(p_scr[1])
    a_scr[1] = jnp.ones_like(a_scr[1])
    lax.fori_loop(0, npairs_kt, sweep, store_bias(0, 0, jnp.zeros((1, tq), F32)))
    apply_pv(2 * npairs_kt - 1, 1)
    for c in range(HC // 2):
        ot = jnp.concatenate([acc_scr[h * DH:(h + 1) * DH, :] / l_scr[h:h + 1, :] for h in (2 * c, 2 * c + 1)],
                             axis=0)
        o_ref[:, c * LANES:(c + 1) * LANES] = ot.T


def _attn_prompt(qt, qit, wt, kbf, vt, kibf, ltri, bsz, s_len):
    tq = LANES
    kt = vt.shape[-1]
    nq = s_len // tq
    topk = min(TOPK_MAX, s_len // 4)
    assert s_len % kt == 0 and kt % tq == 0
    k3 = kbf.reshape(bsz, s_len, WC)
    ki3 = kibf.reshape(bsz, s_len, DI)
    return pl.pallas_call(
        functools.partial(_attn_prompt_body, tq=tq, kt=kt, topk=topk),
        grid=(bsz, nq),
        in_specs=[pl.BlockSpec((None, WC, tq), lambda b, i: (b, 0, i)),
                  pl.BlockSpec((None, HI * DI, tq), lambda b, i: (b, 0, i)),
                  pl.BlockSpec((None, SUBLANES, tq), lambda b, i: (b, 0, i)),
                  pl.BlockSpec((None, s_len, WC), lambda b, i: (b, 0, 0)),
                  pl.BlockSpec((None, s_len // kt, WC, kt), lambda b, i: (b, 0, 0, 0)),
                  pl.BlockSpec((None, s_len, DI), lambda b, i: (b, 0, 0)),
                  pl.BlockSpec((kt, kt), lambda b, i: (0, 0))],
        out_specs=pl.BlockSpec((tq, WC), lambda b, i: (b * nq + i, 0)),
        out_shape=jax.ShapeDtypeStruct((bsz * s_len, WC), F32),
        scratch_shapes=[pltpu.VMEM((s_len // kt, kt, tq), I32),
                        pltpu.VMEM((WC, tq), F32),
                        pltpu.VMEM((SUBLANES, tq), F32),
                        pltpu.VMEM((SUBLANES, tq), F32),
                        pltpu.VMEM((2, kt, HI * tq), F32),
                        pltpu.VMEM((2, HC // 2, kt, 2 * tq), F32),
                        pltpu.VMEM((2, HC // 2, kt, 2 * tq), BF16),
                        pltpu.VMEM((2, SUBLANES, tq), F32),
                        pltpu.VMEM((2, kt, tq), F32)],
        compiler_params=_cparams(("parallel", "arbitrary")),
    )(qt, qit, wt, k3, vt, ki3, ltri)


def _attn_sample_body(pt_ref, q_ref, qi_ref, kiw_ref, knew_ref, vnew_ref, tri_ref, *refs, t, npg, topk):
    kidx_refs, k_refs, v_refs = refs[0:npg], refs[npg:2 * npg], refs[2 * npg:3 * npg]
    o_ref, s_scr, kipad_scr, kpad_scr, vpad_scr = refs[3 * npg:]
    nt = npg + 1
    qi = qi_ref[...]
    qi_stack = jnp.concatenate([qi[:, h * DI:(h + 1) * DI] for h in range(HI)], axis=0)
    w = kiw_ref[:, LANE_W:LANE_W + HI] * ((HI * DI) ** -0.5)
    w_stack = jnp.concatenate([w[:, h:h + 1] for h in range(HI)], axis=0)

    def score_keys(kib):
        rel = jnp.maximum(_dot_nt(qi_stack, kib), 0.0) * w_stack
        sc = rel[0:t]
        for h in range(1, HI):
            sc = sc + rel[h * t:(h + 1) * t]
        return _float_key(sc)

    keys = [score_keys(kidx_refs[j][...].astype(BF16)) for j in range(npg)]
    kipad_scr[...] = jnp.zeros_like(kipad_scr)
    kipad_scr[0:t, :] = kiw_ref[:, 0:DI]
    vis_new = lax.broadcasted_iota(I32, (t, LANES), 1) <= lax.broadcasted_iota(I32, (t, LANES), 0)
    keys.append(jnp.where(vis_new, score_keys(kipad_scr[...].astype(BF16)), INT_MIN))

    def count_ge(cand):
        acc = jnp.zeros((t, LANES), F32)
        for key in keys:
            acc = acc + jnp.where(key >= cand, 1.0, 0.0)
        return jnp.sum(acc, axis=1, keepdims=True)

    vstar = _kth_largest_key(count_ge, (t, 1), topk)
    need = topk - count_ge(vstar + 1)

    kpad_scr[...] = jnp.zeros_like(kpad_scr)
    vpad_scr[...] = jnp.zeros_like(vpad_scr)
    kpad_scr[0:t, :] = knew_ref[...]
    vpad_scr[0:t, :] = vnew_ref[...]
    head_of_lane = lax.broadcasted_iota(I32, (t, WC), 1) // DH
    q = q_ref[...].astype(F32)
    q_stack = jnp.concatenate([jnp.where(head_of_lane == h, q, 0.0) for h in range(HC)], axis=0).astype(BF16)
    tri = tri_ref[...]
    running = jnp.zeros((t, 1), F32)
    m_acc = jnp.full((HC * t, LANES), NEG_BIG, F32)
    for j in range(nt):
        sel, running = _select_tile(keys[j], vstar, need, running, tri, vis_new if j == npg else None)
        bias = jnp.where(sel, 0.0, -jnp.inf)
        kb = (k_refs[j] if j < npg else kpad_scr)[...].astype(BF16)
        s = _dot_nt(q_stack, kb) + jnp.concatenate([bias] * HC, axis=0)
        s_scr[j] = s
        m_acc = jnp.maximum(m_acc, s)
    m = jnp.max(m_acc, axis=1, keepdims=True)
    l_acc = jnp.zeros((HC * t, LANES), F32)
    acc = jnp.zeros((HC * t, WC), F32)
    for j in range(nt):
        p = jnp.exp(s_scr[j] - m)
        l_acc = l_acc + p
        vb = (v_refs[j] if j < npg else vpad_scr)[...].astype(BF16)
        acc = acc + _dot(p.astype(BF16), vb)
    o = acc / jnp.sum(l_acc, axis=1, keepdims=True)
    out = jnp.where(head_of_lane == 0, o[0:t], 0.0)
    for h in range(1, HC):
        out = out + jnp.where(head_of_lane == h, o[h * t:(h + 1) * t], 0.0)
    o_ref[...] = out


def _attn_sample(page_table, q, qi, kiw, knew, vnew, tri, cache_kidx, cache_k, cache_v, layer, bsz, t):
    npg = page_table.shape[1]
    page = cache_k.shape[2]
    assert page == LANES
    topk = min(TOPK_MAX, (npg * page + t) // 4)
    row = lambda b, pt: (b, 0)

    def page_map(j):
        return lambda b, pt: (layer, pt[b, j], 0, 0)

    in_specs = [pl.BlockSpec((t, WC), row), pl.BlockSpec((t, HI * DI), row), pl.BlockSpec((t, LANES), row),
                pl.BlockSpec((t, WC), row), pl.BlockSpec((t, WC), row),
                pl.BlockSpec((LANES, LANES), lambda b, pt: (0, 0))]
    in_specs += [pl.BlockSpec((None, None, page, DI), page_map(j)) for j in range(npg)]
    in_specs += [pl.BlockSpec((None, None, page, WC), page_map(j)) for j in range(npg)]
    in_specs += [pl.BlockSpec((None, None, page, WC), page_map(j)) for j in range(npg)]
    grid_spec = pltpu.PrefetchScalarGridSpec(
        num_scalar_prefetch=1,
        grid=(bsz,),
        in_specs=in_specs,
        out_specs=pl.BlockSpec((t, WC), row),
        scratch_shapes=[pltpu.VMEM((npg + 1, HC * t, LANES), F32), pltpu.VMEM((LANES, DI), F32),
                        pltpu.VMEM((LANES, WC), F32), pltpu.VMEM((LANES, WC), F32)])
    return pl.pallas_call(
        functools.partial(_attn_sample_body, t=t, npg=npg, topk=topk),
        grid_spec=grid_spec,
        out_shape=jax.ShapeDtypeStruct((bsz * t, WC), F32),
        compiler_params=_cparams(("parallel",)),
    )(page_table, q, qi, kiw, knew, vnew, tri,
      *([cache_kidx] * npg), *([cache_k] * npg), *([cache_v] * npg))


def _rope_tables(pos):
    half = ROPE_DIM // 2
    inv_freq = ROPE_THETA ** (-jnp.arange(half, dtype=F32) / half)
    ang = pos.astype(F32)[:, None] * inv_freq[None, :]
    cos, sin = jnp.cos(ang), jnp.sin(ang)
    n = pos.shape[0]
    one = jnp.ones((n, DH - ROPE_DIM), F32)
    zero_h = jnp.zeros((n, half), F32)
    zero_r = jnp.zeros((n, DH - ROPE_DIM), F32)
    c64 = jnp.concatenate([cos, cos, one], axis=1)
    s1 = jnp.concatenate([-sin, zero_h, zero_r], axis=1)
    s2 = jnp.concatenate([zero_h, sin, zero_r], axis=1)
    rep = LANES // DH
    return tuple(jnp.tile(a, (1, rep)) for a in (c64, s1, s2))


def _reorder_w_in(w_in):
    sizes = (WA, WA, WA, QKV_B, WB, HB, HB, 3 * WC, HI * DI, DI, HI)
    offs = [0]
    for s in sizes:
        offs.append(offs[-1] + s)
    seg = lambda i: w_in[:, :, offs[i]:offs[i + 1]]
    pad = jnp.zeros(w_in.shape[:2] + (LANES - DI - HI - 2 * HB,), w_in.dtype)
    order = [seg(3), seg(7), seg(0), seg(1), seg(2), seg(4), seg(8), seg(9), seg(10), seg(5), seg(6), pad]
    return jnp.concatenate(order, axis=-1).astype(BF16)


def _pad_state(prev):
    return jnp.pad(prev, ((0, 0), (SUBLANES - prev.shape[1], 0), (0, 0)))


def kernel(x_prompt, x_sample, state_conv_a, state_gdn_conv, state_gdn, cache_k, cache_v, cache_kidx, page_table, ffn1_norm, ffn1_w_gate, ffn1_w_up, ffn1_w_down, mix_norm, w_in, conv_a_w, gdn_conv_w, gdn_a_log, gdn_dt_bias, gdn_out_norm, attn_q_norm, attn_k_norm, w_out, ffn2_norm, ffn2_w_gate, ffn2_w_up, ffn2_w_down):
    depth = w_in.shape[0]
    bp, tp, d = x_prompt.shape
    bs, ts, _ = x_sample.shape
    mp, ms = bp * tp, bs * ts
    npool, page = cache_k.shape[1], cache_k.shape[2]
    past = page_table.shape[1] * page
    assert ts == SUBLANES and tp % LANES == 0 and mp % 256 == 0

    x = jnp.concatenate([x_prompt.reshape(mp, d), x_sample.reshape(ms, d)], axis=0)
    w_in_r = _reorder_w_in(w_in)
    bf = lambda a: a.astype(BF16)
    f1g, f1u, f1d = bf(ffn1_w_gate), bf(ffn1_w_up), bf(ffn1_w_down)
    f2g, f2u, f2d = bf(ffn2_w_gate), bf(ffn2_w_up), bf(ffn2_w_down)
    w_out_b = bf(w_out)
    ck = cache_k.reshape(depth, npool, page, WC)
    cv = cache_v.reshape(depth, npool, page, WC)

    li = jnp.arange(LANES)
    bd = ((li[:, None] // DH) == (li[None, :] // DH)).astype(BF16)
    tri = (li[:, None] <= li[None, :]).astype(BF16)
    kt = r_p = _pick(tp, (256, 128))
    lk = jnp.arange(kt)
    ltri = (lk[None, :] <= lk[:, None]).astype(BF16)
    r_s =_pick(ms, (128, 64, 32, 16, 8))
    tabs_p = _rope_tables(jnp.arange(tp))
    tabs_s = _rope_tables(jnp.tile(past + jnp.arange(ts), r_s // ts))
    zero_a = jnp.zeros((bp, SUBLANES, WA), F32)
    zero_b = jnp.zeros((bp, SUBLANES, QKV_B), F32)
    zero_s = jnp.zeros((bp, HB, DKB, DVB), F32)
    lane_pad = lambda v, off: jnp.zeros((1, LANES), F32).at[0, off:off + v.shape[0]].set(v)

    outs = [[] for _ in range(12)]
    for l in range(depth):
        x = _ffn(x, ffn1_norm[l][None], f1g[l], f1u[l], f1d[l])
        p = _inproj(x, mix_norm[l][None], w_in_r[l])
        avec = lane_pad(jnp.exp(gdn_a_log[l].astype(F32)), LANE_GA)
        dtb = lane_pad(gdn_dt_bias[l].astype(F32), LANE_GA)
        gn = gdn_out_norm[l][None].astype(F32)
        qn = jnp.tile(attn_q_norm[l].astype(F32), LANES // DH)[None]
        kn = jnp.tile(attn_k_norm[l].astype(F32), LANES // DH)[None]

        ya, yb, yc = [], [], []
        groups = ((0, bp, tp, zero_a, zero_b, zero_s), (mp, bs, ts, _pad_state(state_conv_a[l]),
                                                       _pad_state(state_gdn_conv[l]), state_gdn[l]))
        for gi, (row0, bsz, t, prev_a, prev_b, s0) in enumerate(groups):
            m = bsz * t
            y_a, st_a = _mixer_a(p, row0, bsz, t, conv_a_w[l], prev_a)
            qg, kg, vg, gbg, st_b = _gdn_prep(p, row0, bsz, t, gdn_conv_w[l], prev_b, bd, avec, dtb)
            y_b, s_new = _gdn(qg, kg, vg, gbg, p, row0, bsz, t, s0, gn)
            if gi == 0:
                kc, vc, kiw, kbf, kibf, qt, qit, wt, vt = _attn_prep_t(p, bsz, t, r_p, kt, tabs_p, bd, qn, kn)
                y_c = _attn_prompt(qt, qit, wt, kbf, vt, kibf, ltri, bsz, t)
            else:
                qc, kc, vc, qi, kiw = _attn_prep(p, row0, m, r_s, tabs_s, bd, qn, kn)
                y_c = _attn_sample(page_table, qc, qi, kiw, kc, vc, tri, cache_kidx, ck, cv, l, bsz, t)
            ya.append(y_a)
            yb.append(y_b)
            yc.append(y_c)
            res = (st_a[:, SUBLANES - (CONV_A - 1):], st_b[:, SUBLANES - (GDN_CONV - 1):], s_new,
                   kc.reshape(bsz, t, HC, DH), vc.reshape(bsz, t, HC, DH), kiw[:, :DI].reshape(bsz, t, DI))
            for i, r in enumerate(res):
                outs[gi * 6 + i].append(r)
        x = _outproj(x, jnp.concatenate(ya), jnp.concatenate(yb), jnp.concatenate(yc), w_out_b[l])
        x = _ffn(x, ffn2_norm[l][None], f2g[l], f2u[l], f2d[l])

    y_prompt = x[:mp].reshape(bp, tp, d)
    y_sample = x[mp:].reshape(bs, ts, d)
    return (y_prompt, y_sample) + tuple(jnp.stack(o) for o in outs)
```

```python
import functools
import math

import jax
import jax.numpy as jnp
from jax import lax
from jax.experimental import pallas as pl
from jax.experimental.pallas import tpu as pltpu

F32 = jnp.float32
BF16 = jnp.bfloat16
I32 = jnp.int32

EPS = 1e-6
LANES = 128
SUBLANES = 8
VMEM_LIMIT = 56 * 1024 * 1024

A_GROUPS = 4
CONV_A = 3
HB = 6
DKB = 64
DVB = 64
GDN_CONV = 4
GDN_ROWS = 64
HC = 6
DH = 64
HI = 4
DI = 64
TOPK_MAX = 256
ROPE_DIM = DH // 4
ROPE_THETA = 500000.0
INT_MIN = -2 ** 31
COUNT_ROWS = 4 * SUBLANES
NEG_BIG = -1e30
LOG2E = math.log2(math.e)

WA = 256
QKV_B = 2 * HB * DKB + HB * DVB
WB = HB * DVB
WC = HC * DH
OFF_BQKV = 0
OFF_CQKV = QKV_B
OFF_A = 2 * QKV_B
OFF_BZ = OFF_A + 3 * WA
OFF_CI = OFF_BZ + WB
N_P = OFF_CI + 384
LANE_W = DI
LANE_GA = DI + HI
LANE_GB = DI + HI + HB


def _pick(n, cands):
    for c in cands:
        if n % c == 0:
            return c
    raise ValueError(f"no tile for {n}")


def _cparams(sem):
    return pltpu.CompilerParams(dimension_semantics=sem, vmem_limit_bytes=VMEM_LIMIT)


def _dot(a, b):
    return jnp.dot(a, b, preferred_element_type=F32)


def _dot_nt(a, b):
    return lax.dot_general(a, b, (((1,), (1,)), ((), ())), preferred_element_type=F32)


def _dot_hi(a, b):
    return jnp.dot(a, b, preferred_element_type=F32, precision=lax.Precision.HIGHEST)


def _dot_nt_hi(a, b):
    return lax.dot_general(a, b, (((1,), (1,)), ((), ())), preferred_element_type=F32,
                           precision=lax.Precision.HIGHEST)


def _dot3(a, b):
    ah = a.astype(BF16)
    al = (a - ah.astype(F32)).astype(BF16)
    bh = b.astype(BF16)
    bl = (b - bh.astype(F32)).astype(BF16)
    return _dot(ah, bh) + _dot(ah, bl) + _dot(al, bh)


def _segsum(x, bd):
    hi = x.astype(BF16)
    lo = (x - hi.astype(F32)).astype(BF16)
    return _dot(hi, bd) + _dot(lo, bd)


def _sigmoid(x):
    return 1.0 / (1.0 + jnp.exp(-x))


def _silu(x):
    return x * _sigmoid(x)


def _softplus(x):
    return jnp.maximum(x, 0.0) + jnp.log(1.0 + jnp.exp(-jnp.abs(x)))


def _ffn_body(x_ref, g_ref, wg_ref, wu_ref, wd_ref, o_ref, h_ref, acc_ref):
    j = pl.program_id(1)

    @pl.when(j == 0)
    def _():
        x = x_ref[...]
        ms = jnp.mean(x * x, axis=-1, keepdims=True)
        h_ref[...] = (x * lax.rsqrt(ms + EPS) * g_ref[...]).astype(BF16)
        acc_ref[...] = jnp.zeros_like(acc_ref)

    h = h_ref[...]
    a = _dot(h, wg_ref[...])
    b = _dot(h, wu_ref[...])
    t = (_silu(a) * b).astype(BF16)
    acc_ref[...] += _dot(t, wd_ref[...])

    @pl.when(j == pl.num_programs(1) - 1)
    def _():
        o_ref[...] = x_ref[...] + 0.5 * acc_ref[...]


def _ffn(x, g, wg, wu, wd):
    m, d = x.shape
    f = wg.shape[1]
    tm = _pick(m, (512, 256, 128, 64, 32, 16, 8))
    tf = _pick(f, (1408, 256, 128))
    return pl.pallas_call(
        _ffn_body,
        grid=(m // tm, f // tf),
        in_specs=[pl.BlockSpec((tm, d), lambda i, j: (i, 0)),
                  pl.BlockSpec((1, d), lambda i, j: (0, 0)),
                  pl.BlockSpec((d, tf), lambda i, j: (0, j)),
                  pl.BlockSpec((d, tf), lambda i, j: (0, j)),
                  pl.BlockSpec((tf, d), lambda i, j: (j, 0))],
        out_specs=pl.BlockSpec((tm, d), lambda i, j: (i, 0)),
        out_shape=jax.ShapeDtypeStruct((m, d), F32),
        scratch_shapes=[pltpu.VMEM((tm, d), BF16), pltpu.VMEM((tm, d), F32)],
        compiler_params=_cparams(("parallel", "arbitrary")),
    )(x, g, wg, wu, wd)


def _inproj_body(x_ref, g_ref, w_ref, o_ref, h_ref):
    @pl.when(pl.program_id(1) == 0)
    def _():
        x = x_ref[...]
        ms = jnp.mean(x * x, axis=-1, keepdims=True)
        h_ref[...] = (x * lax.rsqrt(ms + EPS) * g_ref[...]).astype(BF16)

    o_ref[...] = _dot(h_ref[...], w_ref[...])


def _inproj(x, g, w):
    m, d = x.shape
    n = w.shape[1]
    tm = _pick(m, (512, 256, 128, 64, 32, 16, 8))
    tn = _pick(n, (1920, 768, 384, 128))
    return pl.pallas_call(
        _inproj_body,
        grid=(m // tm, n // tn),
        in_specs=[pl.BlockSpec((tm, d), lambda i, j: (i, 0)),
                  pl.BlockSpec((1, d), lambda i, j: (0, 0)),
                  pl.BlockSpec((d, tn), lambda i, j: (0, j))],
        out_specs=pl.BlockSpec((tm, tn), lambda i, j: (i, j)),
        out_shape=jax.ShapeDtypeStruct((m, n), F32),
        scratch_shapes=[pltpu.VMEM((tm, d), BF16)],
        compiler_params=_cparams(("parallel", "arbitrary")),
    )(x, g, w)


def _outproj_body(x_ref, ya_ref, yb_ref, yc_ref, w_ref, o_ref):
    acc = _dot(ya_ref[...].astype(BF16), w_ref[0:WA, :])
    acc += _dot(yb_ref[...].astype(BF16), w_ref[WA:WA + WB, :])
    acc += _dot(yc_ref[...].astype(BF16), w_ref[WA + WB:WA + WB + WC, :])
    o_ref[...] = x_ref[...] + acc


def _outproj(x, ya, yb, yc, w):
    m, d = x.shape
    tm = _pick(m, (512, 256, 128, 64, 32, 16, 8))
    return pl.pallas_call(
        _outproj_body,
        grid=(m // tm,),
        in_specs=[pl.BlockSpec((tm, d), lambda i: (i, 0)),
                  pl.BlockSpec((tm, WA), lambda i: (i, 0)),
                  pl.BlockSpec((tm, WB), lambda i: (i, 0)),
                  pl.BlockSpec((tm, WC), lambda i: (i, 0)),
                  pl.BlockSpec(w.shape, lambda i: (0, 0))],
        out_specs=pl.BlockSpec((tm, d), lambda i: (i, 0)),
        out_shape=jax.ShapeDtypeStruct((m, d), F32),
        compiler_params=_cparams(("parallel",)),
    )(x, ya, yb, yc, w)


def _conv_taps(u, w_ref, prev_ref, ext_ref, tt, width):
    it = pl.program_id(1)

    @pl.when(it == 0)
    def _():
        ext_ref[:, 0:SUBLANES, :] = prev_ref[...]

    @pl.when(it > 0)
    def _():
        ext_ref[:, 0:SUBLANES, :] = ext_ref[:, tt:tt + SUBLANES, :]

    ext_ref[:, SUBLANES:SUBLANES + tt, :] = u
    acc = None
    for s in range(width):
        term = ext_ref[:, SUBLANES - s:SUBLANES - s + tt, :] * w_ref[width - 1 - s:width - s, :]
        acc = term if acc is None else acc + term
    return acc, ext_ref[:, tt:tt + SUBLANES, :]


def _group_tiles(bsz, t):
    if t >= 128:
        return 1, _pick(t, (256, 128))
    return _pick(bsz, (16, 8, 4, 2, 1)), t


def _mixa_body(p_ref, w_ref, prev_ref, y_ref, st_ref, ext_ref, *, bb, tt):
    pa = p_ref[...]
    u = (pa[:, WA:2 * WA] * pa[:, 2 * WA:3 * WA]).reshape(bb, tt, WA)
    conv, tail = _conv_taps(u, w_ref, prev_ref, ext_ref, tt, CONV_A)
    y_ref[...] = pa[:, 0:WA] * conv.reshape(bb * tt, WA)
    st_ref[...] = tail


def _mixer_a(p, row0, bsz, t, w, prev8):
    bb, tt = _group_tiles(bsz, t)
    r = bb * tt
    nb, nt = bsz // bb, t // tt
    base = row0 // r
    assert row0 % r == 0
    return pl.pallas_call(
        functools.partial(_mixa_body, bb=bb, tt=tt),
        grid=(nb, nt),
        in_specs=[pl.BlockSpec((r, 3 * WA), lambda i, j: (base + i * nt + j, OFF_A // (3 * WA))),
                  pl.BlockSpec(w.shape, lambda i, j: (0, 0)),
                  pl.BlockSpec((bb, SUBLANES, WA), lambda i, j: (i, 0, 0))],
        out_specs=[pl.BlockSpec((r, WA), lambda i, j: (i * nt + j, 0)),
                   pl.BlockSpec((bb, SUBLANES, WA), lambda i, j: (i, 0, 0))],
        out_shape=[jax.ShapeDtypeStruct((bsz * t, WA), F32),
                   jax.ShapeDtypeStruct((bsz, SUBLANES, WA), F32)],
        scratch_shapes=[pltpu.VMEM((bb, tt + SUBLANES, WA), F32)],
        compiler_params=_cparams(("parallel", "arbitrary")),
    )(p, w, prev8)


def _gdnprep_body(p_ref, ci_ref, w_ref, prev_ref, bd_ref, avec_ref, dtb_ref,
                  q_ref, k_ref, v_ref, gb_ref, st_ref, ext_ref, *, bb, tt):
    r = bb * tt
    x = p_ref[...].reshape(bb, tt, QKV_B)
    conv, tail = _conv_taps(x, w_ref, prev_ref, ext_ref, tt, GDN_CONV)
    st_ref[...] = tail
    a = _silu(conv.reshape(r, QKV_B))
    bd = bd_ref[...]
    nq = HB * DKB // LANES
    for c in range(2 * nq):
        xc = a[:, c * LANES:(c + 1) * LANES]
        n = xc * lax.rsqrt(_segsum(xc * xc, bd) + EPS)
        if c < nq:
            q_ref[:, c * LANES:(c + 1) * LANES] = n * (DKB ** -0.5)
        else:
            k_ref[:, (c - nq) * LANES:(c - nq + 1) * LANES] = n
    v_ref[...] = a[:, 2 * HB * DKB:]
    raw = ci_ref[...]
    g = -avec_ref[...] * _softplus(raw + dtb_ref[...])
    lane = lax.broadcasted_iota(I32, raw.shape, 1)
    gb_ref[...] = jnp.where((lane >= LANE_GB) & (lane < LANE_GB + HB), _sigmoid(raw), g)


def _gdn_prep(p, row0, bsz, t, w, prev8, bd, avec, dtb):
    bb, tt = _group_tiles(bsz, t)
    r = bb * tt
    nb, nt = bsz // bb, t // tt
    base = row0 // r
    assert row0 % r == 0
    m = bsz * t
    rows = lambda i, j: (i * nt + j, 0)
    return pl.pallas_call(
        functools.partial(_gdnprep_body, bb=bb, tt=tt),
        grid=(nb, nt),
        in_specs=[pl.BlockSpec((r, QKV_B), lambda i, j: (base + i * nt + j, OFF_BQKV // QKV_B)),
                  pl.BlockSpec((r, LANES), lambda i, j: (base + i * nt + j, (OFF_CI + 2 * LANES) // LANES)),
                  pl.BlockSpec(w.shape, lambda i, j: (0, 0)),
                  pl.BlockSpec((bb, SUBLANES, QKV_B), lambda i, j: (i, 0, 0)),
                  pl.BlockSpec((LANES, LANES), lambda i, j: (0, 0)),
                  pl.BlockSpec((1, LANES), lambda i, j: (0, 0)),
                  pl.BlockSpec((1, LANES), lambda i, j: (0, 0))],
        out_specs=[pl.BlockSpec((r, HB * DKB), rows),
                   pl.BlockSpec((r, HB * DKB), rows),
                   pl.BlockSpec((r, WB), rows),
                   pl.BlockSpec((r, LANES), rows),
                   pl.BlockSpec((bb, SUBLANES, QKV_B), lambda i, j: (i, 0, 0))],
        out_shape=[jax.ShapeDtypeStruct((m, HB * DKB), F32),
                   jax.ShapeDtypeStruct((m, HB * DKB), F32),
                   jax.ShapeDtypeStruct((m, WB), F32),
                   jax.ShapeDtypeStruct((m, LANES), F32),
                   jax.ShapeDtypeStruct((bsz, SUBLANES, QKV_B), F32)],
        scratch_shapes=[pltpu.VMEM((bb, tt + SUBLANES, QKV_B), F32)],
        compiler_params=_cparams(("parallel", "arbitrary")),
    )(p, p, w, prev8, bd, avec, dtb)


def _gdn_body(q_ref, k_ref, v_ref, gb_ref, z_ref, s0_ref, gn_ref, y_ref, sout_ref, s_scr, *, c, g, nch):
    r = 2 * GDN_ROWS
    n = 2 * g
    it = pl.program_id(1)

    @pl.when(it == 0)
    def _():
        s_scr[...] = s0_ref[...]

    ri = lax.broadcasted_iota(I32, (r, r), 0)
    ci = lax.broadcasted_iota(I32, (r, r), 1)
    same = (ri // c) == (ci // c)
    incl = same & (ci <= ri)
    strict = same & (ci < ri)
    eye = (ri == ci).astype(F32)
    lastsel = (ci == (ri // c) * c + (c - 1)).astype(F32)
    chunks = range(nch)
    rows_of = lambda ch: slice(ch * GDN_ROWS, (ch + 1) * GDN_ROWS)
    gb = [jnp.concatenate([gb_ref[rows_of(ch), :]] * 2, axis=0) for ch in chunks]
    gcum = [_dot_hi(incl.astype(F32), gb[ch]) for ch in chunks]
    glast = [_dot_hi(lastsel, gcum[ch]) for ch in chunks]
    lane = lax.broadcasted_iota(I32, (r, LANES), 1)
    lane0 = (lane == 0).astype(F32)
    upper = lax.broadcasted_iota(I32, (r, 1), 0) < GDN_ROWS
    er = lax.broadcasted_iota(I32, (r, n * DKB), 0)
    ec = lax.broadcasted_iota(I32, (r, n * DKB), 1)
    emask = (er // c) == (ec // DKB)
    tr = lax.broadcasted_iota(I32, (DKB, n * DKB), 0)
    tc = lax.broadcasted_iota(I32, (DKB, n * DKB), 1)
    tile_mat = ((tc % DKB) == tr).astype(BF16)
    xr = lax.broadcasted_iota(I32, (n * DKB, r), 0)
    xc = lax.broadcasted_iota(I32, (n * DKB, r), 1)
    emask_t = (xr // DKB) == (xc // c)
    sr = lax.broadcasted_iota(I32, (n * DKB, DKB), 0)
    sc = lax.broadcasted_iota(I32, (n * DKB, DKB), 1)
    tile_mat_t = ((sr % DKB) == sc).astype(BF16)
    esel = (xc == (xr // DKB) * c + (c - 1)).astype(F32)
    gend = [_dot_hi(esel, gcum[ch]) for ch in chunks]
    s_upper = lax.broadcasted_iota(I32, (n * DKB, 1), 0) < g * DKB
    n_sq = max(0, int(math.ceil(math.log2(c))) - 1)

    npair = HB // 2
    items = [(ch, i) for ch in chunks for i in range(npair)]
    each = lambda f: [f(n_, ch, i) for n_, (ch, i) in enumerate(items)]

    def stack(ref, width):
        return each(lambda n_, ch, i: jnp.concatenate([ref[rows_of(ch), 2 * i * width:(2 * i + 1) * width],
                                                       ref[rows_of(ch), (2 * i + 1) * width:(2 * i + 2) * width]], axis=0))

    def pick(mats, rows_upper, base):
        return each(lambda n_, ch, i: jnp.where(rows_upper, mats[ch][:, base + 2 * i:base + 2 * i + 1],
                                                mats[ch][:, base + 2 * i + 1:base + 2 * i + 2]))

    q, k, v, zz = stack(q_ref, DKB), stack(k_ref, DKB), stack(v_ref, DVB), stack(z_ref, DVB)
    gcol = pick(gcum, upper, LANE_GA)
    glc = pick(glast, upper, LANE_GA)
    beta = pick(gb, upper, LANE_GB)
    srow = [jnp.exp(x) for x in pick(gend, s_upper, LANE_GA)]
    grow = each(lambda n_, ch, i: _dot_nt_hi(lane0, jnp.where(lane == 0, gcol[n_], 0.0)))
    decay = each(lambda n_, ch, i: jnp.exp(jnp.where(incl, gcol[n_] - grow[n_], -jnp.inf)))
    kb = each(lambda n_, ch, i: k[n_] * beta[n_])
    a_mat = each(lambda n_, ch, i: _dot_nt(kb[n_], k[n_]) * jnp.where(strict, decay[n_], 0.0))
    pw = [-a for a in a_mat]
    t_mat = [eye + x for x in pw]
    for _ in range(n_sq):
        pw = [_dot3(x, x) for x in pw]
        t_mat = each(lambda n_, ch, i: t_mat[n_] + _dot3(t_mat[n_], pw[n_]))
    u = each(lambda n_, ch, i: _dot(t_mat[n_], v[n_] * beta[n_]))
    wk = each(lambda n_, ch, i: _dot(t_mat[n_], kb[n_] * jnp.exp(gcol[n_])))
    qk = each(lambda n_, ch, i: _dot_nt(q[n_], k[n_]) * decay[n_])
    wk_e = each(lambda n_, ch, i: jnp.where(emask, _dot(wk[n_].astype(BF16), tile_mat), 0.0))
    qg_e = each(lambda n_, ch, i: jnp.where(emask, _dot((q[n_] * jnp.exp(gcol[n_])).astype(BF16), tile_mat), 0.0))
    kd_t = each(lambda n_, ch, i: jnp.where(
        emask_t, _dot_nt(tile_mat_t, (k[n_] * jnp.exp(glc[n_] - gcol[n_])).astype(BF16)), 0.0))
    s = [s_scr[2 * i:2 * i + 2].reshape(n * DKB, DVB) for i in range(npair)]
    for ch in chunks:
        sel = [ch * npair + i for i in range(npair)]
        v_new = [u[n_] - _dot(wk_e[n_], s[i]) for i, n_ in enumerate(sel)]
        out = [_dot(qg_e[n_], s[i]) + _dot(qk[n_], v_new[i]) for i, n_ in enumerate(sel)]
        s = [s[i] * srow[n_] + _dot(kd_t[n_], v_new[i]) for i, n_ in enumerate(sel)]
        for i, n_ in enumerate(sel):
            ms = jnp.mean(out[i] * out[i], axis=-1, keepdims=True)
            y = out[i] * lax.rsqrt(ms + EPS) * gn_ref[...] * _silu(zz[n_])
            y_ref[rows_of(ch), 2 * i * DVB:(2 * i + 1) * DVB] = y[0:GDN_ROWS]
            y_ref[rows_of(ch), (2 * i + 1) * DVB:(2 * i + 2) * DVB] = y[GDN_ROWS:]
    for i in range(npair):
        s_scr[2 * i:2 * i + 2] = s[i].reshape(2, g * DKB, DVB)

    @pl.when(it == pl.num_programs(1) - 1)
    def _():
        sout_ref[...] = s_scr[...]


def _gdn(q, k, v, gb, p, row0, bsz, t, s0, gn):
    c = min(GDN_ROWS, t)
    g = GDN_ROWS // c
    nch = 2 if (g == 1 and (t // c) % 2 == 0) else 1
    rb = nch * GDN_ROWS
    assert t % (c * nch) == 0 and bsz % g == 0 and row0 % rb == 0
    nb, nt = bsz // g, t // (c * nch)
    base = row0 // rb
    rows = lambda i, j: (i * nt + j, 0)
    s0r = s0.reshape(nb, g, HB, DKB, DVB).transpose(0, 2, 1, 3, 4).reshape(nb, HB, g * DKB, DVB)
    y, s_out = pl.pallas_call(
        functools.partial(_gdn_body, c=c, g=g, nch=nch),
        grid=(nb, nt),
        in_specs=[pl.BlockSpec((rb, HB * DKB), rows),
                  pl.BlockSpec((rb, HB * DKB), rows),
                  pl.BlockSpec((rb, WB), rows),
                  pl.BlockSpec((rb, LANES), rows),
                  pl.BlockSpec((rb, WB), lambda i, j: (base + i * nt + j, OFF_BZ // WB)),
                  pl.BlockSpec((None, HB, g * DKB, DVB), lambda i, j: (i, 0, 0, 0)),
                  pl.BlockSpec((1, DVB), lambda i, j: (0, 0))],
        out_specs=[pl.BlockSpec((rb, WB), rows),
                   pl.BlockSpec((None, HB, g * DKB, DVB), lambda i, j: (i, 0, 0, 0))],
        out_shape=[jax.ShapeDtypeStruct((bsz * t, WB), F32),
                   jax.ShapeDtypeStruct((nb, HB, g * DKB, DVB), F32)],
        scratch_shapes=[pltpu.VMEM((HB, g * DKB, DVB), F32)],
        compiler_params=_cparams(("parallel", "arbitrary")),
    )(q, k, v, gb, p, s0r, gn)
    s_out = s_out.reshape(nb, HB, g, DKB, DVB).transpose(0, 2, 1, 3, 4).reshape(bsz, HB, DKB, DVB)
    return y, s_out


def _rope(x, cos, s1, s2):
    half = ROPE_DIM // 2
    return x * cos + pltpu.roll(x, LANES - half, 1) * s1 + pltpu.roll(x, half, 1) * s2


def _attnprep_body(p_ref, ci_ref, cos_ref, s1_ref, s2_ref, bd_ref, qn_ref, kn_ref,
                   q_ref, k_ref, v_ref, qi_ref, kiw_ref):
    cos, s1, s2 = cos_ref[...], s1_ref[...], s2_ref[...]
    bd = bd_ref[...]
    nch = WC // LANES
    for c in range(2 * nch):
        x = p_ref[:, c * LANES:(c + 1) * LANES]
        gain = (qn_ref if c < nch else kn_ref)[...]
        n = x * lax.rsqrt(_segsum(x * x, bd) * (1.0 / DH) + EPS) * gain
        rot = _rope(n, cos, s1, s2)
        if c < nch:
            q_ref[:, c * LANES:(c + 1) * LANES] = (rot * (DH ** -0.5)).astype(BF16)
        else:
            k_ref[:, (c - nch) * LANES:(c - nch + 1) * LANES] = rot
    v_ref[...] = p_ref[:, 2 * WC:3 * WC]
    for c in range(HI * DI // LANES):
        x = ci_ref[:, c * LANES:(c + 1) * LANES]
        qi_ref[:, c * LANES:(c + 1) * LANES] = _rope(x, cos, s1, s2).astype(BF16)
    x = ci_ref[:, HI * DI:HI * DI + LANES]
    lane = lax.broadcasted_iota(I32, x.shape, 1)
    is_ki = lane < DI
    kiw_ref[...] = _rope(x, jnp.where(is_ki, cos, 1.0), jnp.where(is_ki, s1, 0.0), jnp.where(is_ki, s2, 0.0))


def _attn_prep(p, row0, m, r, tabs, bd, qn, kn):
    nper = tabs[0].shape[0] // r
    tmap = lambda i: (i % nper, 0)
    assert row0 % r == 0 and m % r == 0 and tabs[0].shape[0] % r == 0
    base = row0 // r
    rows = lambda i: (i, 0)
    return pl.pallas_call(
        _attnprep_body,
        grid=(m // r,),
        in_specs=[pl.BlockSpec((r, 3 * WC), lambda i: (base + i, OFF_CQKV // (3 * WC))),
                  pl.BlockSpec((r, 384), lambda i: (base + i, OFF_CI // 384)),
                  pl.BlockSpec((r, LANES), tmap), pl.BlockSpec((r, LANES), tmap), pl.BlockSpec((r, LANES), tmap),
                  pl.BlockSpec((LANES, LANES), lambda i: (0, 0)),
                  pl.BlockSpec((1, LANES), lambda i: (0, 0)),
                  pl.BlockSpec((1, LANES), lambda i: (0, 0))],
        out_specs=[pl.BlockSpec((r, WC), rows), pl.BlockSpec((r, WC), rows), pl.BlockSpec((r, WC), rows),
                   pl.BlockSpec((r, HI * DI), rows), pl.BlockSpec((r, LANES), rows)],
        out_shape=[jax.ShapeDtypeStruct((m, WC), BF16), jax.ShapeDtypeStruct((m, WC), F32),
                   jax.ShapeDtypeStruct((m, WC), F32), jax.ShapeDtypeStruct((m, HI * DI), BF16),
                   jax.ShapeDtypeStruct((m, LANES), F32)],
        compiler_params=_cparams(("parallel",)),
    )(p, p, *tabs, bd, qn, kn)


def _attnprep_t_body(p_ref, ci_ref, cos_ref, s1_ref, s2_ref, bd_ref, qn_ref, kn_ref,
                     k_ref, v_ref, kiw_ref, kb_ref, kib_ref, qt_ref, qit_ref, wt_ref, vt_ref):
    cos, s1, s2 = cos_ref[...], s1_ref[...], s2_ref[...]
    bd = bd_ref[...]
    nch = WC // LANES
    r = p_ref.shape[0]
    for c in range(2 * nch):
        x = p_ref[:, c * LANES:(c + 1) * LANES]
        gain = (qn_ref if c < nch else kn_ref)[...]
        n = x * lax.rsqrt(_segsum(x * x, bd) * (1.0 / DH) + EPS) * gain
        rot = _rope(n, cos, s1, s2)
        if c < nch:
            qt_ref[c * LANES:(c + 1) * LANES, :] = (rot * (DH ** -0.5 * LOG2E)).T.astype(BF16)
        else:
            k_ref[:, (c - nch) * LANES:(c - nch + 1) * LANES] = rot
            kb_ref[:, (c - nch) * LANES:(c - nch + 1) * LANES] = rot.astype(BF16)
    v = p_ref[:, 2 * WC:3 * WC]
    v_ref[...] = v
    kt = vt_ref.shape[-1]
    for t in range(r // kt):
        for c in range(nch):
            vt_ref[t, c * LANES:(c + 1) * LANES, :] = v[t * kt:(t + 1) * kt, c * LANES:(c + 1) * LANES].T.astype(BF16)
    for c in range(HI * DI // LANES):
        x = ci_ref[:, c * LANES:(c + 1) * LANES]
        qit_ref[c * LANES:(c + 1) * LANES, :] = _rope(x, cos, s1, s2).T.astype(BF16)
    x = ci_ref[:, HI * DI:HI * DI + LANES]
    lane = lax.broadcasted_iota(I32, x.shape, 1)
    is_ki = lane < DI
    kiw = _rope(x, jnp.where(is_ki, cos, 1.0), jnp.where(is_ki, s1, 0.0), jnp.where(is_ki, s2, 0.0))
    kiw_ref[...] = kiw
    kib_ref[...] = kiw[:, 0:DI].astype(BF16)
    wt_ref[...] = kiw.T[LANE_W:LANE_W + SUBLANES, :]


def _attn_prep_t(p, bsz, t, r, kt, tabs, bd, qn, kn):
    nt = t // r
    m = bsz * t
    assert t % r == 0 and r % kt == 0 and kt % LANES == 0
    rows = lambda b, i: (b * nt + i, 0)
    tmap = lambda b, i: (i, 0)
    cst = lambda b, i: (0, 0)
    return pl.pallas_call(
        _attnprep_t_body,
        grid=(bsz, nt),
        in_specs=[pl.BlockSpec((r, 3 * WC), lambda b, i: (b * nt + i, OFF_CQKV // (3 * WC))),
                  pl.BlockSpec((r, 384), lambda b, i: (b * nt + i, OFF_CI // 384)),
                  pl.BlockSpec((r, LANES), tmap), pl.BlockSpec((r, LANES), tmap), pl.BlockSpec((r, LANES), tmap),
                  pl.BlockSpec((LANES, LANES), cst),
                  pl.BlockSpec((1, LANES), cst),
                  pl.BlockSpec((1, LANES), cst)],
        out_specs=[pl.BlockSpec((r, WC), rows), pl.BlockSpec((r, WC), rows), pl.BlockSpec((r, LANES), rows),
                   pl.BlockSpec((r, WC), rows), pl.BlockSpec((r, DI), rows),
                   pl.BlockSpec((None, WC, r), lambda b, i: (b, 0, i)),
                   pl.BlockSpec((None, HI * DI, r), lambda b, i: (b, 0, i)),
                   pl.BlockSpec((None, SUBLANES, r), lambda b, i: (b, 0, i)),
                   pl.BlockSpec((None, r // kt, WC, kt), lambda b, i: (b, i, 0, 0))],
        out_shape=[jax.ShapeDtypeStruct((m, WC), F32), jax.ShapeDtypeStruct((m, WC), F32),
                   jax.ShapeDtypeStruct((m, LANES), F32),
                   jax.ShapeDtypeStruct((m, WC), BF16), jax.ShapeDtypeStruct((m, DI), BF16),
                   jax.ShapeDtypeStruct((bsz, WC, t), BF16),
                   jax.ShapeDtypeStruct((bsz, HI * DI, t), BF16),
                   jax.ShapeDtypeStruct((bsz, SUBLANES, t), F32),
                   jax.ShapeDtypeStruct((bsz, t // kt, WC, kt), BF16)],
        compiler_params=_cparams(("parallel", "parallel")),
    )(p, p, *tabs, bd, qn, kn)


def _float_key(sc):
    bits = lax.bitcast_convert_type(sc, I32)
    return bits ^ ((bits >> 31) & 0x7FFFFFFF)


def _kth_largest_key(count_ge, shape, topk):
    base = jnp.where(count_ge(jnp.zeros(shape, I32)) >= topk, 0, INT_MIN).astype(I32)

    def bit_step(i, base):
        cand = base + (jnp.int32(1) << (30 - i))
        return jnp.where(count_ge(cand) >= topk, cand, base)

    return lax.fori_loop(0, 31, bit_step, base)


def _select_tile(key, vstar, need, running, tri, visible):
    eq = key == vstar
    pref = _dot(eq.astype(BF16), tri)
    sel = (key > vstar) | (eq & (running + pref <= need))
    if visible is not None:
        sel = sel & visible
    return sel, running + pref[:, LANES - 1:LANES]


def _attn_prompt_body(qt_ref, qit_ref, wt_ref, k_ref, vt_ref, ki_ref, ltri_ref, o_ref,
                      key_scr, acc_scr, m_scr, l_scr, raw_scr, s_scr, p_scr, a_scr, bias_scr, *, tq, kt, topk):
    qblk = pl.program_id(1)
    nkb = (qblk * tq + tq + kt - 1) // kt
    qpos = qblk * tq + lax.broadcasted_iota(I32, (1, tq), 1)
    krow = lax.broadcasted_iota(I32, (kt, 1), 0)
    w = wt_ref[...] * ((HI * DI) ** -0.5)
    qit = qit_ref[...]
    qi_cat = jnp.concatenate([qit[h * DI:(h + 1) * DI, :] for h in range(HI)], axis=1)

    def key_tile(j):
        return pl.ds(pl.multiple_of(jnp.minimum(j, nkb - 1) * kt, kt), kt)

    def idx_dot(j):
        return _dot(ki_ref[key_tile(j), :], qi_cat)

    npairs_kt = (nkb + 1) // 2

    def fill_tile(j, cur):
        raw_scr[1 - cur] = idx_dot(j + 1)
        sc = None
        for h in range(HI):
            term = w[h:h + 1, :] * jnp.maximum(raw_scr[cur, :, h * tq:(h + 1) * tq], 0.0)
            sc = term if sc is None else sc + term
        key_scr[j] = jnp.where((krow + j * kt) <= qpos, _float_key(sc), INT_MIN)

    def fill(i, carry):
        fill_tile(2 * i, 0)
        fill_tile(2 * i + 1, 1)
        return carry

    raw_scr[0] = idx_dot(0)
    lax.fori_loop(0, npairs_kt, fill, 0)

    def count_ge(cand):
        def body(i, acc):
            for j in (2 * i, 2 * i + 1):
                ind = jnp.where(key_scr[j] >= cand, 1.0, 0.0)
                acc = acc + jnp.sum(ind.reshape(kt // COUNT_ROWS, COUNT_ROWS, tq), axis=0)
            return acc
        acc = lax.fori_loop(0, npairs_kt, body, jnp.zeros((COUNT_ROWS, tq), F32))
        return jnp.sum(acc, axis=0, keepdims=True)

    npair = HC // 2
    half = lax.broadcasted_iota(I32, (LANES, tq), 0) // DH
    q_pairs = []
    for c in range(npair):
        qc = qt_ref[c * LANES:(c + 1) * LANES, :].astype(F32)
        q_pairs.append(jnp.concatenate([jnp.where(half == 0, qc, 0.0), jnp.where(half == 1, qc, 0.0)],
                                       axis=1).astype(BF16))

    def store_scores(j, slot):
        kb = k_ref[key_tile(j), :]
        for c in range(npair):
            s_scr[slot, c] = _dot(kb[:, c * LANES:(c + 1) * LANES], q_pairs[c])

    store_scores(0, 0)
    vstar = _kth_largest_key(count_ge, (1, tq), topk)
    need = topk - count_ge(vstar + 1)

    m_scr[...] = jnp.full_like(m_scr, NEG_BIG)
    l_scr[...] = jnp.zeros_like(l_scr)
    acc_scr[...] = jnp.zeros_like(acc_scr)
    ltri = ltri_ref[...]

    def store_bias(j, slot, running):
        key = key_scr[jnp.minimum(j, nkb - 1)]
        eq = key == vstar
        pref = _dot(ltri, eq.astype(BF16))
        sel = ((key > vstar) | (eq & (running + pref <= need))) & ((krow + j * kt) <= qpos)
        bias_scr[slot] = jnp.where(sel, 0.0, -jnp.inf)
        return running + pref[kt - 1:kt, :]

    def apply_pv(j, slot):
        vtb = vt_ref[jnp.clip(j, 0, nkb - 1)]
        for c in range(npair):
            pv = _dot(vtb[c * LANES:(c + 1) * LANES, :], p_scr[slot, c])
            for hh in range(2):
                h = 2 * c + hh
                acc_scr[h * DH:(h + 1) * DH, :] = (a_scr[slot, h:h + 1, :] * acc_scr[h * DH:(h + 1) * DH, :]
                                                   + pv[hh * DH:(hh + 1) * DH, hh * tq:(hh + 1) * tq])

    def sweep_tile(j, cur, running):
        nxt = 1 - cur
        apply_pv(j - 1, nxt)
        store_scores(j + 1, nxt)
        running = store_bias(j + 1, nxt, running)
        bias = bias_scr[cur]
        for c in range(npair):
            pc = []
            for hh in range(2):
                h = 2 * c + hh
                s = s_scr[cur, c, :, hh * tq:(hh + 1) * tq] + bias
                m_old = m_scr[h:h + 1, :]
                m_new = jnp.maximum(m_old, jnp.max(s, axis=0, keepdims=True))
                p = jnp.exp2(s - m_new)
                alpha = jnp.exp2(m_old - m_new)
                l_scr[h:h + 1, :] = alpha * l_scr[h:h + 1, :] + jnp.sum(p, axis=0, keepdims=True)
                m_scr[h:h + 1, :] = m_new
                a_scr[cur, h:h + 1, :] = alpha
                pc.append(p.astype(BF16))
            p_scr[cur, c] = jnp.concatenate(pc, axis=1)
        return running

    def sweep(i, running):
        return sweep_tile(2 * i + 1, 1, sweep_tile(2 * i, 0, running))

    p_scr[1] = jnp.zeros_like(p_scr[1])
    a_scr[1] = jnp.ones_like(a_scr[1])
    lax.fori_loop(0, npairs_kt, sweep, store_bias(0, 0, jnp.zeros((1, tq), F32)))
    apply_pv(2 * npairs_kt - 1, 1)
    for c in range(npair):
        ot = jnp.concatenate([acc_scr[h * DH:(h + 1) * DH, :] / l_scr[h:h + 1, :] for h in (2 * c, 2 * c + 1)],
                             axis=0)
        o_ref[:, c * LANES:(c + 1) * LANES] = ot.T


def _attn_prompt(qt, qit, wt, kbf, vt, kibf, ltri, bsz, s_len):
    tq = LANES
    kt = vt.shape[-1]
    nq = s_len // tq
    topk = min(TOPK_MAX, s_len // 4)
    assert s_len % kt == 0 and kt % tq == 0
    k3 = kbf.reshape(bsz, s_len, WC)
    ki3 = kibf.reshape(bsz, s_len, DI)
    return pl.pallas_call(
        functools.partial(_attn_prompt_body, tq=tq, kt=kt, topk=topk),
        grid=(bsz, nq),
        in_specs=[pl.BlockSpec((None, WC, tq), lambda b, i: (b, 0, i)),
                  pl.BlockSpec((None, HI * DI, tq), lambda b, i: (b, 0, i)),
                  pl.BlockSpec((None, SUBLANES, tq), lambda b, i: (b, 0, i)),
                  pl.BlockSpec((None, s_len, WC), lambda b, i: (b, 0, 0)),
                  pl.BlockSpec((None, s_len // kt, WC, kt), lambda b, i: (b, 0, 0, 0)),
                  pl.BlockSpec((None, s_len, DI), lambda b, i: (b, 0, 0)),
                  pl.BlockSpec((kt, kt), lambda b, i: (0, 0))],
        out_specs=pl.BlockSpec((tq, WC), lambda b, i: (b * nq + i, 0)),
        out_shape=jax.ShapeDtypeStruct((bsz * s_len, WC), F32),
        scratch_shapes=[pltpu.VMEM((s_len // kt, kt, tq), I32),
                        pltpu.VMEM((WC, tq), F32),
                        pltpu.VMEM((SUBLANES, tq), F32),
                        pltpu.VMEM((SUBLANES, tq), F32),
                        pltpu.VMEM((2, kt, HI * tq), F32),
                        pltpu.VMEM((2, HC // 2, kt, 2 * tq), F32),
                        pltpu.VMEM((2, HC // 2, kt, 2 * tq), BF16),
                        pltpu.VMEM((2, SUBLANES, tq), F32),
                        pltpu.VMEM((2, kt, tq), F32)],
        compiler_params=_cparams(("parallel", "arbitrary")),
    )(qt, qit, wt, k3, vt, ki3, ltri)


def _attn_sample_body(pt_ref, q_ref, qi_ref, kiw_ref, knew_ref, vnew_ref, tri_ref, *refs, t, npg, topk):
    kidx_refs, k_refs, v_refs = refs[0:npg], refs[npg:2 * npg], refs[2 * npg:3 * npg]
    o_ref, s_scr, kipad_scr, kpad_scr, vpad_scr = refs[3 * npg:]
    nt = npg + 1
    qi = qi_ref[...]
    qi_stack = jnp.concatenate([qi[:, h * DI:(h + 1) * DI] for h in range(HI)], axis=0)
    w = kiw_ref[:, LANE_W:LANE_W + HI] * ((HI * DI) ** -0.5)
    w_stack = jnp.concatenate([w[:, h:h + 1] for h in range(HI)], axis=0)

    def score_keys(kib):
        rel = jnp.maximum(_dot_nt(qi_stack, kib), 0.0) * w_stack
        sc = rel[0:t]
        for h in range(1, HI):
            sc = sc + rel[h * t:(h + 1) * t]
        return _float_key(sc)

    keys = [score_keys(kidx_refs[j][...].astype(BF16)) for j in range(npg)]
    kipad_scr[...] = jnp.zeros_like(kipad_scr)
    kipad_scr[0:t, :] = kiw_ref[:, 0:DI]
    vis_new = lax.broadcasted_iota(I32, (t, LANES), 1) <= lax.broadcasted_iota(I32, (t, LANES), 0)
    keys.append(jnp.where(vis_new, score_keys(kipad_scr[...].astype(BF16)), INT_MIN))

    def count_ge(cand):
        acc = jnp.zeros((t, LANES), F32)
        for key in keys:
            acc = acc + jnp.where(key >= cand, 1.0, 0.0)
        return jnp.sum(acc, axis=1, keepdims=True)

    vstar = _kth_largest_key(count_ge, (t, 1), topk)
    need = topk - count_ge(vstar + 1)

    kpad_scr[...] = jnp.zeros_like(kpad_scr)
    vpad_scr[...] = jnp.zeros_like(vpad_scr)
    kpad_scr[0:t, :] = knew_ref[...]
    vpad_scr[0:t, :] = vnew_ref[...]
    head_of_lane = lax.broadcasted_iota(I32, (t, WC), 1) // DH
    q = q_ref[...].astype(F32)
    q_stack = jnp.concatenate([jnp.where(head_of_lane == h, q, 0.0) for h in range(HC)], axis=0).astype(BF16)
    tri = tri_ref[...]
    running = jnp.zeros((t, 1), F32)
    m_acc = jnp.full((HC * t, LANES), NEG_BIG, F32)
    for j in range(nt):
        sel, running = _select_tile(keys[j], vstar, need, running, tri, vis_new if j == npg else None)
        bias = jnp.where(sel, 0.0, -jnp.inf)
        kb = (k_refs[j] if j < npg else kpad_scr)[...].astype(BF16)
        s = _dot_nt(q_stack, kb) + jnp.concatenate([bias] * HC, axis=0)
        s_scr[j] = s
        m_acc = jnp.maximum(m_acc, s)
    m = jnp.max(m_acc, axis=1, keepdims=True)
    l_acc = jnp.zeros((HC * t, LANES), F32)
    acc = jnp.zeros((HC * t, WC), F32)
    for j in range(nt):
        p = jnp.exp(s_scr[j] - m)
        l_acc = l_acc + p
        vb = (v_refs[j] if j < npg else vpad_scr)[...].astype(BF16)
        acc = acc + _dot(p.astype(BF16), vb)
    o = acc / jnp.sum(l_acc, axis=1, keepdims=True)
    out = jnp.where(head_of_lane == 0, o[0:t], 0.0)
    for h in range(1, HC):
        out = out + jnp.where(head_of_lane == h, o[h * t:(h + 1) * t], 0.0)
    o_ref[...] = out


def _attn_sample(page_table, q, qi, kiw, knew, vnew, tri, cache_kidx, cache_k, cache_v, layer, bsz, t):
    npg = page_table.shape[1]
    page = cache_k.shape[2]
    assert page == LANES
    topk = min(TOPK_MAX, (npg * page + t) // 4)
    row = lambda b, pt: (b, 0)

    def page_map(j):
        return lambda b, pt: (layer, pt[b, j], 0, 0)

    in_specs = [pl.BlockSpec((t, WC), row), pl.BlockSpec((t, HI * DI), row), pl.BlockSpec((t, LANES), row),
                pl.BlockSpec((t, WC), row), pl.BlockSpec((t, WC), row),
                pl.BlockSpec((LANES, LANES), lambda b, pt: (0, 0))]
    in_specs += [pl.BlockSpec((None, None, page, DI), page_map(j)) for j in range(npg)]
    in_specs += [pl.BlockSpec((None, None, page, WC), page_map(j)) for j in range(npg)]
    in_specs += [pl.BlockSpec((None, None, page, WC), page_map(j)) for j in range(npg)]
    grid_spec = pltpu.PrefetchScalarGridSpec(
        num_scalar_prefetch=1,
        grid=(bsz,),
        in_specs=in_specs,
        out_specs=pl.BlockSpec((t, WC), row),
        scratch_shapes=[pltpu.VMEM((npg + 1, HC * t, LANES), F32), pltpu.VMEM((LANES, DI), F32),
                        pltpu.VMEM((LANES, WC), F32), pltpu.VMEM((LANES, WC), F32)])
    return pl.pallas_call(
        functools.partial(_attn_sample_body, t=t, npg=npg, topk=topk),
        grid_spec=grid_spec,
        out_shape=jax.ShapeDtypeStruct((bsz * t, WC), F32),
        compiler_params=_cparams(("parallel",)),
    )(page_table, q, qi, kiw, knew, vnew, tri,
      *([cache_kidx] * npg), *([cache_k] * npg), *([cache_v] * npg))


def _rope_tables(pos):
    half = ROPE_DIM // 2
    inv_freq = ROPE_THETA ** (-jnp.arange(half, dtype=F32) / half)
    ang = pos.astype(F32)[:, None] * inv_freq[None, :]
    cos, sin = jnp.cos(ang), jnp.sin(ang)
    n = pos.shape[0]
    one = jnp.ones((n, DH - ROPE_DIM), F32)
    zero_h = jnp.zeros((n, half), F32)
    zero_r = jnp.zeros((n, DH - ROPE_DIM), F32)
    c64 = jnp.concatenate([cos, cos, one], axis=1)
    s1 = jnp.concatenate([-sin, zero_h, zero_r], axis=1)
    s2 = jnp.concatenate([zero_h, sin, zero_r], axis=1)
    rep = LANES // DH
    return tuple(jnp.tile(a, (1, rep)) for a in (c64, s1, s2))


def _reorder_w_in(w_in):
    sizes = (WA, WA, WA, QKV_B, WB, HB, HB, 3 * WC, HI * DI, DI, HI)
    offs = [0]
    for s in sizes:
        offs.append(offs[-1] + s)
    seg = lambda i: w_in[:, :, offs[i]:offs[i + 1]]
    pad = jnp.zeros(w_in.shape[:2] + (LANES - DI - HI - 2 * HB,), w_in.dtype)
    order = [seg(3), seg(7), seg(0), seg(1), seg(2), seg(4), seg(8), seg(9), seg(10), seg(5), seg(6), pad]
    return jnp.concatenate(order, axis=-1).astype(BF16)


def _pad_state(prev):
    return jnp.pad(prev, ((0, 0), (SUBLANES - prev.shape[1], 0), (0, 0)))


def kernel(x_prompt, x_sample, state_conv_a, state_gdn_conv, state_gdn, cache_k, cache_v, cache_kidx, page_table, ffn1_norm, ffn1_w_gate, ffn1_w_up, ffn1_w_down, mix_norm, w_in, conv_a_w, gdn_conv_w, gdn_a_log, gdn_dt_bias, gdn_out_norm, attn_q_norm, attn_k_norm, w_out, ffn2_norm, ffn2_w_gate, ffn2_w_up, ffn2_w_down):
    depth = w_in.shape[0]
    bp, tp, d = x_prompt.shape
    bs, ts, _ = x_sample.shape
    mp, ms = bp * tp, bs * ts
    npool, page = cache_k.shape[1], cache_k.shape[2]
    past = page_table.shape[1] * page
    assert ts == SUBLANES and tp % LANES == 0 and mp % 256 == 0

    x = jnp.concatenate([x_prompt.reshape(mp, d), x_sample.reshape(ms, d)], axis=0)
    w_in_r = _reorder_w_in(w_in)
    bf = lambda a: a.astype(BF16)
    f1g, f1u, f1d = bf(ffn1_w_gate), bf(ffn1_w_up), bf(ffn1_w_down)
    f2g, f2u, f2d = bf(ffn2_w_gate), bf(ffn2_w_up), bf(ffn2_w_down)
    w_out_b = bf(w_out)
    ck = cache_k.reshape(depth, npool, page, WC)
    cv = cache_v.reshape(depth, npool, page, WC)

    li = jnp.arange(LANES)
    bd = ((li[:, None] // DH) == (li[None, :] // DH)).astype(BF16)
    tri = (li[:, None] <= li[None, :]).astype(BF16)
    kt = r_p = _pick(tp, (256, 128))
    lk = jnp.arange(kt)
    ltri = (lk[None, :] <= lk[:, None]).astype(BF16)
    r_s = _pick(ms, (128, 64, 32, 16, 8))
    tabs_p = _rope_tables(jnp.arange(tp))
    tabs_s = _rope_tables(jnp.tile(past + jnp.arange(ts), r_s // ts))
    zero_a = jnp.zeros((bp, SUBLANES, WA), F32)
    zero_b = jnp.zeros((bp, SUBLANES, QKV_B), F32)
    zero_s = jnp.zeros((bp, HB, DKB, DVB), F32)
    lane_pad = lambda v, off: jnp.zeros((1, LANES), F32).at[0, off:off + v.shape[0]].set(v)

    outs = [[] for _ in range(12)]
    for l in range(depth):
        x = _ffn(x, ffn1_norm[l][None], f1g[l], f1u[l], f1d[l])
        p = _inproj(x, mix_norm[l][None], w_in_r[l])
        avec = lane_pad(jnp.exp(gdn_a_log[l].astype(F32)), LANE_GA)
        dtb = lane_pad(gdn_dt_bias[l].astype(F32), LANE_GA)
        gn = gdn_out_norm[l][None].astype(F32)
        qn = jnp.tile(attn_q_norm[l].astype(F32), LANES // DH)[None]
        kn = jnp.tile(attn_k_norm[l].astype(F32), LANES // DH)[None]

        ya, yb, yc = [], [], []
        groups = ((0, bp, tp, zero_a, zero_b, zero_s), (mp, bs, ts, _pad_state(state_conv_a[l]),
                                                       _pad_state(state_gdn_conv[l]), state_gdn[l]))
        for gi, (row0, bsz, t, prev_a, prev_b, s0) in enumerate(groups):
            m = bsz * t
            y_a, st_a = _mixer_a(p, row0, bsz, t, conv_a_w[l], prev_a)
            qg, kg, vg, gbg, st_b = _gdn_prep(p, row0, bsz, t, gdn_conv_w[l], prev_b, bd, avec, dtb)
            y_b, s_new = _gdn(qg, kg, vg, gbg, p, row0, bsz, t, s0, gn)
            if gi == 0:
                kc, vc, kiw, kbf, kibf, qt, qit, wt, vt = _attn_prep_t(p, bsz, t, r_p, kt, tabs_p, bd, qn, kn)
                y_c = _attn_prompt(qt, qit, wt, kbf, vt, kibf, ltri, bsz, t)
            else:
                qc, kc, vc, qi, kiw = _attn_prep(p, row0, m, r_s, tabs_s, bd, qn, kn)
                y_c = _attn_sample(page_table, qc, qi, kiw, kc, vc, tri, cache_kidx, ck, cv, l, bsz, t)
            ya.append(y_a)
            yb.append(y_b)
            yc.append(y_c)
            res = (st_a[:, SUBLANES - (CONV_A - 1):], st_b[:, SUBLANES - (GDN_CONV - 1):], s_new,
                   kc.reshape(bsz, t, HC, DH), vc.reshape(bsz, t, HC, DH), kiw[:, :DI].reshape(bsz, t, DI))
            for i, r in enumerate(res):
                outs[gi * 6 + i].append(r)
        x = _outproj(x, jnp.concatenate(ya), jnp.concatenate(yb), jnp.concatenate(yc), w_out_b[l])
        x = _ffn(x, ffn2_norm[l][None], f2g[l], f2u[l], f2d[l])

    y_prompt = x[:mp].reshape(bp, tp, d)
    y_sample = x[mp:].reshape(bs, ts, d)
    return (y_prompt, y_sample) + tuple(jnp.stack(o) for o in outs)
```

```python
import functools
import math

import jax
import jax.numpy as jnp
from jax import lax
from jax.experimental import pallas as pl
from jax.experimental.pallas import tpu as pltpu

F32 = jnp.float32
BF16 = jnp.bfloat16
I32 = jnp.int32

EPS = 1e-6
LANES = 128
SUBLANES = 8
VMEM_LIMIT = 56 * 1024 * 1024

A_GROUPS = 4
CONV_A = 3
HB = 6
DKB = 64
DVB = 64
GDN_CONV = 4
GDN_ROWS = 64
HC = 6
DH = 64
HI = 4
DI = 64
TOPK_MAX = 256
ROPE_DIM = DH // 4
ROPE_THETA = 500000.0
INT_MIN = -2 ** 31
COUNT_ROWS = 4 * SUBLANES
NEG_BIG = -1e30
LOG2E = math.log2(math.e)

WA = 256
QKV_B = 2 * HB * DKB + HB * DVB
WB = HB * DVB
WC = HC * DH
OFF_BQKV = 0
OFF_CQKV = QKV_B
OFF_A = 2 * QKV_B
OFF_BZ = OFF_A + 3 * WA
OFF_CI = OFF_BZ + WB
N_P = OFF_CI + 384
LANE_W = DI
LANE_GA = DI + HI
LANE_GB = DI + HI + HB


def _pick(n, cands):
    for c in cands:
        if n % c == 0:
            return c
    raise ValueError(f"no tile for {n}")


def _cparams(sem):
    return pltpu.CompilerParams(dimension_semantics=sem, vmem_limit_bytes=VMEM_LIMIT)


def _dot(a, b):
    return jnp.dot(a, b, preferred_element_type=F32)


def _dot_nt(a, b):
    return lax.dot_general(a, b, (((1,), (1,)), ((), ())), preferred_element_type=F32)


def _split3(x):
    hi = x.astype(BF16)
    r1 = x - hi.astype(F32)
    mid = r1.astype(BF16)
    return hi, mid, (r1 - mid.astype(F32)).astype(BF16)


def _dot01(sel01, x):
    s = sel01.astype(BF16)
    hi, mid, lo = _split3(x)
    return _dot(s, hi) + _dot(s, mid) + _dot(s, lo)


def _dot01_nt(sel01, x):
    s = sel01.astype(BF16)
    hi, mid, lo = _split3(x)
    return _dot_nt(s, hi) + _dot_nt(s, mid) + _dot_nt(s, lo)


def _dot3(a, b):
    ah = a.astype(BF16)
    al = (a - ah.astype(F32)).astype(BF16)
    bh = b.astype(BF16)
    bl = (b - bh.astype(F32)).astype(BF16)
    return _dot(ah, bh) + _dot(ah, bl) + _dot(al, bh)


def _segsum(x, bd):
    hi = x.astype(BF16)
    lo = (x - hi.astype(F32)).astype(BF16)
    return _dot(hi, bd) + _dot(lo, bd)


def _sigmoid(x):
    return 1.0 / (1.0 + jnp.exp(-x))


def _silu(x):
    return x * _sigmoid(x)


def _softplus(x):
    return jnp.maximum(x, 0.0) + jnp.log(1.0 + jnp.exp(-jnp.abs(x)))


def _ffn_body(x_ref, g_ref, wg_ref, wu_ref, wd_ref, o_ref, h_ref, acc_ref):
    j = pl.program_id(1)

    @pl.when(j == 0)
    def _():
        x = x_ref[...]
        ms = jnp.mean(x * x, axis=-1, keepdims=True)
        h_ref[...] = (x * lax.rsqrt(ms + EPS) * g_ref[...]).astype(BF16)
        acc_ref[...] = jnp.zeros_like(acc_ref)

    h = h_ref[...]
    a = _dot(h, wg_ref[...])
    b = _dot(h, wu_ref[...])
    t = (_silu(a) * b).astype(BF16)
    acc_ref[...] += _dot(t, wd_ref[...])

    @pl.when(j == pl.num_programs(1) - 1)
    def _():
        o_ref[...] = x_ref[...] + 0.5 * acc_ref[...]


def _ffn(x, g, wg, wu, wd):
    m, d = x.shape
    f = wg.shape[1]
    tm = _pick(m, (512, 256, 128, 64, 32, 16, 8))
    tf = _pick(f, (1408, 256, 128))
    return pl.pallas_call(
        _ffn_body,
        grid=(m // tm, f // tf),
        in_specs=[pl.BlockSpec((tm, d), lambda i, j: (i, 0)),
                  pl.BlockSpec((1, d), lambda i, j: (0, 0)),
                  pl.BlockSpec((d, tf), lambda i, j: (0, j)),
                  pl.BlockSpec((d, tf), lambda i, j: (0, j)),
                  pl.BlockSpec((tf, d), lambda i, j: (j, 0))],
        out_specs=pl.BlockSpec((tm, d), lambda i, j: (i, 0)),
        out_shape=jax.ShapeDtypeStruct((m, d), F32),
        scratch_shapes=[pltpu.VMEM((tm, d), BF16), pltpu.VMEM((tm, d), F32)],
        compiler_params=_cparams(("parallel", "arbitrary")),
    )(x, g, wg, wu, wd)


def _inproj_body(x_ref, g_ref, w_ref, o_ref, h_ref):
    @pl.when(pl.program_id(1) == 0)
    def _():
        x = x_ref[...]
        ms = jnp.mean(x * x, axis=-1, keepdims=True)
        h_ref[...] = (x * lax.rsqrt(ms + EPS) * g_ref[...]).astype(BF16)

    o_ref[...] = _dot(h_ref[...], w_ref[...])


def _inproj(x, g, w):
    m, d = x.shape
    n = w.shape[1]
    tm = _pick(m, (512, 256, 128, 64, 32, 16, 8))
    tn = _pick(n, (3840, 1920, 768, 384, 128))
    return pl.pallas_call(
        _inproj_body,
        grid=(m // tm, n // tn),
        in_specs=[pl.BlockSpec((tm, d), lambda i, j: (i, 0)),
                  pl.BlockSpec((1, d), lambda i, j: (0, 0)),
                  pl.BlockSpec((d, tn), lambda i, j: (0, j))],
        out_specs=pl.BlockSpec((tm, tn), lambda i, j: (i, j)),
        out_shape=jax.ShapeDtypeStruct((m, n), F32),
        scratch_shapes=[pltpu.VMEM((tm, d), BF16)],
        compiler_params=_cparams(("parallel", "arbitrary")),
    )(x, g, w)


def _outproj_body(x_ref, ya_ref, yb_ref, yc_ref, w_ref, o_ref):
    acc = _dot(ya_ref[...].astype(BF16), w_ref[0:WA, :])
    acc += _dot(yb_ref[...].astype(BF16), w_ref[WA:WA + WB, :])
    acc += _dot(yc_ref[...].astype(BF16), w_ref[WA + WB:WA + WB + WC, :])
    o_ref[...] = x_ref[...] + acc


def _outproj(x, ya, yb, yc, w):
    m, d = x.shape
    tm = _pick(m, (512, 256, 128, 64, 32, 16, 8))
    return pl.pallas_call(
        _outproj_body,
        grid=(m // tm,),
        in_specs=[pl.BlockSpec((tm, d), lambda i: (i, 0)),
                  pl.BlockSpec((tm, WA), lambda i: (i, 0)),
                  pl.BlockSpec((tm, WB), lambda i: (i, 0)),
                  pl.BlockSpec((tm, WC), lambda i: (i, 0)),
                  pl.BlockSpec(w.shape, lambda i: (0, 0))],
        out_specs=pl.BlockSpec((tm, d), lambda i: (i, 0)),
        out_shape=jax.ShapeDtypeStruct((m, d), F32),
        compiler_params=_cparams(("parallel",)),
    )(x, ya, yb, yc, w)


def _conv_taps(u, w_ref, prev_ref, ext_ref, tt, width):
    it = pl.program_id(1)

    @pl.when(it == 0)
    def _():
        ext_ref[:, 0:SUBLANES, :] = prev_ref[...]

    @pl.when(it > 0)
    def _():
        ext_ref[:, 0:SUBLANES, :] = ext_ref[:, tt:tt + SUBLANES, :]

    ext_ref[:, SUBLANES:SUBLANES + tt, :] = u
    acc = None
    for s in range(width):
        term = ext_ref[:, SUBLANES - s:SUBLANES - s + tt, :] * w_ref[width - 1 - s:width - s, :]
        acc = term if acc is None else acc + term
    return acc, ext_ref[:, tt:tt + SUBLANES, :]


def _group_tiles(bsz, t):
    if t >= 128:
        return 1, _pick(t, (256, 128))
    return _pick(bsz, (16, 8, 4, 2, 1)), t


def _mixa_body(p_ref, w_ref, prev_ref, y_ref, st_ref, ext_ref, *, bb, tt):
    pa = p_ref[...]
    u = (pa[:, WA:2 * WA] * pa[:, 2 * WA:3 * WA]).reshape(bb, tt, WA)
    conv, tail = _conv_taps(u, w_ref, prev_ref, ext_ref, tt, CONV_A)
    y_ref[...] = pa[:, 0:WA] * conv.reshape(bb * tt, WA)
    st_ref[...] = tail


def _mixer_a(p, row0, bsz, t, w, prev8):
    bb, tt = _group_tiles(bsz, t)
    r = bb * tt
    nb, nt = bsz // bb, t // tt
    base = row0 // r
    assert row0 % r == 0
    return pl.pallas_call(
        functools.partial(_mixa_body, bb=bb, tt=tt),
        grid=(nb, nt),
        in_specs=[pl.BlockSpec((r, 3 * WA), lambda i, j: (base + i * nt + j, OFF_A // (3 * WA))),
                  pl.BlockSpec(w.shape, lambda i, j: (0, 0)),
                  pl.BlockSpec((bb, SUBLANES, WA), lambda i, j: (i, 0, 0))],
        out_specs=[pl.BlockSpec((r, WA), lambda i, j: (i * nt + j, 0)),
                   pl.BlockSpec((bb, SUBLANES, WA), lambda i, j: (i, 0, 0))],
        out_shape=[jax.ShapeDtypeStruct((bsz * t, WA), F32),
                   jax.ShapeDtypeStruct((bsz, SUBLANES, WA), F32)],
        scratch_shapes=[pltpu.VMEM((bb, tt + SUBLANES, WA), F32)],
        compiler_params=_cparams(("parallel", "arbitrary")),
    )(p, w, prev8)


def _gdnprep_body(p_ref, ci_ref, w_ref, prev_ref, bd_ref, avec_ref, dtb_ref,
                  q_ref, k_ref, v_ref, gb_ref, st_ref, ext_ref, *, bb, tt):
    r = bb * tt
    x = p_ref[...].reshape(bb, tt, QKV_B)
    conv, tail = _conv_taps(x, w_ref, prev_ref, ext_ref, tt, GDN_CONV)
    st_ref[...] = tail
    a = _silu(conv.reshape(r, QKV_B))
    bd = bd_ref[...]
    nq = HB * DKB // LANES
    for c in range(2 * nq):
        xc = a[:, c * LANES:(c + 1) * LANES]
        n = xc * lax.rsqrt(_segsum(xc * xc, bd) + EPS)
        if c < nq:
            q_ref[:, c * LANES:(c + 1) * LANES] = n * (DKB ** -0.5)
        else:
            k_ref[:, (c - nq) * LANES:(c - nq + 1) * LANES] = n
    v_ref[...] = a[:, 2 * HB * DKB:]
    raw = ci_ref[...]
    g = -avec_ref[...] * _softplus(raw + dtb_ref[...])
    lane = lax.broadcasted_iota(I32, raw.shape, 1)
    gb_ref[...] = jnp.where((lane >= LANE_GB) & (lane < LANE_GB + HB), _sigmoid(raw), g)


def _gdn_prep(p, row0, bsz, t, w, prev8, bd, avec, dtb):
    bb, tt = _group_tiles(bsz, t)
    r = bb * tt
    nb, nt = bsz // bb, t // tt
    base = row0 // r
    assert row0 % r == 0
    m = bsz * t
    rows = lambda i, j: (i * nt + j, 0)
    return pl.pallas_call(
        functools.partial(_gdnprep_body, bb=bb, tt=tt),
        grid=(nb, nt),
        in_specs=[pl.BlockSpec((r, QKV_B), lambda i, j: (base + i * nt + j, OFF_BQKV // QKV_B)),
                  pl.BlockSpec((r, LANES), lambda i, j: (base + i * nt + j, (OFF_CI + 2 * LANES) // LANES)),
                  pl.BlockSpec(w.shape, lambda i, j: (0, 0)),
                  pl.BlockSpec((bb, SUBLANES, QKV_B), lambda i, j: (i, 0, 0)),
                  pl.BlockSpec((LANES, LANES), lambda i, j: (0, 0)),
                  pl.BlockSpec((1, LANES), lambda i, j: (0, 0)),
                  pl.BlockSpec((1, LANES), lambda i, j: (0, 0))],
        out_specs=[pl.BlockSpec((r, HB * DKB), rows),
                   pl.BlockSpec((r, HB * DKB), rows),
                   pl.BlockSpec((r, WB), rows),
                   pl.BlockSpec((r, LANES), rows),
                   pl.BlockSpec((bb, SUBLANES, QKV_B), lambda i, j: (i, 0, 0))],
        out_shape=[jax.ShapeDtypeStruct((m, HB * DKB), F32),
                   jax.ShapeDtypeStruct((m, HB * DKB), F32),
                   jax.ShapeDtypeStruct((m, WB), F32),
                   jax.ShapeDtypeStruct((m, LANES), F32),
                   jax.ShapeDtypeStruct((bsz, SUBLANES, QKV_B), F32)],
        scratch_shapes=[pltpu.VMEM((bb, tt + SUBLANES, QKV_B), F32)],
        compiler_params=_cparams(("parallel", "arbitrary")),
    )(p, p, w, prev8, bd, avec, dtb)


def _gdn_body(q_ref, k_ref, v_ref, gb_ref, z_ref, s0_ref, gn_ref, y_ref, sout_ref, s_scr, *, c, g, nch):
    r = 2 * GDN_ROWS
    n = 2 * g
    it = pl.program_id(1)

    @pl.when(it == 0)
    def _():
        s_scr[...] = s0_ref[...]

    ri = lax.broadcasted_iota(I32, (r, r), 0)
    ci = lax.broadcasted_iota(I32, (r, r), 1)
    same = (ri // c) == (ci // c)
    incl = same & (ci <= ri)
    strict = same & (ci < ri)
    eye = (ri == ci).astype(F32)
    lastsel = (ci == (ri // c) * c + (c - 1)).astype(F32)
    chunks = range(nch)
    rows_of = lambda ch: slice(ch * GDN_ROWS, (ch + 1) * GDN_ROWS)
    gb = [jnp.concatenate([gb_ref[rows_of(ch), :]] * 2, axis=0) for ch in chunks]
    gcum = [_dot01(incl, gb[ch]) for ch in chunks]
    glast = [_dot01(lastsel, gcum[ch]) for ch in chunks]
    lane = lax.broadcasted_iota(I32, (r, LANES), 1)
    lane0 = (lane == 0).astype(F32)
    upper = lax.broadcasted_iota(I32, (r, 1), 0) < GDN_ROWS
    er = lax.broadcasted_iota(I32, (r, n * DKB), 0)
    ec = lax.broadcasted_iota(I32, (r, n * DKB), 1)
    emask = (er // c) == (ec // DKB)
    tr = lax.broadcasted_iota(I32, (DKB, n * DKB), 0)
    tc = lax.broadcasted_iota(I32, (DKB, n * DKB), 1)
    tile_mat = ((tc % DKB) == tr).astype(BF16)
    xr = lax.broadcasted_iota(I32, (n * DKB, r), 0)
    xc = lax.broadcasted_iota(I32, (n * DKB, r), 1)
    emask_t = (xr // DKB) == (xc // c)
    sr = lax.broadcasted_iota(I32, (n * DKB, DKB), 0)
    sc = lax.broadcasted_iota(I32, (n * DKB, DKB), 1)
    tile_mat_t = ((sr % DKB) == sc).astype(BF16)
    esel = (xc == (xr // DKB) * c + (c - 1)).astype(F32)
    gend = [_dot01(esel, gcum[ch]) for ch in chunks]
    s_upper = lax.broadcasted_iota(I32, (n * DKB, 1), 0) < g * DKB
    n_sq = max(0, int(math.ceil(math.log2(c))) - 1)

    npair = HB // 2
    items = [(ch, i) for ch in chunks for i in range(npair)]
    each = lambda f: [f(n_, ch, i) for n_, (ch, i) in enumerate(items)]

    def stack(ref, width):
        return each(lambda n_, ch, i: jnp.concatenate([ref[rows_of(ch), 2 * i * width:(2 * i + 1) * width],
                                                       ref[rows_of(ch), (2 * i + 1) * width:(2 * i + 2) * width]], axis=0))

    def pick(mats, rows_upper, base):
        return each(lambda n_, ch, i: jnp.where(rows_upper, mats[ch][:, base + 2 * i:base + 2 * i + 1],
                                                mats[ch][:, base + 2 * i + 1:base + 2 * i + 2]))

    q, k, v, zz = stack(q_ref, DKB), stack(k_ref, DKB), stack(v_ref, DVB), stack(z_ref, DVB)
    gcol = pick(gcum, upper, LANE_GA)
    glc = pick(glast, upper, LANE_GA)
    beta = pick(gb, upper, LANE_GB)
    srow = [jnp.exp(x) for x in pick(gend, s_upper, LANE_GA)]
    grow = each(lambda n_, ch, i: _dot01_nt(lane0, jnp.where(lane == 0, gcol[n_], 0.0)))
    decay = each(lambda n_, ch, i: jnp.exp(jnp.where(incl, gcol[n_] - grow[n_], -jnp.inf)))
    kb = each(lambda n_, ch, i: k[n_] * beta[n_])
    a_mat = each(lambda n_, ch, i: _dot_nt(kb[n_], k[n_]) * jnp.where(strict, decay[n_], 0.0))
    pw = [-a for a in a_mat]
    t_mat = [eye + x for x in pw]
    for _ in range(n_sq):
        pw = [_dot3(x, x) for x in pw]
        t_mat = each(lambda n_, ch, i: t_mat[n_] + _dot3(t_mat[n_], pw[n_]))
    u = each(lambda n_, ch, i: _dot(t_mat[n_], v[n_] * beta[n_]))
    wk = each(lambda n_, ch, i: _dot(t_mat[n_], kb[n_] * jnp.exp(gcol[n_])))
    qk = each(lambda n_, ch, i: _dot_nt(q[n_], k[n_]) * decay[n_])
    wk_e = each(lambda n_, ch, i: jnp.where(emask, _dot(wk[n_].astype(BF16), tile_mat), 0.0))
    qg_e = each(lambda n_, ch, i: jnp.where(emask, _dot((q[n_] * jnp.exp(gcol[n_])).astype(BF16), tile_mat), 0.0))
    kd_t = each(lambda n_, ch, i: jnp.where(
        emask_t, _dot_nt(tile_mat_t, (k[n_] * jnp.exp(glc[n_] - gcol[n_])).astype(BF16)), 0.0))
    s = [s_scr[2 * i:2 * i + 2].reshape(n * DKB, DVB) for i in range(npair)]
    for ch in chunks:
        sel = [ch * npair + i for i in range(npair)]
        v_new = [u[n_] - _dot(wk_e[n_], s[i]) for i, n_ in enumerate(sel)]
        out = [_dot(qg_e[n_], s[i]) + _dot(qk[n_], v_new[i]) for i, n_ in enumerate(sel)]
        s = [s[i] * srow[n_] + _dot(kd_t[n_], v_new[i]) for i, n_ in enumerate(sel)]
        for i, n_ in enumerate(sel):
            ms = jnp.mean(out[i] * out[i], axis=-1, keepdims=True)
            y = out[i] * lax.rsqrt(ms + EPS) * gn_ref[...] * _silu(zz[n_])
            y_ref[rows_of(ch), 2 * i * DVB:(2 * i + 1) * DVB] = y[0:GDN_ROWS]
            y_ref[rows_of(ch), (2 * i + 1) * DVB:(2 * i + 2) * DVB] = y[GDN_ROWS:]
    for i in range(npair):
        s_scr[2 * i:2 * i + 2] = s[i].reshape(2, g * DKB, DVB)

    @pl.when(it == pl.num_programs(1) - 1)
    def _():
        sout_ref[...] = s_scr[...]


def _gdn(q, k, v, gb, p, row0, bsz, t, s0, gn):
    c = min(GDN_ROWS, t)
    g = GDN_ROWS // c
    nch = 2 if (g == 1 and (t // c) % 2 == 0) else 1
    rb = nch * GDN_ROWS
    assert t % (c * nch) == 0 and bsz % g == 0 and row0 % rb == 0
    nb, nt = bsz // g, t // (c * nch)
    base = row0 // rb
    rows = lambda i, j: (i * nt + j, 0)
    s0r = s0.reshape(nb, g, HB, DKB, DVB).transpose(0, 2, 1, 3, 4).reshape(nb, HB, g * DKB, DVB)
    y, s_out = pl.pallas_call(
        functools.partial(_gdn_body, c=c, g=g, nch=nch),
        grid=(nb, nt),
        in_specs=[pl.BlockSpec((rb, HB * DKB), rows),
                  pl.BlockSpec((rb, HB * DKB), rows),
                  pl.BlockSpec((rb, WB), rows),
                  pl.BlockSpec((rb, LANES), rows),
                  pl.BlockSpec((rb, WB), lambda i, j: (base + i * nt + j, OFF_BZ // WB)),
                  pl.BlockSpec((None, HB, g * DKB, DVB), lambda i, j: (i, 0, 0, 0)),
                  pl.BlockSpec((1, DVB), lambda i, j: (0, 0))],
        out_specs=[pl.BlockSpec((rb, WB), rows),
                   pl.BlockSpec((None, HB, g * DKB, DVB), lambda i, j: (i, 0, 0, 0))],
        out_shape=[jax.ShapeDtypeStruct((bsz * t, WB), F32),
                   jax.ShapeDtypeStruct((nb, HB, g * DKB, DVB), F32)],
        scratch_shapes=[pltpu.VMEM((HB, g * DKB, DVB), F32)],
        compiler_params=_cparams(("parallel", "arbitrary")),
    )(q, k, v, gb, p, s0r, gn)
    s_out = s_out.reshape(nb, HB, g, DKB, DVB).transpose(0, 2, 1, 3, 4).reshape(bsz, HB, DKB, DVB)
    return y, s_out


def _rope(x, cos, s1, s2):
    half = ROPE_DIM // 2
    return x * cos + pltpu.roll(x, LANES - half, 1) * s1 + pltpu.roll(x, half, 1) * s2


def _attnprep_body(p_ref, ci_ref, cos_ref, s1_ref, s2_ref, bd_ref, qn_ref, kn_ref,
                   q_ref, k_ref, v_ref, qi_ref, kiw_ref):
    cos, s1, s2 = cos_ref[...], s1_ref[...], s2_ref[...]
    bd = bd_ref[...]
    nch = WC // LANES
    for c in range(2 * nch):
        x = p_ref[:, c * LANES:(c + 1) * LANES]
        gain = (qn_ref if c < nch else kn_ref)[...]
        n = x * lax.rsqrt(_segsum(x * x, bd) * (1.0 / DH) + EPS) * gain
        rot = _rope(n, cos, s1, s2)
        if c < nch:
            q_ref[:, c * LANES:(c + 1) * LANES] = (rot * (DH ** -0.5)).astype(BF16)
        else:
            k_ref[:, (c - nch) * LANES:(c - nch + 1) * LANES] = rot
    v_ref[...] = p_ref[:, 2 * WC:3 * WC]
    for c in range(HI * DI // LANES):
        x = ci_ref[:, c * LANES:(c + 1) * LANES]
        qi_ref[:, c * LANES:(c + 1) * LANES] = _rope(x, cos, s1, s2).astype(BF16)
    x = ci_ref[:, HI * DI:HI * DI + LANES]
    lane = lax.broadcasted_iota(I32, x.shape, 1)
    is_ki = lane < DI
    kiw_ref[...] = _rope(x, jnp.where(is_ki, cos, 1.0), jnp.where(is_ki, s1, 0.0), jnp.where(is_ki, s2, 0.0))


def _attn_prep(p, row0, m, r, tabs, bd, qn, kn):
    nper = tabs[0].shape[0] // r
    tmap = lambda i: (i % nper, 0)
    assert row0 % r == 0 and m % r == 0 and tabs[0].shape[0] % r == 0
    base = row0 // r
    rows = lambda i: (i, 0)
    return pl.pallas_call(
        _attnprep_body,
        grid=(m // r,),
        in_specs=[pl.BlockSpec((r, 3 * WC), lambda i: (base + i, OFF_CQKV // (3 * WC))),
                  pl.BlockSpec((r, 384), lambda i: (base + i, OFF_CI // 384)),
                  pl.BlockSpec((r, LANES), tmap), pl.BlockSpec((r, LANES), tmap), pl.BlockSpec((r, LANES), tmap),
                  pl.BlockSpec((LANES, LANES), lambda i: (0, 0)),
                  pl.BlockSpec((1, LANES), lambda i: (0, 0)),
                  pl.BlockSpec((1, LANES), lambda i: (0, 0))],
        out_specs=[pl.BlockSpec((r, WC), rows), pl.BlockSpec((r, WC), rows), pl.BlockSpec((r, WC), rows),
                   pl.BlockSpec((r, HI * DI), rows), pl.BlockSpec((r, LANES), rows)],
        out_shape=[jax.ShapeDtypeStruct((m, WC), BF16), jax.ShapeDtypeStruct((m, WC), F32),
                   jax.ShapeDtypeStruct((m, WC), F32), jax.ShapeDtypeStruct((m, HI * DI), BF16),
                   jax.ShapeDtypeStruct((m, LANES), F32)],
        compiler_params=_cparams(("parallel",)),
    )(p, p, *tabs, bd, qn, kn)


def _attnprep_t_body(p_ref, ci_ref, cos_ref, s1_ref, s2_ref, bd_ref, qn_ref, kn_ref,
                     k_ref, v_ref, kiw_ref, kb_ref, kib_ref, qt_ref, qit_ref, wt_ref, vt_ref):
    cos, s1, s2 = cos_ref[...], s1_ref[...], s2_ref[...]
    bd = bd_ref[...]
    nch = WC // LANES
    r = p_ref.shape[0]
    for c in range(2 * nch):
        x = p_ref[:, c * LANES:(c + 1) * LANES]
        gain = (qn_ref if c < nch else kn_ref)[...]
        n = x * lax.rsqrt(_segsum(x * x, bd) * (1.0 / DH) + EPS) * gain
        rot = _rope(n, cos, s1, s2)
        if c < nch:
            qt_ref[c * LANES:(c + 1) * LANES, :] = (rot * (DH ** -0.5 * LOG2E)).T.astype(BF16)
        else:
            k_ref[:, (c - nch) * LANES:(c - nch + 1) * LANES] = rot
            kb_ref[:, (c - nch) * LANES:(c - nch + 1) * LANES] = rot.astype(BF16)
    v = p_ref[:, 2 * WC:3 * WC]
    v_ref[...] = v
    kt = vt_ref.shape[-1]
    for t in range(r // kt):
        for c in range(nch):
            vt_ref[t, c * LANES:(c + 1) * LANES, :] = v[t * kt:(t + 1) * kt, c * LANES:(c + 1) * LANES].T.astype(BF16)
    for c in range(HI * DI // LANES):
        x = ci_ref[:, c * LANES:(c + 1) * LANES]
        qit_ref[c * LANES:(c + 1) * LANES, :] = _rope(x, cos, s1, s2).T.astype(BF16)
    x = ci_ref[:, HI * DI:HI * DI + LANES]
    lane = lax.broadcasted_iota(I32, x.shape, 1)
    is_ki = lane < DI
    kiw = _rope(x, jnp.where(is_ki, cos, 1.0), jnp.where(is_ki, s1, 0.0), jnp.where(is_ki, s2, 0.0))
    kiw_ref[...] = kiw
    kib_ref[...] = kiw[:, 0:DI].astype(BF16)
    wt_ref[...] = kiw.T[LANE_W:LANE_W + SUBLANES, :]


def _attn_prep_t(p, bsz, t, r, kt, tabs, bd, qn, kn):
    nt = t // r
    m = bsz * t
    assert t % r == 0 and r % kt == 0 and kt % LANES == 0
    rows = lambda b, i: (b * nt + i, 0)
    tmap = lambda b, i: (i, 0)
    cst = lambda b, i: (0, 0)
    return pl.pallas_call(
        _attnprep_t_body,
        grid=(bsz, nt),
        in_specs=[pl.BlockSpec((r, 3 * WC), lambda b, i: (b * nt + i, OFF_CQKV // (3 * WC))),
                  pl.BlockSpec((r, 384), lambda b, i: (b * nt + i, OFF_CI // 384)),
                  pl.BlockSpec((r, LANES), tmap), pl.BlockSpec((r, LANES), tmap), pl.BlockSpec((r, LANES), tmap),
                  pl.BlockSpec((LANES, LANES), cst),
                  pl.BlockSpec((1, LANES), cst),
                  pl.BlockSpec((1, LANES), cst)],
        out_specs=[pl.BlockSpec((r, WC), rows), pl.BlockSpec((r, WC), rows), pl.BlockSpec((r, LANES), rows),
                   pl.BlockSpec((r, WC), rows), pl.BlockSpec((r, DI), rows),
                   pl.BlockSpec((None, WC, r), lambda b, i: (b, 0, i)),
                   pl.BlockSpec((None, HI * DI, r), lambda b, i: (b, 0, i)),
                   pl.BlockSpec((None, SUBLANES, r), lambda b, i: (b, 0, i)),
                   pl.BlockSpec((None, r // kt, WC, kt), lambda b, i: (b, i, 0, 0))],
        out_shape=[jax.ShapeDtypeStruct((m, WC), F32), jax.ShapeDtypeStruct((m, WC), F32),
                   jax.ShapeDtypeStruct((m, LANES), F32),
                   jax.ShapeDtypeStruct((m, WC), BF16), jax.ShapeDtypeStruct((m, DI), BF16),
                   jax.ShapeDtypeStruct((bsz, WC, t), BF16),
                   jax.ShapeDtypeStruct((bsz, HI * DI, t), BF16),
                   jax.ShapeDtypeStruct((bsz, SUBLANES, t), F32),
                   jax.ShapeDtypeStruct((bsz, t // kt, WC, kt), BF16)],
        compiler_params=_cparams(("parallel", "parallel")),
    )(p, p, *tabs, bd, qn, kn)


def _float_key(sc):
    bits = lax.bitcast_convert_type(sc, I32)
    return bits ^ ((bits >> 31) & 0x7FFFFFFF)


def _kth_largest_key(count_ge, shape, topk, bits_per_step=1):
    base = jnp.where(count_ge(jnp.zeros(shape, I32)) >= topk, 0, INT_MIN).astype(I32)

    def resolve(low, nbits, base):
        best = base
        for m in range(1, 2 ** nbits):
            cand = base + (jnp.int32(m) << low)
            best = jnp.where(count_ge(cand) >= topk, cand, best)
        return best

    nfull, rest = divmod(31, bits_per_step)
    base = lax.fori_loop(0, nfull, lambda i, b: resolve(31 - (i + 1) * bits_per_step, bits_per_step, b), base)
    return resolve(0, rest, base) if rest else base


def _select_tile(key, vstar, need, running, tri, visible):
    eq = key == vstar
    pref = _dot(eq.astype(BF16), tri)
    sel = (key > vstar) | (eq & (running + pref <= need))
    if visible is not None:
        sel = sel & visible
    return sel, running + pref[:, LANES - 1:LANES]


def _attn_prompt_body(qt_ref, qit_ref, wt_ref, k_ref, vt_ref, ki_ref, ltri_ref, o_ref,
                      key_scr, acc_scr, m_scr, l_scr, raw_scr, s_scr, p_scr, a_scr, bias_scr, *, tq, kt, topk):
    qblk = pl.program_id(1)
    nkb = (qblk * tq + tq + kt - 1) // kt
    qpos = qblk * tq + lax.broadcasted_iota(I32, (1, tq), 1)
    krow = lax.broadcasted_iota(I32, (kt, 1), 0)
    w = wt_ref[...] * ((HI * DI) ** -0.5)
    qit = qit_ref[...]
    qi_cat = jnp.concatenate([qit[h * DI:(h + 1) * DI, :] for h in range(HI)], axis=1)

    def key_tile(j):
        return pl.ds(pl.multiple_of(jnp.minimum(j, nkb - 1) * kt, kt), kt)

    def idx_dot(j):
        return _dot(ki_ref[key_tile(j), :], qi_cat)

    npairs_kt = (nkb + 1) // 2

    def fill_tile(j, cur):
        raw_scr[1 - cur] = idx_dot(j + 1)
        sc = None
        for h in range(HI):
            term = w[h:h + 1, :] * jnp.maximum(raw_scr[cur, :, h * tq:(h + 1) * tq], 0.0)
            sc = term if sc is None else sc + term
        key_scr[j] = jnp.where((krow + j * kt) <= qpos, _float_key(sc), INT_MIN)

    def fill(i, carry):
        fill_tile(2 * i, 0)
        fill_tile(2 * i + 1, 1)
        return carry

    raw_scr[0] = idx_dot(0)
    lax.fori_loop(0, npairs_kt, fill, 0)

    def count_ge(cand):
        def body(i, acc):
            for j in (2 * i, 2 * i + 1):
                ind = jnp.where(key_scr[j] >= cand, 1.0, 0.0)
                acc = acc + jnp.sum(ind.reshape(kt // COUNT_ROWS, COUNT_ROWS, tq), axis=0)
            return acc
        acc = lax.fori_loop(0, npairs_kt, body, jnp.zeros((COUNT_ROWS, tq), F32))
        return jnp.sum(acc, axis=0, keepdims=True)

    npair = HC // 2
    half = lax.broadcasted_iota(I32, (LANES, tq), 0) // DH
    q_pairs = []
    for c in range(npair):
        qc = qt_ref[c * LANES:(c + 1) * LANES, :].astype(F32)
        q_pairs.append(jnp.concatenate([jnp.where(half == 0, qc, 0.0), jnp.where(half == 1, qc, 0.0)],
                                       axis=1).astype(BF16))

    def store_scores(j, slot):
        kb = k_ref[key_tile(j), :]
        for c in range(npair):
            s_scr[slot, c] = _dot(kb[:, c * LANES:(c + 1) * LANES], q_pairs[c])

    vstar = _kth_largest_key(count_ge, (1, tq), topk)
    need = topk - count_ge(vstar + 1)

    store_scores(0, 0)
    m_scr[...] = jnp.full_like(m_scr, NEG_BIG)
    l_scr[...] = jnp.zeros_like(l_scr)
    acc_scr[...] = jnp.zeros_like(acc_scr)
    ltri = ltri_ref[...]

    def store_bias(j, slot, running):
        key = key_scr[jnp.minimum(j, nkb - 1)]
        eq = key == vstar
        pref = _dot(ltri, eq.astype(BF16))
        sel = ((key > vstar) | (eq & (running + pref <= need))) & ((krow + j * kt) <= qpos)
        bias_scr[slot] = jnp.where(sel, 0.0, -jnp.inf)
        return running + pref[kt - 1:kt, :]

    def apply_pv(j, slot):
        vtb = vt_ref[jnp.clip(j, 0, nkb - 1)]
        for c in range(npair):
            pv = _dot(vtb[c * LANES:(c + 1) * LANES, :], p_scr[slot, c])
            for hh in range(2):
                h = 2 * c + hh
                acc_scr[h * DH:(h + 1) * DH, :] = (a_scr[slot, h:h + 1, :] * acc_scr[h * DH:(h + 1) * DH, :]
                                                   + pv[hh * DH:(hh + 1) * DH, hh * tq:(hh + 1) * tq])

    def sweep_tile(j, cur, running):
        nxt = 1 - cur
        apply_pv(j - 1, nxt)
        store_scores(j + 1, nxt)
        running = store_bias(j + 1, nxt, running)
        bias = bias_scr[cur]
        for c in range(npair):
            pc = []
            for hh in range(2):
                h = 2 * c + hh
                s = s_scr[cur, c, :, hh * tq:(hh + 1) * tq] + bias
                m_old = m_scr[h:h + 1, :]
                m_new = jnp.maximum(m_old, jnp.max(s, axis=0, keepdims=True))
                p = jnp.exp2(s - m_new)
                alpha = jnp.exp2(m_old - m_new)
                l_scr[h:h + 1, :] = alpha * l_scr[h:h + 1, :] + jnp.sum(p, axis=0, keepdims=True)
                m_scr[h:h + 1, :] = m_new
                a_scr[cur, h:h + 1, :] = alpha
                pc.append(p.astype(BF16))
            p_scr[cur, c] = jnp.concatenate(pc, axis=1)
        return running

    def sweep(i, running):
        return sweep_tile(2 * i + 1, 1, sweep_tile(2 * i, 0, running))

    p_scr[1] = jnp.zeros_like(p_scr[1])
    a_scr[1] = jnp.ones_like(a_scr[1])
    lax.fori_loop(0, npairs_kt, sweep, store_bias(0, 0, jnp.zeros((1, tq), F32)))
    apply_pv(2 * npairs_kt - 1, 1)
    for c in range(npair):
        ot = jnp.concatenate([acc_scr[h * DH:(h + 1) * DH, :] / l_scr[h:h + 1, :] for h in (2 * c, 2 * c + 1)],
                             axis=0)
        o_ref[:, c * LANES:(c + 1) * LANES] = ot.T


def _attn_prompt(qt, qit, wt, kbf, vt, kibf, ltri, bsz, s_len):
    tq = LANES
    kt = vt.shape[-1]
    nq = s_len // tq
    topk = min(TOPK_MAX, s_len // 4)
    assert s_len % kt == 0 and kt % tq == 0
    k3 = kbf.reshape(bsz, s_len, WC)
    ki3 = kibf.reshape(bsz, s_len, DI)
    return pl.pallas_call(
        functools.partial(_attn_prompt_body, tq=tq, kt=kt, topk=topk),
        grid=(bsz, nq),
        in_specs=[pl.BlockSpec((None, WC, tq), lambda b, i: (b, 0, i)),
                  pl.BlockSpec((None, HI * DI, tq), lambda b, i: (b, 0, i)),
                  pl.BlockSpec((None, SUBLANES, tq), lambda b, i: (b, 0, i)),
                  pl.BlockSpec((None, s_len, WC), lambda b, i: (b, 0, 0)),
                  pl.BlockSpec((None, s_len // kt, WC, kt), lambda b, i: (b, 0, 0, 0)),
                  pl.BlockSpec((None, s_len, DI), lambda b, i: (b, 0, 0)),
                  pl.BlockSpec((kt, kt), lambda b, i: (0, 0))],
        out_specs=pl.BlockSpec((tq, WC), lambda b, i: (b * nq + i, 0)),
        out_shape=jax.ShapeDtypeStruct((bsz * s_len, WC), F32),
        scratch_shapes=[pltpu.VMEM((s_len // kt, kt, tq), I32),
                        pltpu.VMEM((WC, tq), F32),
                        pltpu.VMEM((SUBLANES, tq), F32),
                        pltpu.VMEM((SUBLANES, tq), F32),
                        pltpu.VMEM((2, kt, HI * tq), F32),
                        pltpu.VMEM((2, HC // 2, kt, 2 * tq), F32),
                        pltpu.VMEM((2, HC // 2, kt, 2 * tq), BF16),
                        pltpu.VMEM((2, SUBLANES, tq), F32),
                        pltpu.VMEM((2, kt, tq), F32)],
        compiler_params=_cparams(("parallel", "arbitrary")),
    )(qt, qit, wt, k3, vt, ki3, ltri)


def _attn_sample_body(pt_ref, q_ref, qi_ref, kiw_ref, knew_ref, vnew_ref, tri_ref, *refs, t, npg, topk):
    kidx_refs, k_refs, v_refs = refs[0:npg], refs[npg:2 * npg], refs[2 * npg:3 * npg]
    o_ref, s_scr, kipad_scr, kpad_scr, vpad_scr = refs[3 * npg:]
    nt = npg + 1
    qi = qi_ref[...]
    qi_stack = jnp.concatenate([qi[:, h * DI:(h + 1) * DI] for h in range(HI)], axis=0)
    w = kiw_ref[:, LANE_W:LANE_W + HI] * ((HI * DI) ** -0.5)
    w_stack = jnp.concatenate([w[:, h:h + 1] for h in range(HI)], axis=0)

    def score_keys(kib):
        rel = jnp.maximum(_dot_nt(qi_stack, kib), 0.0) * w_stack
        sc = rel[0:t]
        for h in range(1, HI):
            sc = sc + rel[h * t:(h + 1) * t]
        return _float_key(sc)

    keys = [score_keys(kidx_refs[j][...].astype(BF16)) for j in range(npg)]
    kipad_scr[...] = jnp.zeros_like(kipad_scr)
    kipad_scr[0:t, :] = kiw_ref[:, 0:DI]
    vis_new = lax.broadcasted_iota(I32, (t, LANES), 1) <= lax.broadcasted_iota(I32, (t, LANES), 0)
    keys.append(jnp.where(vis_new, score_keys(kipad_scr[...].astype(BF16)), INT_MIN))

    def count_ge(cand):
        terms = [jnp.where(key >= cand, 1.0, 0.0) for key in keys]
        while len(terms) > 1:
            terms = [a + b for a, b in zip(terms[0::2], terms[1::2])] + terms[len(terms) & ~1:]
        return jnp.sum(terms[0], axis=1, keepdims=True)

    vstar = _kth_largest_key(count_ge, (t, 1), topk, bits_per_step=3)
    need = topk - count_ge(vstar + 1)

    kpad_scr[...] = jnp.zeros_like(kpad_scr)
    vpad_scr[...] = jnp.zeros_like(vpad_scr)
    kpad_scr[0:t, :] = knew_ref[...]
    vpad_scr[0:t, :] = vnew_ref[...]
    head_of_lane = lax.broadcasted_iota(I32, (t, WC), 1) // DH
    q = q_ref[...].astype(F32)
    q_stack = jnp.concatenate([jnp.where(head_of_lane == h, q, 0.0) for h in range(HC)], axis=0).astype(BF16)
    tri = tri_ref[...]
    running = jnp.zeros((t, 1), F32)
    m_acc = jnp.full((HC * t, LANES), NEG_BIG, F32)
    for j in range(nt):
        sel, running = _select_tile(keys[j], vstar, need, running, tri, vis_new if j == npg else None)
        bias = jnp.where(sel, 0.0, -jnp.inf)
        kb = (k_refs[j] if j < npg else kpad_scr)[...].astype(BF16)
        s = _dot_nt(q_stack, kb) + jnp.concatenate([bias] * HC, axis=0)
        s_scr[j] = s
        m_acc = jnp.maximum(m_acc, s)
    m = jnp.max(m_acc, axis=1, keepdims=True)
    l_acc = jnp.zeros((HC * t, LANES), F32)
    acc = jnp.zeros((HC * t, WC), F32)
    for j in range(nt):
        p = jnp.exp(s_scr[j] - m)
        l_acc = l_acc + p
        vb = (v_refs[j] if j < npg else vpad_scr)[...].astype(BF16)
        acc = acc + _dot(p.astype(BF16), vb)
    o = acc / jnp.sum(l_acc, axis=1, keepdims=True)
    out = jnp.where(head_of_lane == 0, o[0:t], 0.0)
    for h in range(1, HC):
        out = out + jnp.where(head_of_lane == h, o[h * t:(h + 1) * t], 0.0)
    o_ref[...] = out


def _attn_sample(page_table, q, qi, kiw, knew, vnew, tri, cache_kidx, cache_k, cache_v, layer, bsz, t):
    npg = page_table.shape[1]
    page = cache_k.shape[2]
    assert page == LANES
    topk = min(TOPK_MAX, (npg * page + t) // 4)
    row = lambda b, pt: (b, 0)

    def page_map(j):
        return lambda b, pt: (layer, pt[b, j], 0, 0)

    in_specs = [pl.BlockSpec((t, WC), row), pl.BlockSpec((t, HI * DI), row), pl.BlockSpec((t, LANES), row),
                pl.BlockSpec((t, WC), row), pl.BlockSpec((t, WC), row),
                pl.BlockSpec((LANES, LANES), lambda b, pt: (0, 0))]
    in_specs += [pl.BlockSpec((None, None, page, DI), page_map(j)) for j in range(npg)]
    in_specs += [pl.BlockSpec((None, None, page, WC), page_map(j)) for j in range(npg)]
    in_specs += [pl.BlockSpec((None, None, page, WC), page_map(j)) for j in range(npg)]
    grid_spec = pltpu.PrefetchScalarGridSpec(
        num_scalar_prefetch=1,
        grid=(bsz,),
        in_specs=in_specs,
        out_specs=pl.BlockSpec((t, WC), row),
        scratch_shapes=[pltpu.VMEM((npg + 1, HC * t, LANES), F32), pltpu.VMEM((LANES, DI), F32),
                        pltpu.VMEM((LANES, WC), F32), pltpu.VMEM((LANES, WC), F32)])
    return pl.pallas_call(
        functools.partial(_attn_sample_body, t=t, npg=npg, topk=topk),
        grid_spec=grid_spec,
        out_shape=jax.ShapeDtypeStruct((bsz * t, WC), F32),
        compiler_params=_cparams(("parallel",)),
    )(page_table, q, qi, kiw, knew, vnew, tri,
      *([cache_kidx] * npg), *([cache_k] * npg), *([cache_v] * npg))


def _rope_tables(pos):
    half = ROPE_DIM // 2
    inv_freq = ROPE_THETA ** (-jnp.arange(half, dtype=F32) / half)
    ang = pos.astype(F32)[:, None] * inv_freq[None, :]
    cos, sin = jnp.cos(ang), jnp.sin(ang)
    n = pos.shape[0]
    one = jnp.ones((n, DH - ROPE_DIM), F32)
    zero_h = jnp.zeros((n, half), F32)
    zero_r = jnp.zeros((n, DH - ROPE_DIM), F32)
    c64 = jnp.concatenate([cos, cos, one], axis=1)
    s1 = jnp.concatenate([-sin, zero_h, zero_r], axis=1)
    s2 = jnp.concatenate([zero_h, sin, zero_r], axis=1)
    rep = LANES // DH
    return tuple(jnp.tile(a, (1, rep)) for a in (c64, s1, s2))


def _reorder_w_in(w_in):
    sizes = (WA, WA, WA, QKV_B, WB, HB, HB, 3 * WC, HI * DI, DI, HI)
    offs = [0]
    for s in sizes:
        offs.append(offs[-1] + s)
    seg = lambda i: w_in[:, :, offs[i]:offs[i + 1]]
    pad = jnp.zeros(w_in.shape[:2] + (LANES - DI - HI - 2 * HB,), w_in.dtype)
    order = [seg(3), seg(7), seg(0), seg(1), seg(2), seg(4), seg(8), seg(9), seg(10), seg(5), seg(6), pad]
    return jnp.concatenate(order, axis=-1).astype(BF16)


def _pad_state(prev):
    return jnp.pad(prev, ((0, 0), (SUBLANES - prev.shape[1], 0), (0, 0)))


def kernel(x_prompt, x_sample, state_conv_a, state_gdn_conv, state_gdn, cache_k, cache_v, cache_kidx, page_table, ffn1_norm, ffn1_w_gate, ffn1_w_up, ffn1_w_down, mix_norm, w_in, conv_a_w, gdn_conv_w, gdn_a_log, gdn_dt_bias, gdn_out_norm, attn_q_norm, attn_k_norm, w_out, ffn2_norm, ffn2_w_gate, ffn2_w_up, ffn2_w_down):
    depth = w_in.shape[0]
    bp, tp, d = x_prompt.shape
    bs, ts, _ = x_sample.shape
    mp, ms = bp * tp, bs * ts
    npool, page = cache_k.shape[1], cache_k.shape[2]
    past = page_table.shape[1] * page
    assert ts == SUBLANES and tp % LANES == 0 and mp % 256 == 0

    x = jnp.concatenate([x_prompt.reshape(mp, d), x_sample.reshape(ms, d)], axis=0)
    w_in_r = _reorder_w_in(w_in)
    bf = lambda a: a.astype(BF16)
    f1g, f1u, f1d = bf(ffn1_w_gate), bf(ffn1_w_up), bf(ffn1_w_down)
    f2g, f2u, f2d = bf(ffn2_w_gate), bf(ffn2_w_up), bf(ffn2_w_down)
    w_out_b = bf(w_out)
    ck = cache_k.reshape(depth, npool, page, WC)
    cv = cache_v.reshape(depth, npool, page, WC)

    li = jnp.arange(LANES)
    bd = ((li[:, None] // DH) == (li[None, :] // DH)).astype(BF16)
    tri = (li[:, None] <= li[None, :]).astype(BF16)
    kt = r_p = _pick(tp, (256, 128))
    lk = jnp.arange(kt)
    ltri = (lk[None, :] <= lk[:, None]).astype(BF16)
    r_s = _pick(ms, (128, 64, 32, 16, 8))
    tabs_p = _rope_tables(jnp.arange(tp))
    tabs_s = _rope_tables(jnp.tile(past + jnp.arange(ts), r_s // ts))
    zero_a = jnp.zeros((bp, SUBLANES, WA), F32)
    zero_b = jnp.zeros((bp, SUBLANES, QKV_B), F32)
    zero_s = jnp.zeros((bp, HB, DKB, DVB), F32)
    lane_pad = lambda v, off: jnp.zeros((1, LANES), F32).at[0, off:off + v.shape[0]].set(v)

    outs = [[] for _ in range(12)]
    for l in range(depth):
        x = _ffn(x, ffn1_norm[l][None], f1g[l], f1u[l], f1d[l])
        p = _inproj(x, mix_norm[l][None], w_in_r[l])
        avec = lane_pad(jnp.exp(gdn_a_log[l].astype(F32)), LANE_GA)
        dtb = lane_pad(gdn_dt_bias[l].astype(F32), LANE_GA)
        gn = gdn_out_norm[l][None].astype(F32)
        qn = jnp.tile(attn_q_norm[l].astype(F32), LANES // DH)[None]
        kn = jnp.tile(attn_k_norm[l].astype(F32), LANES // DH)[None]

        ya, yb, yc = [], [], []
        groups = ((0, bp, tp, zero_a, zero_b, zero_s), (mp, bs, ts, _pad_state(state_conv_a[l]),
                                                       _pad_state(state_gdn_conv[l]), state_gdn[l]))
        for gi, (row0, bsz, t, prev_a, prev_b, s0) in enumerate(groups):
            m = bsz * t
            y_a, st_a = _mixer_a(p, row0, bsz, t, conv_a_w[l], prev_a)
            qg, kg, vg, gbg, st_b = _gdn_prep(p, row0, bsz, t, gdn_conv_w[l], prev_b, bd, avec, dtb)
            y_b, s_new = _gdn(qg, kg, vg, gbg, p, row0, bsz, t, s0, gn)
            if gi == 0:
                kc, vc, kiw, kbf, kibf, qt, qit, wt, vt = _attn_prep_t(p, bsz, t, r_p, kt, tabs_p, bd, qn, kn)
                y_c = _attn_prompt(qt, qit, wt, kbf, vt, kibf, ltri, bsz, t)
            else:
                qc, kc, vc, qi, kiw = _attn_prep(p, row0, m, r_s, tabs_s, bd, qn, kn)
                y_c = _attn_sample(page_table, qc, qi, kiw, kc, vc, tri, cache_kidx, ck, cv, l, bsz, t)
            ya.append(y_a)
            yb.append(y_b)
            yc.append(y_c)
            res = (st_a[:, SUBLANES - (CONV_A - 1):], st_b[:, SUBLANES - (GDN_CONV - 1):], s_new,
                   kc.reshape(bsz, t, HC, DH), vc.reshape(bsz, t, HC, DH), kiw[:, :DI].reshape(bsz, t, DI))
            for i, r in enumerate(res):
                outs[gi * 6 + i].append(r)
        x = _outproj(x, jnp.concatenate(ya), jnp.concatenate(yb), jnp.concatenate(yc), w_out_b[l])
        x = _ffn(x, ffn2_norm[l][None], f2g[l], f2u[l], f2d[l])

    y_prompt = x[:mp].reshape(bp, tp, d)
    y_sample = x[mp:].reshape(bs, ts, d)
    return (y_prompt, y_sample) + tuple(jnp.stack(o) for o in outs)
```

```python
import functools
import math

import jax
import jax.numpy as jnp
from jax import lax
from jax.experimental import pallas as pl
from jax.experimental.pallas import tpu as pltpu

F32 = jnp.float32
BF16 = jnp.bfloat16
I32 = jnp.int32

EPS = 1e-6
LANES = 128
SUBLANES = 8
VMEM_LIMIT = 56 * 1024 * 1024

A_GROUPS = 4
CONV_A = 3
HB = 6
DKB = 64
DVB = 64
GDN_CONV = 4
GDN_ROWS = 64
HC = 6
DH = 64
HI = 4
DI = 64
TOPK_MAX = 256
ROPE_DIM = DH // 4
ROPE_THETA = 500000.0
INT_MIN = -2 ** 31
COUNT_ROWS = 4 * SUBLANES
NEG_BIG = -1e30
LOG2E = math.log2(math.e)

WA = 256
QKV_B = 2 * HB * DKB + HB * DVB
WB = HB * DVB
WC = HC * DH
OFF_BQKV = 0
OFF_CQKV = QKV_B
OFF_A = 2 * QKV_B
OFF_BZ = OFF_A + 3 * WA
OFF_CI = OFF_BZ + WB
N_P = OFF_CI + 384
LANE_W = DI
LANE_GA = DI + HI
LANE_GB = DI + HI + HB


def _pick(n, cands):
    for c in cands:
        if n % c == 0:
            return c
    raise ValueError(f"no tile for {n}")


def _cparams(sem):
    return pltpu.CompilerParams(dimension_semantics=sem, vmem_limit_bytes=VMEM_LIMIT)


def _dot(a, b):
    return jnp.dot(a, b, preferred_element_type=F32)


def _dot_nt(a, b):
    return lax.dot_general(a, b, (((1,), (1,)), ((), ())), preferred_element_type=F32)


def _split3(x):
    hi = x.astype(BF16)
    r1 = x - hi.astype(F32)
    mid = r1.astype(BF16)
    return hi, mid, (r1 - mid.astype(F32)).astype(BF16)


def _dot01(sel01, x):
    s = sel01.astype(BF16)
    hi, mid, lo = _split3(x)
    return _dot(s, hi) + _dot(s, mid) + _dot(s, lo)


def _dot01_nt(sel01, x):
    s = sel01.astype(BF16)
    hi, mid, lo = _split3(x)
    return _dot_nt(s, hi) + _dot_nt(s, mid) + _dot_nt(s, lo)


def _dot3(a, b):
    ah = a.astype(BF16)
    al = (a - ah.astype(F32)).astype(BF16)
    bh = b.astype(BF16)
    bl = (b - bh.astype(F32)).astype(BF16)
    return _dot(ah, bh) + _dot(ah, bl) + _dot(al, bh)


def _segsum(x, bd):
    hi = x.astype(BF16)
    lo = (x - hi.astype(F32)).astype(BF16)
    return _dot(hi, bd) + _dot(lo, bd)


def _sigmoid(x):
    return 1.0 / (1.0 + jnp.exp(-x))


def _silu(x):
    return x * _sigmoid(x)


def _softplus(x):
    return jnp.maximum(x, 0.0) + jnp.log(1.0 + jnp.exp(-jnp.abs(x)))


def _ffn_body(x_ref, g_ref, wg_ref, wu_ref, wd_ref, o_ref, h_ref, acc_ref):
    j = pl.program_id(1)

    @pl.when(j == 0)
    def _():
        x = x_ref[...]
        ms = jnp.mean(x * x, axis=-1, keepdims=True)
        h_ref[...] = (x * lax.rsqrt(ms + EPS) * g_ref[...]).astype(BF16)
        acc_ref[...] = jnp.zeros_like(acc_ref)

    h = h_ref[...]
    a = _dot(h, wg_ref[...])
    b = _dot(h, wu_ref[...])
    t = (_silu(a) * b).astype(BF16)
    acc_ref[...] += _dot(t, wd_ref[...])

    @pl.when(j == pl.num_programs(1) - 1)
    def _():
        o_ref[...] = x_ref[...] + 0.5 * acc_ref[...]


def _ffn(x, g, wg, wu, wd):
    m, d = x.shape
    f = wg.shape[1]
    tm = _pick(m, (512, 256, 128, 64, 32, 16, 8))
    tf = _pick(f, (1408, 256, 128))
    return pl.pallas_call(
        _ffn_body,
        grid=(m // tm, f // tf),
        in_specs=[pl.BlockSpec((tm, d), lambda i, j: (i, 0)),
                  pl.BlockSpec((1, d), lambda i, j: (0, 0)),
                  pl.BlockSpec((d, tf), lambda i, j: (0, j)),
                  pl.BlockSpec((d, tf), lambda i, j: (0, j)),
                  pl.BlockSpec((tf, d), lambda i, j: (j, 0))],
        out_specs=pl.BlockSpec((tm, d), lambda i, j: (i, 0)),
        out_shape=jax.ShapeDtypeStruct((m, d), F32),
        scratch_shapes=[pltpu.VMEM((tm, d), BF16), pltpu.VMEM((tm, d), F32)],
        compiler_params=_cparams(("parallel", "arbitrary")),
    )(x, g, wg, wu, wd)


def _inproj_body(x_ref, g_ref, w_ref, o_ref, h_ref):
    @pl.when(pl.program_id(1) == 0)
    def _():
        x = x_ref[...]
        ms = jnp.mean(x * x, axis=-1, keepdims=True)
        h_ref[...] = (x * lax.rsqrt(ms + EPS) * g_ref[...]).astype(BF16)

    o_ref[...] = _dot(h_ref[...], w_ref[...])


def _inproj(x, g, w):
    m, d = x.shape
    n = w.shape[1]
    tm = _pick(m, (512, 256, 128, 64, 32, 16, 8))
    tn = _pick(n, (3840, 1920, 768, 384, 128))
    return pl.pallas_call(
        _inproj_body,
        grid=(m // tm, n // tn),
        in_specs=[pl.BlockSpec((tm, d), lambda i, j: (i, 0)),
                  pl.BlockSpec((1, d), lambda i, j: (0, 0)),
                  pl.BlockSpec((d, tn), lambda i, j: (0, j))],
        out_specs=pl.BlockSpec((tm, tn), lambda i, j: (i, j)),
        out_shape=jax.ShapeDtypeStruct((m, n), F32),
        scratch_shapes=[pltpu.VMEM((tm, d), BF16)],
        compiler_params=_cparams(("parallel", "arbitrary")),
    )(x, g, w)


def _outproj_body(x_ref, ya_ref, yb_ref, yc_ref, w_ref, o_ref):
    acc = _dot(ya_ref[...].astype(BF16), w_ref[0:WA, :])
    acc += _dot(yb_ref[...].astype(BF16), w_ref[WA:WA + WB, :])
    acc += _dot(yc_ref[...].astype(BF16), w_ref[WA + WB:WA + WB + WC, :])
    o_ref[...] = x_ref[...] + acc


def _outproj(x, ya, yb, yc, w):
    m, d = x.shape
    tm = _pick(m, (512, 256, 128, 64, 32, 16, 8))
    return pl.pallas_call(
        _outproj_body,
        grid=(m // tm,),
        in_specs=[pl.BlockSpec((tm, d), lambda i: (i, 0)),
                  pl.BlockSpec((tm, WA), lambda i: (i, 0)),
                  pl.BlockSpec((tm, WB), lambda i: (i, 0)),
                  pl.BlockSpec((tm, WC), lambda i: (i, 0)),
                  pl.BlockSpec(w.shape, lambda i: (0, 0))],
        out_specs=pl.BlockSpec((tm, d), lambda i: (i, 0)),
        out_shape=jax.ShapeDtypeStruct((m, d), F32),
        compiler_params=_cparams(("parallel",)),
    )(x, ya, yb, yc, w)


def _conv_taps(u, w_ref, prev_ref, ext_ref, tt, width):
    it = pl.program_id(1)

    @pl.when(it == 0)
    def _():
        ext_ref[:, 0:SUBLANES, :] = prev_ref[...]

    @pl.when(it > 0)
    def _():
        ext_ref[:, 0:SUBLANES, :] = ext_ref[:, tt:tt + SUBLANES, :]

    ext_ref[:, SUBLANES:SUBLANES + tt, :] = u
    acc = None
    for s in range(width):
        term = ext_ref[:, SUBLANES - s:SUBLANES - s + tt, :] * w_ref[width - 1 - s:width - s, :]
        acc = term if acc is None else acc + term
    return acc, ext_ref[:, tt:tt + SUBLANES, :]


def _group_tiles(bsz, t):
    if t >= 128:
        return 1, _pick(t, (256, 128))
    return _pick(bsz, (16, 8, 4, 2, 1)), t


def _mixa_body(p_ref, w_ref, prev_ref, y_ref, st_ref, ext_ref, *, bb, tt):
    pa = p_ref[...]
    u = (pa[:, WA:2 * WA] * pa[:, 2 * WA:3 * WA]).reshape(bb, tt, WA)
    conv, tail = _conv_taps(u, w_ref, prev_ref, ext_ref, tt, CONV_A)
    y_ref[...] = pa[:, 0:WA] * conv.reshape(bb * tt, WA)
    st_ref[...] = tail


def _mixer_a(p, row0, bsz, t, w, prev8):
    bb, tt = _group_tiles(bsz, t)
    r = bb * tt
    nb, nt = bsz // bb, t // tt
    base = row0 // r
    assert row0 % r == 0
    return pl.pallas_call(
        functools.partial(_mixa_body, bb=bb, tt=tt),
        grid=(nb, nt),
        in_specs=[pl.BlockSpec((r, 3 * WA), lambda i, j: (base + i * nt + j, OFF_A // (3 * WA))),
                  pl.BlockSpec(w.shape, lambda i, j: (0, 0)),
                  pl.BlockSpec((bb, SUBLANES, WA), lambda i, j: (i, 0, 0))],
        out_specs=[pl.BlockSpec((r, WA), lambda i, j: (i * nt + j, 0)),
                   pl.BlockSpec((bb, SUBLANES, WA), lambda i, j: (i, 0, 0))],
        out_shape=[jax.ShapeDtypeStruct((bsz * t, WA), F32),
                   jax.ShapeDtypeStruct((bsz, SUBLANES, WA), F32)],
        scratch_shapes=[pltpu.VMEM((bb, tt + SUBLANES, WA), F32)],
        compiler_params=_cparams(("parallel", "arbitrary")),
    )(p, w, prev8)


def _gdnprep_body(p_ref, ci_ref, w_ref, prev_ref, bd_ref, avec_ref, dtb_ref,
                  q_ref, k_ref, v_ref, gb_ref, st_ref, ext_ref, *, bb, tt):
    r = bb * tt
    x = p_ref[...].reshape(bb, tt, QKV_B)
    conv, tail = _conv_taps(x, w_ref, prev_ref, ext_ref, tt, GDN_CONV)
    st_ref[...] = tail
    a = _silu(conv.reshape(r, QKV_B))
    bd = bd_ref[...]
    nq = HB * DKB // LANES
    for c in range(2 * nq):
        xc = a[:, c * LANES:(c + 1) * LANES]
        n = xc * lax.rsqrt(_segsum(xc * xc, bd) + EPS)
        if c < nq:
            q_ref[:, c * LANES:(c + 1) * LANES] = n * (DKB ** -0.5)
        else:
            k_ref[:, (c - nq) * LANES:(c - nq + 1) * LANES] = n
    v_ref[...] = a[:, 2 * HB * DKB:]
    raw = ci_ref[...]
    g = -avec_ref[...] * _softplus(raw + dtb_ref[...])
    lane = lax.broadcasted_iota(I32, raw.shape, 1)
    gb_ref[...] = jnp.where((lane >= LANE_GB) & (lane < LANE_GB + HB), _sigmoid(raw), g)


def _gdn_prep(p, row0, bsz, t, w, prev8, bd, avec, dtb):
    bb, tt = _group_tiles(bsz, t)
    r = bb * tt
    nb, nt = bsz // bb, t // tt
    base = row0 // r
    assert row0 % r == 0
    m = bsz * t
    rows = lambda i, j: (i * nt + j, 0)
    return pl.pallas_call(
        functools.partial(_gdnprep_body, bb=bb, tt=tt),
        grid=(nb, nt),
        in_specs=[pl.BlockSpec((r, QKV_B), lambda i, j: (base + i * nt + j, OFF_BQKV // QKV_B)),
                  pl.BlockSpec((r, LANES), lambda i, j: (base + i * nt + j, (OFF_CI + 2 * LANES) // LANES)),
                  pl.BlockSpec(w.shape, lambda i, j: (0, 0)),
                  pl.BlockSpec((bb, SUBLANES, QKV_B), lambda i, j: (i, 0, 0)),
                  pl.BlockSpec((LANES, LANES), lambda i, j: (0, 0)),
                  pl.BlockSpec((1, LANES), lambda i, j: (0, 0)),
                  pl.BlockSpec((1, LANES), lambda i, j: (0, 0))],
        out_specs=[pl.BlockSpec((r, HB * DKB), rows),
                   pl.BlockSpec((r, HB * DKB), rows),
                   pl.BlockSpec((r, WB), rows),
                   pl.BlockSpec((r, LANES), rows),
                   pl.BlockSpec((bb, SUBLANES, QKV_B), lambda i, j: (i, 0, 0))],
        out_shape=[jax.ShapeDtypeStruct((m, HB * DKB), F32),
                   jax.ShapeDtypeStruct((m, HB * DKB), F32),
                   jax.ShapeDtypeStruct((m, WB), F32),
                   jax.ShapeDtypeStruct((m, LANES), F32),
                   jax.ShapeDtypeStruct((bsz, SUBLANES, QKV_B), F32)],
        scratch_shapes=[pltpu.VMEM((bb, tt + SUBLANES, QKV_B), F32)],
        compiler_params=_cparams(("parallel", "arbitrary")),
    )(p, p, w, prev8, bd, avec, dtb)


def _gdn_body(q_ref, k_ref, v_ref, gb_ref, z_ref, s0_ref, gn_ref, y_ref, sout_ref, s_scr, *, c, g, nch):
    r = 2 * GDN_ROWS
    n = 2 * g
    it = pl.program_id(1)

    @pl.when(it == 0)
    def _():
        s_scr[...] = s0_ref[...]

    ri = lax.broadcasted_iota(I32, (r, r), 0)
    ci = lax.broadcasted_iota(I32, (r, r), 1)
    same = (ri // c) == (ci // c)
    incl = same & (ci <= ri)
    strict = same & (ci < ri)
    eye = (ri == ci).astype(F32)
    lastsel = (ci == (ri // c) * c + (c - 1)).astype(F32)
    chunks = range(nch)
    rows_of = lambda ch: slice(ch * GDN_ROWS, (ch + 1) * GDN_ROWS)
    gb = [jnp.concatenate([gb_ref[rows_of(ch), :]] * 2, axis=0) for ch in chunks]
    gcum = [_dot01(incl, gb[ch]) for ch in chunks]
    glast = [_dot01(lastsel, gcum[ch]) for ch in chunks]
    lane = lax.broadcasted_iota(I32, (r, LANES), 1)
    lane0 = (lane == 0).astype(F32)
    upper = lax.broadcasted_iota(I32, (r, 1), 0) < GDN_ROWS
    er = lax.broadcasted_iota(I32, (r, n * DKB), 0)
    ec = lax.broadcasted_iota(I32, (r, n * DKB), 1)
    emask = (er // c) == (ec // DKB)
    tr = lax.broadcasted_iota(I32, (DKB, n * DKB), 0)
    tc = lax.broadcasted_iota(I32, (DKB, n * DKB), 1)
    tile_mat = ((tc % DKB) == tr).astype(BF16)
    xr = lax.broadcasted_iota(I32, (n * DKB, r), 0)
    xc = lax.broadcasted_iota(I32, (n * DKB, r), 1)
    emask_t = (xr // DKB) == (xc // c)
    sr = lax.broadcasted_iota(I32, (n * DKB, DKB), 0)
    sc = lax.broadcasted_iota(I32, (n * DKB, DKB), 1)
    tile_mat_t = ((sr % DKB) == sc).astype(BF16)
    esel = (xc == (xr // DKB) * c + (c - 1)).astype(F32)
    gend = [_dot01(esel, gcum[ch]) for ch in chunks]
    s_upper = lax.broadcasted_iota(I32, (n * DKB, 1), 0) < g * DKB
    n_sq = max(0, int(math.ceil(math.log2(c))) - 1)

    npair = HB // 2
    items = [(ch, i) for ch in chunks for i in range(npair)]
    each = lambda f: [f(n_, ch, i) for n_, (ch, i) in enumerate(items)]

    def stack(ref, width):
        return each(lambda n_, ch, i: jnp.concatenate([ref[rows_of(ch), 2 * i * width:(2 * i + 1) * width],
                                                       ref[rows_of(ch), (2 * i + 1) * width:(2 * i + 2) * width]], axis=0))

    def pick(mats, rows_upper, base):
        return each(lambda n_, ch, i: jnp.where(rows_upper, mats[ch][:, base + 2 * i:base + 2 * i + 1],
                                                mats[ch][:, base + 2 * i + 1:base + 2 * i + 2]))

    q, k, v, zz = stack(q_ref, DKB), stack(k_ref, DKB), stack(v_ref, DVB), stack(z_ref, DVB)
    gcol = pick(gcum, upper, LANE_GA)
    glc = pick(glast, upper, LANE_GA)
    beta = pick(gb, upper, LANE_GB)
    srow = [jnp.exp(x) for x in pick(gend, s_upper, LANE_GA)]
    grow = each(lambda n_, ch, i: _dot01_nt(lane0, jnp.where(lane == 0, gcol[n_], 0.0)))
    decay = each(lambda n_, ch, i: jnp.exp(jnp.where(incl, gcol[n_] - grow[n_], -jnp.inf)))
    kb = each(lambda n_, ch, i: k[n_] * beta[n_])
    a_mat = each(lambda n_, ch, i: _dot_nt(kb[n_], k[n_]) * jnp.where(strict, decay[n_], 0.0))
    pw = [-a for a in a_mat]
    t_mat = [eye + x for x in pw]
    for _ in range(n_sq):
        pw = [_dot3(x, x) for x in pw]
        t_mat = each(lambda n_, ch, i: t_mat[n_] + _dot(t_mat[n_], pw[n_]))
    u = each(lambda n_, ch, i: _dot(t_mat[n_], v[n_] * beta[n_]))
    wk = each(lambda n_, ch, i: _dot(t_mat[n_], kb[n_] * jnp.exp(gcol[n_])))
    qk = each(lambda n_, ch, i: _dot_nt(q[n_], k[n_]) * decay[n_])
    wk_e = each(lambda n_, ch, i: jnp.where(emask, _dot(wk[n_].astype(BF16), tile_mat), 0.0))
    qg_e = each(lambda n_, ch, i: jnp.where(emask, _dot((q[n_] * jnp.exp(gcol[n_])).astype(BF16), tile_mat), 0.0))
    kd_t = each(lambda n_, ch, i: jnp.where(
        emask_t, _dot_nt(tile_mat_t, (k[n_] * jnp.exp(glc[n_] - gcol[n_])).astype(BF16)), 0.0))
    s = [s_scr[2 * i:2 * i + 2].reshape(n * DKB, DVB) for i in range(npair)]
    for ch in chunks:
        sel = [ch * npair + i for i in range(npair)]
        v_new = [u[n_] - _dot(wk_e[n_], s[i]) for i, n_ in enumerate(sel)]
        out = [_dot(qg_e[n_], s[i]) + _dot(qk[n_], v_new[i]) for i, n_ in enumerate(sel)]
        s = [s[i] * srow[n_] + _dot(kd_t[n_], v_new[i]) for i, n_ in enumerate(sel)]
        for i, n_ in enumerate(sel):
            ms = jnp.mean(out[i] * out[i], axis=-1, keepdims=True)
            y = out[i] * lax.rsqrt(ms + EPS) * gn_ref[...] * _silu(zz[n_])
            y_ref[rows_of(ch), 2 * i * DVB:(2 * i + 1) * DVB] = y[0:GDN_ROWS]
            y_ref[rows_of(ch), (2 * i + 1) * DVB:(2 * i + 2) * DVB] = y[GDN_ROWS:]
    for i in range(npair):
        s_scr[2 * i:2 * i + 2] = s[i].reshape(2, g * DKB, DVB)

    @pl.when(it == pl.num_programs(1) - 1)
    def _():
        sout_ref[...] = s_scr[...]


def _gdn(q, k, v, gb, p, row0, bsz, t, s0, gn):
    c = min(GDN_ROWS, t)
    g = GDN_ROWS // c
    nch = 2 if (g == 1 and (t // c) % 2 == 0) else 1
    rb = nch * GDN_ROWS
    assert t % (c * nch) == 0 and bsz % g == 0 and row0 % rb == 0
    nb, nt = bsz // g, t // (c * nch)
    base = row0 // rb
    rows = lambda i, j: (i * nt + j, 0)
    s0r = s0.reshape(nb, g, HB, DKB, DVB).transpose(0, 2, 1, 3, 4).reshape(nb, HB, g * DKB, DVB)
    y, s_out = pl.pallas_call(
        functools.partial(_gdn_body, c=c, g=g, nch=nch),
        grid=(nb, nt),
        in_specs=[pl.BlockSpec((rb, HB * DKB), rows),
                  pl.BlockSpec((rb, HB * DKB), rows),
                  pl.BlockSpec((rb, WB), rows),
                  pl.BlockSpec((rb, LANES), rows),
                  pl.BlockSpec((rb, WB), lambda i, j: (base + i * nt + j, OFF_BZ // WB)),
                  pl.BlockSpec((None, HB, g * DKB, DVB), lambda i, j: (i, 0, 0, 0)),
                  pl.BlockSpec((1, DVB), lambda i, j: (0, 0))],
        out_specs=[pl.BlockSpec((rb, WB), rows),
                   pl.BlockSpec((None, HB, g * DKB, DVB), lambda i, j: (i, 0, 0, 0))],
        out_shape=[jax.ShapeDtypeStruct((bsz * t, WB), F32),
                   jax.ShapeDtypeStruct((nb, HB, g * DKB, DVB), F32)],
        scratch_shapes=[pltpu.VMEM((HB, g * DKB, DVB), F32)],
        compiler_params=_cparams(("parallel", "arbitrary")),
    )(q, k, v, gb, p, s0r, gn)
    s_out = s_out.reshape(nb, HB, g, DKB, DVB).transpose(0, 2, 1, 3, 4).reshape(bsz, HB, DKB, DVB)
    return y, s_out


def _rope(x, cos, s1, s2):
    half = ROPE_DIM // 2
    return x * cos + pltpu.roll(x, LANES - half, 1) * s1 + pltpu.roll(x, half, 1) * s2


def _attnprep_body(p_ref, ci_ref, cos_ref, s1_ref, s2_ref, bd_ref, qn_ref, kn_ref,
                   q_ref, k_ref, v_ref, qi_ref, kiw_ref):
    cos, s1, s2 = cos_ref[...], s1_ref[...], s2_ref[...]
    bd = bd_ref[...]
    nch = WC // LANES
    for c in range(2 * nch):
        x = p_ref[:, c * LANES:(c + 1) * LANES]
        gain = (qn_ref if c < nch else kn_ref)[...]
        n = x * lax.rsqrt(_segsum(x * x, bd) * (1.0 / DH) + EPS) * gain
        rot = _rope(n, cos, s1, s2)
        if c < nch:
            q_ref[:, c * LANES:(c + 1) * LANES] = (rot * (DH ** -0.5)).astype(BF16)
        else:
            k_ref[:, (c - nch) * LANES:(c - nch + 1) * LANES] = rot
    v_ref[...] = p_ref[:, 2 * WC:3 * WC]
    for c in range(HI * DI // LANES):
        x = ci_ref[:, c * LANES:(c + 1) * LANES]
        qi_ref[:, c * LANES:(c + 1) * LANES] = _rope(x, cos, s1, s2).astype(BF16)
    x = ci_ref[:, HI * DI:HI * DI + LANES]
    lane = lax.broadcasted_iota(I32, x.shape, 1)
    is_ki = lane < DI
    kiw_ref[...] = _rope(x, jnp.where(is_ki, cos, 1.0), jnp.where(is_ki, s1, 0.0), jnp.where(is_ki, s2, 0.0))


def _attn_prep(p, row0, m, r, tabs, bd, qn, kn):
    nper = tabs[0].shape[0] // r
    tmap = lambda i: (i % nper, 0)
    assert row0 % r == 0 and m % r == 0 and tabs[0].shape[0] % r == 0
    base = row0 // r
    rows = lambda i: (i, 0)
    return pl.pallas_call(
        _attnprep_body,
        grid=(m // r,),
        in_specs=[pl.BlockSpec((r, 3 * WC), lambda i: (base + i, OFF_CQKV // (3 * WC))),
                  pl.BlockSpec((r, 384), lambda i: (base + i, OFF_CI // 384)),
                  pl.BlockSpec((r, LANES), tmap), pl.BlockSpec((r, LANES), tmap), pl.BlockSpec((r, LANES), tmap),
                  pl.BlockSpec((LANES, LANES), lambda i: (0, 0)),
                  pl.BlockSpec((1, LANES), lambda i: (0, 0)),
                  pl.BlockSpec((1, LANES), lambda i: (0, 0))],
        out_specs=[pl.BlockSpec((r, WC), rows), pl.BlockSpec((r, WC), rows), pl.BlockSpec((r, WC), rows),
                   pl.BlockSpec((r, HI * DI), rows), pl.BlockSpec((r, LANES), rows)],
        out_shape=[jax.ShapeDtypeStruct((m, WC), BF16), jax.ShapeDtypeStruct((m, WC), F32),
                   jax.ShapeDtypeStruct((m, WC), F32), jax.ShapeDtypeStruct((m, HI * DI), BF16),
                   jax.ShapeDtypeStruct((m, LANES), F32)],
        compiler_params=_cparams(("parallel",)),
    )(p, p, *tabs, bd, qn, kn)


def _attnprep_t_body(p_ref, ci_ref, cos_ref, s1_ref, s2_ref, bd_ref, qn_ref, kn_ref,
                     k_ref, v_ref, kiw_ref, kb_ref, kib_ref, qt_ref, qit_ref, wt_ref, vt_ref):
    cos, s1, s2 = cos_ref[...], s1_ref[...], s2_ref[...]
    bd = bd_ref[...]
    nch = WC // LANES
    r = p_ref.shape[0]
    for c in range(2 * nch):
        x = p_ref[:, c * LANES:(c + 1) * LANES]
        gain = (qn_ref if c < nch else kn_ref)[...]
        n = x * lax.rsqrt(_segsum(x * x, bd) * (1.0 / DH) + EPS) * gain
        rot = _rope(n, cos, s1, s2)
        if c < nch:
            qt_ref[c * LANES:(c + 1) * LANES, :] = (rot * (DH ** -0.5 * LOG2E)).T.astype(BF16)
        else:
            k_ref[:, (c - nch) * LANES:(c - nch + 1) * LANES] = rot
            kb_ref[:, (c - nch) * LANES:(c - nch + 1) * LANES] = rot.astype(BF16)
    v = p_ref[:, 2 * WC:3 * WC]
    v_ref[...] = v
    kt = vt_ref.shape[-1]
    for t in range(r // kt):
        for c in range(nch):
            vt_ref[t, c * LANES:(c + 1) * LANES, :] = v[t * kt:(t + 1) * kt, c * LANES:(c + 1) * LANES].T.astype(BF16)
    for c in range(HI * DI // LANES):
        x = ci_ref[:, c * LANES:(c + 1) * LANES]
        qit_ref[c * LANES:(c + 1) * LANES, :] = _rope(x, cos, s1, s2).T.astype(BF16)
    x = ci_ref[:, HI * DI:HI * DI + LANES]
    lane = lax.broadcasted_iota(I32, x.shape, 1)
    is_ki = lane < DI
    kiw = _rope(x, jnp.where(is_ki, cos, 1.0), jnp.where(is_ki, s1, 0.0), jnp.where(is_ki, s2, 0.0))
    kiw_ref[...] = kiw
    kib_ref[...] = kiw[:, 0:DI].astype(BF16)
    wt_ref[...] = kiw.T[LANE_W:LANE_W + SUBLANES, :]


def _attn_prep_t(p, bsz, t, r, kt, tabs, bd, qn, kn):
    nt = t // r
    m = bsz * t
    assert t % r == 0 and r % kt == 0 and kt % LANES == 0
    rows = lambda b, i: (b * nt + i, 0)
    tmap = lambda b, i: (i, 0)
    cst = lambda b, i: (0, 0)
    return pl.pallas_call(
        _attnprep_t_body,
        grid=(bsz, nt),
        in_specs=[pl.BlockSpec((r, 3 * WC), lambda b, i: (b * nt + i, OFF_CQKV // (3 * WC))),
                  pl.BlockSpec((r, 384), lambda b, i: (b * nt + i, OFF_CI // 384)),
                  pl.BlockSpec((r, LANES), tmap), pl.BlockSpec((r, LANES), tmap), pl.BlockSpec((r, LANES), tmap),
                  pl.BlockSpec((LANES, LANES), cst),
                  pl.BlockSpec((1, LANES), cst),
                  pl.BlockSpec((1, LANES), cst)],
        out_specs=[pl.BlockSpec((r, WC), rows), pl.BlockSpec((r, WC), rows), pl.BlockSpec((r, LANES), rows),
                   pl.BlockSpec((r, WC), rows), pl.BlockSpec((r, DI), rows),
                   pl.BlockSpec((None, WC, r), lambda b, i: (b, 0, i)),
                   pl.BlockSpec((None, HI * DI, r), lambda b, i: (b, 0, i)),
                   pl.BlockSpec((None, SUBLANES, r), lambda b, i: (b, 0, i)),
                   pl.BlockSpec((None, r // kt, WC, kt), lambda b, i: (b, i, 0, 0))],
        out_shape=[jax.ShapeDtypeStruct((m, WC), F32), jax.ShapeDtypeStruct((m, WC), F32),
                   jax.ShapeDtypeStruct((m, LANES), F32),
                   jax.ShapeDtypeStruct((m, WC), BF16), jax.ShapeDtypeStruct((m, DI), BF16),
                   jax.ShapeDtypeStruct((bsz, WC, t), BF16),
                   jax.ShapeDtypeStruct((bsz, HI * DI, t), BF16),
                   jax.ShapeDtypeStruct((bsz, SUBLANES, t), F32),
                   jax.ShapeDtypeStruct((bsz, t // kt, WC, kt), BF16)],
        compiler_params=_cparams(("parallel", "parallel")),
    )(p, p, *tabs, bd, qn, kn)


def _float_key(sc):
    bits = lax.bitcast_convert_type(sc, I32)
    return bits ^ ((bits >> 31) & 0x7FFFFFFF)


def _kth_largest_key(count_ge, shape, topk, bits_per_step=1):
    c0 = count_ge(jnp.zeros(shape, I32))
    base = jnp.where(c0 >= topk, 0, INT_MIN).astype(I32)
    above = jnp.where(c0 >= topk, 0.0, c0)

    def resolve(low, nbits, st):
        base, above = st
        best, failed = base, jnp.zeros(shape, jnp.bool_)
        for m in range(1, 2 ** nbits):
            cand = base + (jnp.int32(m) << low)
            cnt = count_ge(cand)
            ok = cnt >= topk
            best = jnp.where(ok, cand, best)
            above = jnp.where(~ok & ~failed, cnt, above)
            failed = failed | ~ok
        return best, above

    nfull, rest = divmod(31, bits_per_step)
    st = lax.fori_loop(0, nfull, lambda i, s: resolve(31 - (i + 1) * bits_per_step, bits_per_step, s), (base, above))
    return resolve(0, rest, st) if rest else st


def _select_tile(key, vstar, need, running, tri, visible):
    eq = key == vstar
    pref = _dot(eq.astype(BF16), tri)
    sel = (key > vstar) | (eq & (running + pref <= need))
    if visible is not None:
        sel = sel & visible
    return sel, running + pref[:, LANES - 1:LANES]


def _attn_prompt_body(qt_ref, qit_ref, wt_ref, k_ref, vt_ref, ki_ref, ltri_ref, o_ref,
                      key_scr, acc_scr, m_scr, l_scr, raw_scr, s_scr, p_scr, a_scr, bias_scr, *, tq, kt, topk):
    qblk = pl.program_id(1)
    nkb = (qblk * tq + tq + kt - 1) // kt
    qpos = qblk * tq + lax.broadcasted_iota(I32, (1, tq), 1)
    krow = lax.broadcasted_iota(I32, (kt, 1), 0)
    w = wt_ref[...] * ((HI * DI) ** -0.5)
    qit = qit_ref[...]
    qi_cat = jnp.concatenate([qit[h * DI:(h + 1) * DI, :] for h in range(HI)], axis=1)

    def key_tile(j):
        return pl.ds(pl.multiple_of(jnp.minimum(j, nkb - 1) * kt, kt), kt)

    def idx_dot(j):
        return _dot(ki_ref[key_tile(j), :], qi_cat)

    npairs_kt = (nkb + 1) // 2

    def fill_tile(j, cur):
        raw_scr[1 - cur] = idx_dot(j + 1)
        sc = None
        for h in range(HI):
            term = w[h:h + 1, :] * jnp.maximum(raw_scr[cur, :, h * tq:(h + 1) * tq], 0.0)
            sc = term if sc is None else sc + term
        key_scr[j] = jnp.where((krow + j * kt) <= qpos, _float_key(sc), INT_MIN)

    def fill(i, carry):
        fill_tile(2 * i, 0)
        fill_tile(2 * i + 1, 1)
        return carry

    raw_scr[0] = idx_dot(0)
    lax.fori_loop(0, npairs_kt, fill, 0)

    def count_ge(cand):
        def body(i, acc):
            for j in (2 * i, 2 * i + 1):
                ind = jnp.where(key_scr[j] >= cand, 1.0, 0.0)
                acc = acc + jnp.sum(ind.reshape(kt // COUNT_ROWS, COUNT_ROWS, tq), axis=0)
            return acc
        acc = lax.fori_loop(0, npairs_kt, body, jnp.zeros((COUNT_ROWS, tq), F32))
        return jnp.sum(acc, axis=0, keepdims=True)

    npair = HC // 2
    half = lax.broadcasted_iota(I32, (LANES, tq), 0) // DH
    q_pairs = []
    for c in range(npair):
        qc = qt_ref[c * LANES:(c + 1) * LANES, :].astype(F32)
        q_pairs.append(jnp.concatenate([jnp.where(half == 0, qc, 0.0), jnp.where(half == 1, qc, 0.0)],
                                       axis=1).astype(BF16))

    def store_scores(j, slot):
        kb = k_ref[key_tile(j), :]
        for c in range(npair):
            s_scr[slot, c] = _dot(kb[:, c * LANES:(c + 1) * LANES], q_pairs[c])

    vstar, above = _kth_largest_key(count_ge, (1, tq), topk)
    need = topk - above

    store_scores(0, 0)
    m_scr[...] = jnp.full_like(m_scr, NEG_BIG)
    l_scr[...] = jnp.zeros_like(l_scr)
    acc_scr[...] = jnp.zeros_like(acc_scr)
    ltri = ltri_ref[...]

    def store_bias(j, slot, running):
        key = key_scr[jnp.minimum(j, nkb - 1)]
        eq = key == vstar
        pref = _dot(ltri, eq.astype(BF16))
        sel = ((key > vstar) | (eq & (running + pref <= need))) & ((krow + j * kt) <= qpos)
        bias_scr[slot] = jnp.where(sel, 0.0, -jnp.inf)
        return running + pref[kt - 1:kt, :]

    def apply_pv(j, slot):
        vtb = vt_ref[jnp.clip(j, 0, nkb - 1)]
        for c in range(npair):
            pv = _dot(vtb[c * LANES:(c + 1) * LANES, :], p_scr[slot, c])
            for hh in range(2):
                h = 2 * c + hh
                acc_scr[h * DH:(h + 1) * DH, :] = (a_scr[slot, h:h + 1, :] * acc_scr[h * DH:(h + 1) * DH, :]
                                                   + pv[hh * DH:(hh + 1) * DH, hh * tq:(hh + 1) * tq])

    def sweep_tile(j, cur, running):
        nxt = 1 - cur
        apply_pv(j - 1, nxt)
        store_scores(j + 1, nxt)
        running = store_bias(j + 1, nxt, running)
        bias = bias_scr[cur]
        for c in range(npair):
            pc = []
            for hh in range(2):
                h = 2 * c + hh
                s = s_scr[cur, c, :, hh * tq:(hh + 1) * tq] + bias
                m_old = m_scr[h:h + 1, :]
                m_new = jnp.maximum(m_old, jnp.max(s, axis=0, keepdims=True))
                p = jnp.exp2(s - m_new)
                alpha = jnp.exp2(m_old - m_new)
                l_scr[h:h + 1, :] = alpha * l_scr[h:h + 1, :] + jnp.sum(p, axis=0, keepdims=True)
                m_scr[h:h + 1, :] = m_new
                a_scr[cur, h:h + 1, :] = alpha
                pc.append(p.astype(BF16))
            p_scr[cur, c] = jnp.concatenate(pc, axis=1)
        return running

    def sweep(i, running):
        return sweep_tile(2 * i + 1, 1, sweep_tile(2 * i, 0, running))

    p_scr[1] = jnp.zeros_like(p_scr[1])
    a_scr[1] = jnp.ones_like(a_scr[1])
    lax.fori_loop(0, npairs_kt, sweep, store_bias(0, 0, jnp.zeros((1, tq), F32)))
    apply_pv(2 * npairs_kt - 1, 1)
    for c in range(npair):
        ot = jnp.concatenate([acc_scr[h * DH:(h + 1) * DH, :] / l_scr[h:h + 1, :] for h in (2 * c, 2 * c + 1)],
                             axis=0)
        o_ref[:, c * LANES:(c + 1) * LANES] = ot.T


def _attn_prompt(qt, qit, wt, kbf, vt, kibf, ltri, bsz, s_len):
    tq = LANES
    kt = vt.shape[-1]
    nq = s_len // tq
    topk = min(TOPK_MAX, s_len // 4)
    assert s_len % kt == 0 and kt % tq == 0
    k3 = kbf.reshape(bsz, s_len, WC)
    ki3 = kibf.reshape(bsz, s_len, DI)
    return pl.pallas_call(
        functools.partial(_attn_prompt_body, tq=tq, kt=kt, topk=topk),
        grid=(bsz, nq),
        in_specs=[pl.BlockSpec((None, WC, tq), lambda b, i: (b, 0, i)),
                  pl.BlockSpec((None, HI * DI, tq), lambda b, i: (b, 0, i)),
                  pl.BlockSpec((None, SUBLANES, tq), lambda b, i: (b, 0, i)),
                  pl.BlockSpec((None, s_len, WC), lambda b, i: (b, 0, 0)),
                  pl.BlockSpec((None, s_len // kt, WC, kt), lambda b, i: (b, 0, 0, 0)),
                  pl.BlockSpec((None, s_len, DI), lambda b, i: (b, 0, 0)),
                  pl.BlockSpec((kt, kt), lambda b, i: (0, 0))],
        out_specs=pl.BlockSpec((tq, WC), lambda b, i: (b * nq + i, 0)),
        out_shape=jax.ShapeDtypeStruct((bsz * s_len, WC), F32),
        scratch_shapes=[pltpu.VMEM((s_len // kt, kt, tq), I32),
                        pltpu.VMEM((WC, tq), F32),
                        pltpu.VMEM((SUBLANES, tq), F32),
                        pltpu.VMEM((SUBLANES, tq), F32),
                        pltpu.VMEM((2, kt, HI * tq), F32),
                        pltpu.VMEM((2, HC // 2, kt, 2 * tq), F32),
                        pltpu.VMEM((2, HC // 2, kt, 2 * tq), BF16),
                        pltpu.VMEM((2, SUBLANES, tq), F32),
                        pltpu.VMEM((2, kt, tq), F32)],
        compiler_params=_cparams(("parallel", "arbitrary")),
    )(qt, qit, wt, k3, vt, ki3, ltri)


def _attn_sample_body(pt_ref, q_ref, qi_ref, kiw_ref, knew_ref, vnew_ref, tri_ref, *refs, t, npg, topk):
    kidx_refs, k_refs, v_refs = refs[0:npg], refs[npg:2 * npg], refs[2 * npg:3 * npg]
    o_ref, s_scr, kipad_scr, kpad_scr, vpad_scr = refs[3 * npg:]
    nt = npg + 1
    qi = qi_ref[...]
    qi_stack = jnp.concatenate([qi[:, h * DI:(h + 1) * DI] for h in range(HI)], axis=0)
    w = kiw_ref[:, LANE_W:LANE_W + HI] * ((HI * DI) ** -0.5)
    w_stack = jnp.concatenate([w[:, h:h + 1] for h in range(HI)], axis=0)

    def score_keys(kib):
        rel = jnp.maximum(_dot_nt(qi_stack, kib), 0.0) * w_stack
        sc = rel[0:t]
        for h in range(1, HI):
            sc = sc + rel[h * t:(h + 1) * t]
        return _float_key(sc)

    keys = [score_keys(kidx_refs[j][...].astype(BF16)) for j in range(npg)]
    kipad_scr[...] = jnp.zeros_like(kipad_scr)
    kipad_scr[0:t, :] = kiw_ref[:, 0:DI]
    vis_new = lax.broadcasted_iota(I32, (t, LANES), 1) <= lax.broadcasted_iota(I32, (t, LANES), 0)
    keys.append(jnp.where(vis_new, score_keys(kipad_scr[...].astype(BF16)), INT_MIN))

    def count_ge(cand):
        terms = [jnp.where(key >= cand, 1.0, 0.0) for key in keys]
        while len(terms) > 1:
            terms = [a + b for a, b in zip(terms[0::2], terms[1::2])] + terms[len(terms) & ~1:]
        return jnp.sum(terms[0], axis=1, keepdims=True)

    vstar, above = _kth_largest_key(count_ge, (t, 1), topk, bits_per_step=3)
    need = topk - above

    kpad_scr[...] = jnp.zeros_like(kpad_scr)
    vpad_scr[...] = jnp.zeros_like(vpad_scr)
    kpad_scr[0:t, :] = knew_ref[...]
    vpad_scr[0:t, :] = vnew_ref[...]
    head_of_lane = lax.broadcasted_iota(I32, (t, WC), 1) // DH
    q = q_ref[...].astype(F32)
    q_stack = jnp.concatenate([jnp.where(head_of_lane == h, q, 0.0) for h in range(HC)], axis=0).astype(BF16)
    tri = tri_ref[...]
    running = jnp.zeros((t, 1), F32)
    m_acc = jnp.full((HC * t, LANES), NEG_BIG, F32)
    for j in range(nt):
        sel, running = _select_tile(keys[j], vstar, need, running, tri, vis_new if j == npg else None)
        bias = jnp.where(sel, 0.0, -jnp.inf)
        kb = (k_refs[j] if j < npg else kpad_scr)[...].astype(BF16)
        s = _dot_nt(q_stack, kb) + jnp.concatenate([bias] * HC, axis=0)
        s_scr[j] = s
        m_acc = jnp.maximum(m_acc, s)
    m = jnp.max(m_acc, axis=1, keepdims=True)
    l_acc = jnp.zeros((HC * t, LANES), F32)
    acc = jnp.zeros((HC * t, WC), F32)
    for j in range(nt):
        p = jnp.exp(s_scr[j] - m)
        l_acc = l_acc + p
        vb = (v_refs[j] if j < npg else vpad_scr)[...].astype(BF16)
        acc = acc + _dot(p.astype(BF16), vb)
    o = acc / jnp.sum(l_acc, axis=1, keepdims=True)
    out = jnp.where(head_of_lane == 0, o[0:t], 0.0)
    for h in range(1, HC):
        out = out + jnp.where(head_of_lane == h, o[h * t:(h + 1) * t], 0.0)
    o_ref[...] = out


def _attn_sample(page_table, q, qi, kiw, knew, vnew, tri, cache_kidx, cache_k, cache_v, layer, bsz, t):
    npg = page_table.shape[1]
    page = cache_k.shape[2]
    assert page == LANES
    topk = min(TOPK_MAX, (npg * page + t) // 4)
    row = lambda b, pt: (b, 0)

    def page_map(j):
        return lambda b, pt: (layer, pt[b, j], 0, 0)

    in_specs = [pl.BlockSpec((t, WC), row), pl.BlockSpec((t, HI * DI), row), pl.BlockSpec((t, LANES), row),
                pl.BlockSpec((t, WC), row), pl.BlockSpec((t, WC), row),
                pl.BlockSpec((LANES, LANES), lambda b, pt: (0, 0))]
    in_specs += [pl.BlockSpec((None, None, page, DI), page_map(j)) for j in range(npg)]
    in_specs += [pl.BlockSpec((None, None, page, WC), page_map(j)) for j in range(npg)]
    in_specs += [pl.BlockSpec((None, None, page, WC), page_map(j)) for j in range(npg)]
    grid_spec = pltpu.PrefetchScalarGridSpec(
        num_scalar_prefetch=1,
        grid=(bsz,),
        in_specs=in_specs,
        out_specs=pl.BlockSpec((t, WC), row),
        scratch_shapes=[pltpu.VMEM((npg + 1, HC * t, LANES), F32), pltpu.VMEM((LANES, DI), F32),
                        pltpu.VMEM((LANES, WC), F32), pltpu.VMEM((LANES, WC), F32)])
    return pl.pallas_call(
        functools.partial(_attn_sample_body, t=t, npg=npg, topk=topk),
        grid_spec=grid_spec,
        out_shape=jax.ShapeDtypeStruct((bsz * t, WC), F32),
        compiler_params=_cparams(("parallel",)),
    )(page_table, q, qi, kiw, knew, vnew, tri,
      *([cache_kidx] * npg), *([cache_k] * npg), *([cache_v] * npg))


def _rope_tables(pos):
    half = ROPE_DIM // 2
    inv_freq = ROPE_THETA ** (-jnp.arange(half, dtype=F32) / half)
    ang = pos.astype(F32)[:, None] * inv_freq[None, :]
    cos, sin = jnp.cos(ang), jnp.sin(ang)
    n = pos.shape[0]
    one = jnp.ones((n, DH - ROPE_DIM), F32)
    zero_h = jnp.zeros((n, half), F32)
    zero_r = jnp.zeros((n, DH - ROPE_DIM), F32)
    c64 = jnp.concatenate([cos, cos, one], axis=1)
    s1 = jnp.concatenate([-sin, zero_h, zero_r], axis=1)
    s2 = jnp.concatenate([zero_h, sin, zero_r], axis=1)
    rep = LANES // DH
    return tuple(jnp.tile(a, (1, rep)) for a in (c64, s1, s2))


def _reorder_w_in(w_in):
    sizes = (WA, WA, WA, QKV_B, WB, HB, HB, 3 * WC, HI * DI, DI, HI)
    offs = [0]
    for s in sizes:
        offs.append(offs[-1] + s)
    seg = lambda i: w_in[:, :, offs[i]:offs[i + 1]]
    pad = jnp.zeros(w_in.shape[:2] + (LANES - DI - HI - 2 * HB,), w_in.dtype)
    order = [seg(3), seg(7), seg(0), seg(1), seg(2), seg(4), seg(8), seg(9), seg(10), seg(5), seg(6), pad]
    return jnp.concatenate(order, axis=-1).astype(BF16)


def _pad_state(prev):
    return jnp.pad(prev, ((0, 0), (SUBLANES - prev.shape[1], 0), (0, 0)))


def kernel(x_prompt, x_sample, state_conv_a, state_gdn_conv, state_gdn, cache_k, cache_v, cache_kidx, page_table, ffn1_norm, ffn1_w_gate, ffn1_w_up, ffn1_w_down, mix_norm, w_in, conv_a_w, gdn_conv_w, gdn_a_log, gdn_dt_bias, gdn_out_norm, attn_q_norm, attn_k_norm, w_out, ffn2_norm, ffn2_w_gate, ffn2_w_up, ffn2_w_down):
    depth = w_in.shape[0]
    bp, tp, d = x_prompt.shape
    bs, ts, _ = x_sample.shape
    mp, ms = bp * tp, bs * ts
    npool, page = cache_k.shape[1], cache_k.shape[2]
    past = page_table.shape[1] * page
    assert ts == SUBLANES and tp % LANES == 0 and mp % 256 == 0

    x = jnp.concatenate([x_prompt.reshape(mp, d), x_sample.reshape(ms, d)], axis=0)
    w_in_r = _reorder_w_in(w_in)
    bf = lambda a: a.astype(BF16)
    f1g, f1u, f1d = bf(ffn1_w_gate), bf(ffn1_w_up), bf(ffn1_w_down)
    f2g, f2u, f2d = bf(ffn2_w_gate), bf(ffn2_w_up), bf(ffn2_w_down)
    w_out_b = bf(w_out)
    ck = cache_k.reshape(depth, npool, page, WC)
    cv = cache_v.reshape(depth, npool, page, WC)

    li = jnp.arange(LANES)
    bd = ((li[:, None] // DH) == (li[None, :] // DH)).astype(BF16)
    tri = (li[:, None] <= li[None, :]).astype(BF16)
    kt = r_p = _pick(tp, (256, 128))
    lk = jnp.arange(kt)
    ltri = (lk[None, :] <= lk[:, None]).astype(BF16)
    r_s = _pick(ms, (128, 64, 32, 16, 8))
    tabs_p = _rope_tables(jnp.arange(tp))
    tabs_s = _rope_tables(jnp.tile(past + jnp.arange(ts), r_s // ts))
    zero_a = jnp.zeros((bp, SUBLANES, WA), F32)
    zero_b = jnp.zeros((bp, SUBLANES, QKV_B), F32)
    zero_s = jnp.zeros((bp, HB, DKB, DVB), F32)
    lane_pad = lambda v, off: jnp.zeros((1, LANES), F32).at[0, off:off + v.shape[0]].set(v)

    outs = [[] for _ in range(12)]
    for l in range(depth):
        x = _ffn(x, ffn1_norm[l][None], f1g[l], f1u[l], f1d[l])
        p = _inproj(x, mix_norm[l][None], w_in_r[l])
        avec = lane_pad(jnp.exp(gdn_a_log[l].astype(F32)), LANE_GA)
        dtb = lane_pad(gdn_dt_bias[l].astype(F32), LANE_GA)
        gn = gdn_out_norm[l][None].astype(F32)
        qn = jnp.tile(attn_q_norm[l].astype(F32), LANES // DH)[None]
        kn = jnp.tile(attn_k_norm[l].astype(F32), LANES // DH)[None]

        ya, yb, yc = [], [], []
        groups = ((0, bp, tp, zero_a, zero_b, zero_s), (mp, bs, ts, _pad_state(state_conv_a[l]),
                                                       _pad_state(state_gdn_conv[l]), state_gdn[l]))
        for gi, (row0, bsz, t, prev_a, prev_b, s0) in enumerate(groups):
            m = bsz * t
            y_a, st_a = _mixer_a(p, row0, bsz, t, conv_a_w[l], prev_a)
            qg, kg, vg, gbg, st_b = _gdn_prep(p, row0, bsz, t, gdn_conv_w[l], prev_b, bd, avec, dtb)
            y_b, s_new = _gdn(qg, kg, vg, gbg, p, row0, bsz, t, s0, gn)
            if gi == 0:
                kc, vc, kiw, kbf, kibf, qt, qit, wt, vt = _attn_prep_t(p, bsz, t, r_p, kt, tabs_p, bd, qn, kn)
                y_c = _attn_prompt(qt, qit, wt, kbf, vt, kibf, ltri, bsz, t)
            else:
                qc, kc, vc, qi, kiw = _attn_prep(p, row0, m, r_s, tabs_s, bd, qn, kn)
                y_c = _attn_sample(page_table, qc, qi, kiw, kc, vc, tri, cache_kidx, ck, cv, l, bsz, t)
            ya.append(y_a)
            yb.append(y_b)
            yc.append(y_c)
            res = (st_a[:, SUBLANES - (CONV_A - 1):], st_b[:, SUBLANES - (GDN_CONV - 1):], s_new,
                   kc.reshape(bsz, t, HC, DH), vc.reshape(bsz, t, HC, DH), kiw[:, :DI].reshape(bsz, t, DI))
            for i, r in enumerate(res):
                outs[gi * 6 + i].append(r)
        x = _outproj(x, jnp.concatenate(ya), jnp.concatenate(yb), jnp.concatenate(yc), w_out_b[l])
        x = _ffn(x, ffn2_norm[l][None], f2g[l], f2u[l], f2d[l])

    y_prompt = x[:mp].reshape(bp, tp, d)
    y_sample = x[mp:].reshape(bs, ts, d)
    return (y_prompt, y_sample) + tuple(jnp.stack(o) for o in outs)
```

```python
import functools
import math

import jax
import jax.numpy as jnp
from jax import lax
from jax.experimental import pallas as pl
from jax.experimental.pallas import tpu as pltpu

F32 = jnp.float32
BF16 = jnp.bfloat16
I32 = jnp.int32

EPS = 1e-6
LANES = 128
SUBLANES = 8
VMEM_LIMIT = 56 * 1024 * 1024

A_GROUPS = 4
CONV_A = 3
HB = 6
DKB = 64
DVB = 64
GDN_CONV = 4
GDN_ROWS = 64
HC = 6
DH = 64
HI = 4
DI = 64
TOPK_MAX = 256
ROPE_DIM = DH // 4
ROPE_THETA = 500000.0
INT_MIN = -2 ** 31
COUNT_ROWS = 4 * SUBLANES
NEG_BIG = -1e30
LOG2E = math.log2(math.e)

WA = 256
QKV_B = 2 * HB * DKB + HB * DVB
WB = HB * DVB
WC = HC * DH
OFF_BQKV = 0
OFF_CQKV = QKV_B
OFF_A = 2 * QKV_B
OFF_BZ = OFF_A + 3 * WA
OFF_CI = OFF_BZ + WB
N_P = OFF_CI + 384
LANE_W = DI
LANE_GA = DI + HI
LANE_GB = DI + HI + HB


def _pick(n, cands):
    for c in cands:
        if n % c == 0:
            return c
    raise ValueError(f"no tile for {n}")


def _cparams(sem):
    return pltpu.CompilerParams(dimension_semantics=sem, vmem_limit_bytes=VMEM_LIMIT)


def _dot(a, b):
    return jnp.dot(a, b, preferred_element_type=F32)


def _dot_nt(a, b):
    return lax.dot_general(a, b, (((1,), (1,)), ((), ())), preferred_element_type=F32)


def _split3(x):
    hi = x.astype(BF16)
    r1 = x - hi.astype(F32)
    mid = r1.astype(BF16)
    return hi, mid, (r1 - mid.astype(F32)).astype(BF16)


def _dot01(sel01, x):
    s = sel01.astype(BF16)
    hi, mid, lo = _split3(x)
    return _dot(s, hi) + _dot(s, mid) + _dot(s, lo)


def _dot01_nt(sel01, x):
    s = sel01.astype(BF16)
    hi, mid, lo = _split3(x)
    return _dot_nt(s, hi) + _dot_nt(s, mid) + _dot_nt(s, lo)


def _dot3(a, b):
    ah = a.astype(BF16)
    al = (a - ah.astype(F32)).astype(BF16)
    bh = b.astype(BF16)
    bl = (b - bh.astype(F32)).astype(BF16)
    return _dot(ah, bh) + _dot(ah, bl) + _dot(al, bh)


def _segsum(x, bd):
    hi = x.astype(BF16)
    lo = (x - hi.astype(F32)).astype(BF16)
    return _dot(hi, bd) + _dot(lo, bd)


def _sigmoid(x):
    return 1.0 / (1.0 + jnp.exp(-x))


def _silu(x):
    return x * _sigmoid(x)


def _softplus(x):
    return jnp.maximum(x, 0.0) + jnp.log(1.0 + jnp.exp(-jnp.abs(x)))


def _ffn_body(x_ref, g_ref, wg_ref, wu_ref, wd_ref, o_ref, h_ref, acc_ref):
    j = pl.program_id(1)

    @pl.when(j == 0)
    def _():
        x = x_ref[...]
        ms = jnp.mean(x * x, axis=-1, keepdims=True)
        h_ref[...] = (x * lax.rsqrt(ms + EPS) * g_ref[...]).astype(BF16)
        acc_ref[...] = jnp.zeros_like(acc_ref)

    h = h_ref[...]
    a = _dot(h, wg_ref[...])
    b = _dot(h, wu_ref[...])
    t = (_silu(a) * b).astype(BF16)
    acc_ref[...] += _dot(t, wd_ref[...])

    @pl.when(j == pl.num_programs(1) - 1)
    def _():
        o_ref[...] = x_ref[...] + 0.5 * acc_ref[...]


def _ffn(x, g, wg, wu, wd):
    m, d = x.shape
    f = wg.shape[1]
    tm = _pick(m, (512, 256, 128, 64, 32, 16, 8))
    tf = f
    once = pl.Buffered(1)
    return pl.pallas_call(
        _ffn_body,
        grid=(m // tm, f // tf),
        in_specs=[pl.BlockSpec((tm, d), lambda i, j: (i, 0)),
                  pl.BlockSpec((1, d), lambda i, j: (0, 0)),
                  pl.BlockSpec((d, tf), lambda i, j: (0, j), pipeline_mode=once),
                  pl.BlockSpec((d, tf), lambda i, j: (0, j), pipeline_mode=once),
                  pl.BlockSpec((tf, d), lambda i, j: (j, 0), pipeline_mode=once)],
        out_specs=pl.BlockSpec((tm, d), lambda i, j: (i, 0)),
        out_shape=jax.ShapeDtypeStruct((m, d), F32),
        scratch_shapes=[pltpu.VMEM((tm, d), BF16), pltpu.VMEM((tm, d), F32)],
        compiler_params=_cparams(("parallel", "arbitrary")),
    )(x, g, wg, wu, wd)


def _inproj_body(x_ref, g_ref, w_ref, o_ref, h_ref):
    @pl.when(pl.program_id(1) == 0)
    def _():
        x = x_ref[...]
        ms = jnp.mean(x * x, axis=-1, keepdims=True)
        h_ref[...] = (x * lax.rsqrt(ms + EPS) * g_ref[...]).astype(BF16)

    o_ref[...] = _dot(h_ref[...], w_ref[...])


def _inproj(x, g, w):
    m, d = x.shape
    n = w.shape[1]
    tm = _pick(m, (512, 256, 128, 64, 32, 16, 8))
    tn = _pick(n, (3840, 1920, 768, 384, 128))
    return pl.pallas_call(
        _inproj_body,
        grid=(m // tm, n // tn),
        in_specs=[pl.BlockSpec((tm, d), lambda i, j: (i, 0)),
                  pl.BlockSpec((1, d), lambda i, j: (0, 0)),
                  pl.BlockSpec((d, tn), lambda i, j: (0, j))],
        out_specs=pl.BlockSpec((tm, tn), lambda i, j: (i, j)),
        out_shape=jax.ShapeDtypeStruct((m, n), F32),
        scratch_shapes=[pltpu.VMEM((tm, d), BF16)],
        compiler_params=_cparams(("parallel", "arbitrary")),
    )(x, g, w)


def _outproj_body(x_ref, ya_ref, yb_ref, yc_ref, w_ref, o_ref):
    acc = _dot(ya_ref[...].astype(BF16), w_ref[0:WA, :])
    acc += _dot(yb_ref[...].astype(BF16), w_ref[WA:WA + WB, :])
    acc += _dot(yc_ref[...].astype(BF16), w_ref[WA + WB:WA + WB + WC, :])
    o_ref[...] = x_ref[...] + acc


def _outproj(x, ya, yb, yc, w):
    m, d = x.shape
    tm = _pick(m, (512, 256, 128, 64, 32, 16, 8))
    return pl.pallas_call(
        _outproj_body,
        grid=(m // tm,),
        in_specs=[pl.BlockSpec((tm, d), lambda i: (i, 0)),
                  pl.BlockSpec((tm, WA), lambda i: (i, 0)),
                  pl.BlockSpec((tm, WB), lambda i: (i, 0)),
                  pl.BlockSpec((tm, WC), lambda i: (i, 0)),
                  pl.BlockSpec(w.shape, lambda i: (0, 0))],
        out_specs=pl.BlockSpec((tm, d), lambda i: (i, 0)),
        out_shape=jax.ShapeDtypeStruct((m, d), F32),
        compiler_params=_cparams(("parallel",)),
    )(x, ya, yb, yc, w)


def _conv_taps(u, w_ref, prev_ref, ext_ref, tt, width):
    it = pl.program_id(1)

    @pl.when(it == 0)
    def _():
        ext_ref[:, 0:SUBLANES, :] = prev_ref[...]

    @pl.when(it > 0)
    def _():
        ext_ref[:, 0:SUBLANES, :] = ext_ref[:, tt:tt + SUBLANES, :]

    ext_ref[:, SUBLANES:SUBLANES + tt, :] = u
    acc = None
    for s in range(width):
        term = ext_ref[:, SUBLANES - s:SUBLANES - s + tt, :] * w_ref[width - 1 - s:width - s, :]
        acc = term if acc is None else acc + term
    return acc, ext_ref[:, tt:tt + SUBLANES, :]


def _group_tiles(bsz, t):
    if t >= 128:
        return 1, _pick(t, (256, 128))
    return _pick(bsz, (16, 8, 4, 2, 1)), t


def _mixa_body(p_ref, w_ref, prev_ref, y_ref, st_ref, ext_ref, *, bb, tt):
    pa = p_ref[...]
    u = (pa[:, WA:2 * WA] * pa[:, 2 * WA:3 * WA]).reshape(bb, tt, WA)
    conv, tail = _conv_taps(u, w_ref, prev_ref, ext_ref, tt, CONV_A)
    y_ref[...] = pa[:, 0:WA] * conv.reshape(bb * tt, WA)
    st_ref[...] = tail


def _mixer_a(p, row0, bsz, t, w, prev8):
    bb, tt = _group_tiles(bsz, t)
    r = bb * tt
    nb, nt = bsz // bb, t // tt
    base = row0 // r
    assert row0 % r == 0
    return pl.pallas_call(
        functools.partial(_mixa_body, bb=bb, tt=tt),
        grid=(nb, nt),
        in_specs=[pl.BlockSpec((r, 3 * WA), lambda i, j: (base + i * nt + j, OFF_A // (3 * WA))),
                  pl.BlockSpec(w.shape, lambda i, j: (0, 0)),
                  pl.BlockSpec((bb, SUBLANES, WA), lambda i, j: (i, 0, 0))],
        out_specs=[pl.BlockSpec((r, WA), lambda i, j: (i * nt + j, 0)),
                   pl.BlockSpec((bb, SUBLANES, WA), lambda i, j: (i, 0, 0))],
        out_shape=[jax.ShapeDtypeStruct((bsz * t, WA), F32),
                   jax.ShapeDtypeStruct((bsz, SUBLANES, WA), F32)],
        scratch_shapes=[pltpu.VMEM((bb, tt + SUBLANES, WA), F32)],
        compiler_params=_cparams(("parallel", "arbitrary")),
    )(p, w, prev8)


def _gdnprep_body(p_ref, ci_ref, w_ref, prev_ref, bd_ref, avec_ref, dtb_ref,
                  q_ref, k_ref, v_ref, gb_ref, st_ref, ext_ref, *, bb, tt):
    r = bb * tt
    x = p_ref[...].reshape(bb, tt, QKV_B)
    conv, tail = _conv_taps(x, w_ref, prev_ref, ext_ref, tt, GDN_CONV)
    st_ref[...] = tail
    a = _silu(conv.reshape(r, QKV_B))
    bd = bd_ref[...]
    nq = HB * DKB // LANES
    for c in range(2 * nq):
        xc = a[:, c * LANES:(c + 1) * LANES]
        n = xc * lax.rsqrt(_segsum(xc * xc, bd) + EPS)
        if c < nq:
            q_ref[:, c * LANES:(c + 1) * LANES] = n * (DKB ** -0.5)
        else:
            k_ref[:, (c - nq) * LANES:(c - nq + 1) * LANES] = n
    v_ref[...] = a[:, 2 * HB * DKB:]
    raw = ci_ref[...]
    g = -avec_ref[...] * _softplus(raw + dtb_ref[...])
    lane = lax.broadcasted_iota(I32, raw.shape, 1)
    gb_ref[...] = jnp.where((lane >= LANE_GB) & (lane < LANE_GB + HB), _sigmoid(raw), g)


def _gdn_prep(p, row0, bsz, t, w, prev8, bd, avec, dtb):
    bb, tt = _group_tiles(bsz, t)
    r = bb * tt
    nb, nt = bsz // bb, t // tt
    base = row0 // r
    assert row0 % r == 0
    m = bsz * t
    rows = lambda i, j: (i * nt + j, 0)
    return pl.pallas_call(
        functools.partial(_gdnprep_body, bb=bb, tt=tt),
        grid=(nb, nt),
        in_specs=[pl.BlockSpec((r, QKV_B), lambda i, j: (base + i * nt + j, OFF_BQKV // QKV_B)),
                  pl.BlockSpec((r, LANES), lambda i, j: (base + i * nt + j, (OFF_CI + 2 * LANES) // LANES)),
                  pl.BlockSpec(w.shape, lambda i, j: (0, 0)),
                  pl.BlockSpec((bb, SUBLANES, QKV_B), lambda i, j: (i, 0, 0)),
                  pl.BlockSpec((LANES, LANES), lambda i, j: (0, 0)),
                  pl.BlockSpec((1, LANES), lambda i, j: (0, 0)),
                  pl.BlockSpec((1, LANES), lambda i, j: (0, 0))],
        out_specs=[pl.BlockSpec((r, HB * DKB), rows),
                   pl.BlockSpec((r, HB * DKB), rows),
                   pl.BlockSpec((r, WB), rows),
                   pl.BlockSpec((r, LANES), rows),
                   pl.BlockSpec((bb, SUBLANES, QKV_B), lambda i, j: (i, 0, 0))],
        out_shape=[jax.ShapeDtypeStruct((m, HB * DKB), F32),
                   jax.ShapeDtypeStruct((m, HB * DKB), F32),
                   jax.ShapeDtypeStruct((m, WB), F32),
                   jax.ShapeDtypeStruct((m, LANES), F32),
                   jax.ShapeDtypeStruct((bsz, SUBLANES, QKV_B), F32)],
        scratch_shapes=[pltpu.VMEM((bb, tt + SUBLANES, QKV_B), F32)],
        compiler_params=_cparams(("parallel", "arbitrary")),
    )(p, p, w, prev8, bd, avec, dtb)


def _gdn_body(q_ref, k_ref, v_ref, gb_ref, z_ref, s0_ref, gn_ref, y_ref, sout_ref, s_scr, *, c, g, nch):
    r = 2 * GDN_ROWS
    n = 2 * g
    it = pl.program_id(1)

    @pl.when(it == 0)
    def _():
        s_scr[...] = s0_ref[...]

    ri = lax.broadcasted_iota(I32, (r, r), 0)
    ci = lax.broadcasted_iota(I32, (r, r), 1)
    same = (ri // c) == (ci // c)
    incl = same & (ci <= ri)
    strict = same & (ci < ri)
    eye = (ri == ci).astype(F32)
    lastsel = (ci == (ri // c) * c + (c - 1)).astype(F32)
    chunks = range(nch)
    rows_of = lambda ch: slice(ch * GDN_ROWS, (ch + 1) * GDN_ROWS)
    gb = [jnp.concatenate([gb_ref[rows_of(ch), :]] * 2, axis=0) for ch in chunks]
    gcum = [_dot01(incl, gb[ch]) for ch in chunks]
    glast = [_dot01(lastsel, gcum[ch]) for ch in chunks]
    lane = lax.broadcasted_iota(I32, (r, LANES), 1)
    lane0 = (lane == 0).astype(F32)
    upper = lax.broadcasted_iota(I32, (r, 1), 0) < GDN_ROWS
    er = lax.broadcasted_iota(I32, (r, n * DKB), 0)
    ec = lax.broadcasted_iota(I32, (r, n * DKB), 1)
    emask = (er // c) == (ec // DKB)
    tr = lax.broadcasted_iota(I32, (DKB, n * DKB), 0)
    tc = lax.broadcasted_iota(I32, (DKB, n * DKB), 1)
    tile_mat = ((tc % DKB) == tr).astype(BF16)
    xr = lax.broadcasted_iota(I32, (n * DKB, r), 0)
    xc = lax.broadcasted_iota(I32, (n * DKB, r), 1)
    emask_t = (xr // DKB) == (xc // c)
    sr = lax.broadcasted_iota(I32, (n * DKB, DKB), 0)
    sc = lax.broadcasted_iota(I32, (n * DKB, DKB), 1)
    tile_mat_t = ((sr % DKB) == sc).astype(BF16)
    esel = (xc == (xr // DKB) * c + (c - 1)).astype(F32)
    gend = [_dot01(esel, gcum[ch]) for ch in chunks]
    s_upper = lax.broadcasted_iota(I32, (n * DKB, 1), 0) < g * DKB
    n_sq = max(0, int(math.ceil(math.log2(c))) - 1)

    npair = HB // 2
    items = [(ch, i) for ch in chunks for i in range(npair)]
    each = lambda f: [f(n_, ch, i) for n_, (ch, i) in enumerate(items)]

    def stack(ref, width):
        return each(lambda n_, ch, i: jnp.concatenate([ref[rows_of(ch), 2 * i * width:(2 * i + 1) * width],
                                                       ref[rows_of(ch), (2 * i + 1) * width:(2 * i + 2) * width]], axis=0))

    def pick(mats, rows_upper, base):
        return each(lambda n_, ch, i: jnp.where(rows_upper, mats[ch][:, base + 2 * i:base + 2 * i + 1],
                                                mats[ch][:, base + 2 * i + 1:base + 2 * i + 2]))

    q, k, v, zz = stack(q_ref, DKB), stack(k_ref, DKB), stack(v_ref, DVB), stack(z_ref, DVB)
    gcol = pick(gcum, upper, LANE_GA)
    glc = pick(glast, upper, LANE_GA)
    beta = pick(gb, upper, LANE_GB)
    srow = [jnp.exp(x) for x in pick(gend, s_upper, LANE_GA)]
    grow = each(lambda n_, ch, i: _dot01_nt(lane0, jnp.where(lane == 0, gcol[n_], 0.0)))
    decay = each(lambda n_, ch, i: jnp.exp(jnp.where(incl, gcol[n_] - grow[n_], -jnp.inf)))
    kb = each(lambda n_, ch, i: k[n_] * beta[n_])
    a_mat = each(lambda n_, ch, i: _dot_nt(kb[n_], k[n_]) * jnp.where(strict, decay[n_], 0.0))
    pw = [-a for a in a_mat]
    t_mat = [eye + x for x in pw]
    for _ in range(n_sq):
        pw = [_dot3(x, x) for x in pw]
        t_mat = each(lambda n_, ch, i: t_mat[n_] + _dot(t_mat[n_], pw[n_]))
    u = each(lambda n_, ch, i: _dot(t_mat[n_], v[n_] * beta[n_]))
    wk = each(lambda n_, ch, i: _dot(t_mat[n_], kb[n_] * jnp.exp(gcol[n_])))
    qk = each(lambda n_, ch, i: _dot_nt(q[n_], k[n_]) * decay[n_])
    wk_e = each(lambda n_, ch, i: jnp.where(emask, _dot(wk[n_].astype(BF16), tile_mat), 0.0))
    qg_e = each(lambda n_, ch, i: jnp.where(emask, _dot((q[n_] * jnp.exp(gcol[n_])).astype(BF16), tile_mat), 0.0))
    kd_t = each(lambda n_, ch, i: jnp.where(
        emask_t, _dot_nt(tile_mat_t, (k[n_] * jnp.exp(glc[n_] - gcol[n_])).astype(BF16)), 0.0))
    s = [s_scr[2 * i:2 * i + 2].reshape(n * DKB, DVB) for i in range(npair)]
    for ch in chunks:
        sel = [ch * npair + i for i in range(npair)]
        v_new = [u[n_] - _dot(wk_e[n_], s[i]) for i, n_ in enumerate(sel)]
        out = [_dot(qg_e[n_], s[i]) + _dot(qk[n_], v_new[i]) for i, n_ in enumerate(sel)]
        s = [s[i] * srow[n_] + _dot(kd_t[n_], v_new[i]) for i, n_ in enumerate(sel)]
        for i, n_ in enumerate(sel):
            ms = jnp.mean(out[i] * out[i], axis=-1, keepdims=True)
            y = out[i] * lax.rsqrt(ms + EPS) * gn_ref[...] * _silu(zz[n_])
            y_ref[rows_of(ch), 2 * i * DVB:(2 * i + 1) * DVB] = y[0:GDN_ROWS]
            y_ref[rows_of(ch), (2 * i + 1) * DVB:(2 * i + 2) * DVB] = y[GDN_ROWS:]
    for i in range(npair):
        s_scr[2 * i:2 * i + 2] = s[i].reshape(2, g * DKB, DVB)

    @pl.when(it == pl.num_programs(1) - 1)
    def _():
        sout_ref[...] = s_scr[...]


def _gdn(q, k, v, gb, p, row0, bsz, t, s0, gn):
    c = min(GDN_ROWS, t)
    g = GDN_ROWS // c
    nch = 2 if (g == 1 and (t // c) % 2 == 0) else 1
    rb = nch * GDN_ROWS
    assert t % (c * nch) == 0 and bsz % g == 0 and row0 % rb == 0
    nb, nt = bsz // g, t // (c * nch)
    base = row0 // rb
    rows = lambda i, j: (i * nt + j, 0)
    s0r = s0.reshape(nb, g, HB, DKB, DVB).transpose(0, 2, 1, 3, 4).reshape(nb, HB, g * DKB, DVB)
    y, s_out = pl.pallas_call(
        functools.partial(_gdn_body, c=c, g=g, nch=nch),
        grid=(nb, nt),
        in_specs=[pl.BlockSpec((rb, HB * DKB), rows),
                  pl.BlockSpec((rb, HB * DKB), rows),
                  pl.BlockSpec((rb, WB), rows),
                  pl.BlockSpec((rb, LANES), rows),
                  pl.BlockSpec((rb, WB), lambda i, j: (base + i * nt + j, OFF_BZ // WB)),
                  pl.BlockSpec((None, HB, g * DKB, DVB), lambda i, j: (i, 0, 0, 0)),
                  pl.BlockSpec((1, DVB), lambda i, j: (0, 0))],
        out_specs=[pl.BlockSpec((rb, WB), rows),
                   pl.BlockSpec((None, HB, g * DKB, DVB), lambda i, j: (i, 0, 0, 0))],
        out_shape=[jax.ShapeDtypeStruct((bsz * t, WB), F32),
                   jax.ShapeDtypeStruct((nb, HB, g * DKB, DVB), F32)],
        scratch_shapes=[pltpu.VMEM((HB, g * DKB, DVB), F32)],
        compiler_params=_cparams(("parallel", "arbitrary")),
    )(q, k, v, gb, p, s0r, gn)
    s_out = s_out.reshape(nb, HB, g, DKB, DVB).transpose(0, 2, 1, 3, 4).reshape(bsz, HB, DKB, DVB)
    return y, s_out


def _rope(x, cos, s1, s2):
    half = ROPE_DIM // 2
    return x * cos + pltpu.roll(x, LANES - half, 1) * s1 + pltpu.roll(x, half, 1) * s2


def _attnprep_body(p_ref, ci_ref, cos_ref, s1_ref, s2_ref, bd_ref, qn_ref, kn_ref,
                   q_ref, k_ref, v_ref, qi_ref, kiw_ref):
    cos, s1, s2 = cos_ref[...], s1_ref[...], s2_ref[...]
    bd = bd_ref[...]
    nch = WC // LANES
    for c in range(2 * nch):
        x = p_ref[:, c * LANES:(c + 1) * LANES]
        gain = (qn_ref if c < nch else kn_ref)[...]
        n = x * lax.rsqrt(_segsum(x * x, bd) * (1.0 / DH) + EPS) * gain
        rot = _rope(n, cos, s1, s2)
        if c < nch:
            q_ref[:, c * LANES:(c + 1) * LANES] = (rot * (DH ** -0.5)).astype(BF16)
        else:
            k_ref[:, (c - nch) * LANES:(c - nch + 1) * LANES] = rot
    v_ref[...] = p_ref[:, 2 * WC:3 * WC]
    for c in range(HI * DI // LANES):
        x = ci_ref[:, c * LANES:(c + 1) * LANES]
        qi_ref[:, c * LANES:(c + 1) * LANES] = _rope(x, cos, s1, s2).astype(BF16)
    x = ci_ref[:, HI * DI:HI * DI + LANES]
    lane = lax.broadcasted_iota(I32, x.shape, 1)
    is_ki = lane < DI
    kiw_ref[...] = _rope(x, jnp.where(is_ki, cos, 1.0), jnp.where(is_ki, s1, 0.0), jnp.where(is_ki, s2, 0.0))


def _attn_prep(p, row0, m, r, tabs, bd, qn, kn):
    nper = tabs[0].shape[0] // r
    tmap = lambda i: (i % nper, 0)
    assert row0 % r == 0 and m % r == 0 and tabs[0].shape[0] % r == 0
    base = row0 // r
    rows = lambda i: (i, 0)
    return pl.pallas_call(
        _attnprep_body,
        grid=(m // r,),
        in_specs=[pl.BlockSpec((r, 3 * WC), lambda i: (base + i, OFF_CQKV // (3 * WC))),
                  pl.BlockSpec((r, 384), lambda i: (base + i, OFF_CI // 384)),
                  pl.BlockSpec((r, LANES), tmap), pl.BlockSpec((r, LANES), tmap), pl.BlockSpec((r, LANES), tmap),
                  pl.BlockSpec((LANES, LANES), lambda i: (0, 0)),
                  pl.BlockSpec((1, LANES), lambda i: (0, 0)),
                  pl.BlockSpec((1, LANES), lambda i: (0, 0))],
        out_specs=[pl.BlockSpec((r, WC), rows), pl.BlockSpec((r, WC), rows), pl.BlockSpec((r, WC), rows),
                   pl.BlockSpec((r, HI * DI), rows), pl.BlockSpec((r, LANES), rows)],
        out_shape=[jax.ShapeDtypeStruct((m, WC), BF16), jax.ShapeDtypeStruct((m, WC), F32),
                   jax.ShapeDtypeStruct((m, WC), F32), jax.ShapeDtypeStruct((m, HI * DI), BF16),
                   jax.ShapeDtypeStruct((m, LANES), F32)],
        compiler_params=_cparams(("parallel",)),
    )(p, p, *tabs, bd, qn, kn)


def _attnprep_t_body(p_ref, ci_ref, cos_ref, s1_ref, s2_ref, bd_ref, qn_ref, kn_ref,
                     k_ref, v_ref, kiw_ref, kb_ref, kib_ref, qt_ref, qit_ref, wt_ref, vt_ref):
    cos, s1, s2 = cos_ref[...], s1_ref[...], s2_ref[...]
    bd = bd_ref[...]
    nch = WC // LANES
    r = p_ref.shape[0]
    for c in range(2 * nch):
        x = p_ref[:, c * LANES:(c + 1) * LANES]
        gain = (qn_ref if c < nch else kn_ref)[...]
        n = x * lax.rsqrt(_segsum(x * x, bd) * (1.0 / DH) + EPS) * gain
        rot = _rope(n, cos, s1, s2)
        if c < nch:
            qt_ref[c * LANES:(c + 1) * LANES, :] = (rot * (DH ** -0.5 * LOG2E)).T.astype(BF16)
        else:
            k_ref[:, (c - nch) * LANES:(c - nch + 1) * LANES] = rot
            kb_ref[:, (c - nch) * LANES:(c - nch + 1) * LANES] = rot.astype(BF16)
    v = p_ref[:, 2 * WC:3 * WC]
    v_ref[...] = v
    kt = vt_ref.shape[-1]
    for t in range(r // kt):
        for c in range(nch):
            vt_ref[t, c * LANES:(c + 1) * LANES, :] = v[t * kt:(t + 1) * kt, c * LANES:(c + 1) * LANES].T.astype(BF16)
    for c in range(HI * DI // LANES):
        x = ci_ref[:, c * LANES:(c + 1) * LANES]
        qit_ref[c * LANES:(c + 1) * LANES, :] = _rope(x, cos, s1, s2).T.astype(BF16)
    x = ci_ref[:, HI * DI:HI * DI + LANES]
    lane = lax.broadcasted_iota(I32, x.shape, 1)
    is_ki = lane < DI
    kiw = _rope(x, jnp.where(is_ki, cos, 1.0), jnp.where(is_ki, s1, 0.0), jnp.where(is_ki, s2, 0.0))
    kiw_ref[...] = kiw
    kib_ref[...] = kiw[:, 0:DI].astype(BF16)
    wt_ref[...] = kiw.T[LANE_W:LANE_W + SUBLANES, :]


def _attn_prep_t(p, bsz, t, r, kt, tabs, bd, qn, kn):
    nt = t // r
    m = bsz * t
    assert t % r == 0 and r % kt == 0 and kt % LANES == 0
    rows = lambda b, i: (b * nt + i, 0)
    tmap = lambda b, i: (i, 0)
    cst = lambda b, i: (0, 0)
    return pl.pallas_call(
        _attnprep_t_body,
        grid=(bsz, nt),
        in_specs=[pl.BlockSpec((r, 3 * WC), lambda b, i: (b * nt + i, OFF_CQKV // (3 * WC))),
                  pl.BlockSpec((r, 384), lambda b, i: (b * nt + i, OFF_CI // 384)),
                  pl.BlockSpec((r, LANES), tmap), pl.BlockSpec((r, LANES), tmap), pl.BlockSpec((r, LANES), tmap),
                  pl.BlockSpec((LANES, LANES), cst),
                  pl.BlockSpec((1, LANES), cst),
                  pl.BlockSpec((1, LANES), cst)],
        out_specs=[pl.BlockSpec((r, WC), rows), pl.BlockSpec((r, WC), rows), pl.BlockSpec((r, LANES), rows),
                   pl.BlockSpec((r, WC), rows), pl.BlockSpec((r, DI), rows),
                   pl.BlockSpec((None, WC, r), lambda b, i: (b, 0, i)),
                   pl.BlockSpec((None, HI * DI, r), lambda b, i: (b, 0, i)),
                   pl.BlockSpec((None, SUBLANES, r), lambda b, i: (b, 0, i)),
                   pl.BlockSpec((None, r // kt, WC, kt), lambda b, i: (b, i, 0, 0))],
        out_shape=[jax.ShapeDtypeStruct((m, WC), F32), jax.ShapeDtypeStruct((m, WC), F32),
                   jax.ShapeDtypeStruct((m, LANES), F32),
                   jax.ShapeDtypeStruct((m, WC), BF16), jax.ShapeDtypeStruct((m, DI), BF16),
                   jax.ShapeDtypeStruct((bsz, WC, t), BF16),
                   jax.ShapeDtypeStruct((bsz, HI * DI, t), BF16),
                   jax.ShapeDtypeStruct((bsz, SUBLANES, t), F32),
                   jax.ShapeDtypeStruct((bsz, t // kt, WC, kt), BF16)],
        compiler_params=_cparams(("parallel", "parallel")),
    )(p, p, *tabs, bd, qn, kn)


def _float_key(sc):
    bits = lax.bitcast_convert_type(sc, I32)
    return bits ^ ((bits >> 31) & 0x7FFFFFFF)


def _kth_largest_key(count_ge, shape, topk, bits_per_step=1):
    c0 = count_ge(jnp.zeros(shape, I32))
    base = jnp.where(c0 >= topk, 0, INT_MIN).astype(I32)
    above = jnp.where(c0 >= topk, 0.0, c0)

    def resolve(low, nbits, st):
        base, above = st
        best, failed = base, jnp.zeros(shape, jnp.bool_)
        for m in range(1, 2 ** nbits):
            cand = base + (jnp.int32(m) << low)
            cnt = count_ge(cand)
            ok = cnt >= topk
            best = jnp.where(ok, cand, best)
            above = jnp.where(~ok & ~failed, cnt, above)
            failed = failed | ~ok
        return best, above

    nfull, rest = divmod(31, bits_per_step)
    st = lax.fori_loop(0, nfull, lambda i, s: resolve(31 - (i + 1) * bits_per_step, bits_per_step, s), (base, above))
    return resolve(0, rest, st) if rest else st


def _select_tile(key, vstar, need, running, tri, visible):
    eq = key == vstar
    pref = _dot(eq.astype(BF16), tri)
    sel = (key > vstar) | (eq & (running + pref <= need))
    if visible is not None:
        sel = sel & visible
    return sel, running + pref[:, LANES - 1:LANES]


def _attn_prompt_body(qt_ref, qit_ref, wt_ref, k_ref, vt_ref, ki_ref, ltri_ref, o_ref,
                      key_scr, acc_scr, m_scr, l_scr, raw_scr, s_scr, p_scr, a_scr, bias_scr, *, tq, kt, topk):
    qblk = pl.program_id(1)
    nkb = (qblk * tq + tq + kt - 1) // kt
    qpos = qblk * tq + lax.broadcasted_iota(I32, (1, tq), 1)
    krow = lax.broadcasted_iota(I32, (kt, 1), 0)
    w = wt_ref[...] * ((HI * DI) ** -0.5)
    qit = qit_ref[...]
    qi_cat = jnp.concatenate([qit[h * DI:(h + 1) * DI, :] for h in range(HI)], axis=1)

    def key_tile(j):
        return pl.ds(pl.multiple_of(jnp.minimum(j, nkb - 1) * kt, kt), kt)

    def idx_dot(j):
        return _dot(ki_ref[key_tile(j), :], qi_cat)

    npairs_kt = (nkb + 1) // 2

    def fill_tile(j, cur):
        raw_scr[1 - cur] = idx_dot(j + 1)
        sc = None
        for h in range(HI):
            term = w[h:h + 1, :] * jnp.maximum(raw_scr[cur, :, h * tq:(h + 1) * tq], 0.0)
            sc = term if sc is None else sc + term
        key_scr[j] = jnp.where((krow + j * kt) <= qpos, _float_key(sc), INT_MIN)

    def fill(i, carry):
        fill_tile(2 * i, 0)
        fill_tile(2 * i + 1, 1)
        return carry

    raw_scr[0] = idx_dot(0)
    lax.fori_loop(0, npairs_kt, fill, 0)

    def count_ge(cand):
        def body(i, acc):
            for j in (2 * i, 2 * i + 1):
                ind = jnp.where(key_scr[j] >= cand, 1.0, 0.0)
                acc = acc + jnp.sum(ind.reshape(kt // COUNT_ROWS, COUNT_ROWS, tq), axis=0)
            return acc
        acc = lax.fori_loop(0, npairs_kt, body, jnp.zeros((COUNT_ROWS, tq), F32))
        return jnp.sum(acc, axis=0, keepdims=True)

    npair = HC // 2
    half = lax.broadcasted_iota(I32, (LANES, tq), 0) // DH
    q_pairs = []
    for c in range(npair):
        qc = qt_ref[c * LANES:(c + 1) * LANES, :].astype(F32)
        q_pairs.append(jnp.concatenate([jnp.where(half == 0, qc, 0.0), jnp.where(half == 1, qc, 0.0)],
                                       axis=1).astype(BF16))

    def store_scores(j, slot):
        kb = k_ref[key_tile(j), :]
        for c in range(npair):
            s_scr[slot, c] = _dot(kb[:, c * LANES:(c + 1) * LANES], q_pairs[c])

    vstar, above = _kth_largest_key(count_ge, (1, tq), topk)
    need = topk - above

    store_scores(0, 0)
    m_scr[...] = jnp.full_like(m_scr, NEG_BIG)
    l_scr[...] = jnp.zeros_like(l_scr)
    acc_scr[...] = jnp.zeros_like(acc_scr)
    ltri = ltri_ref[...]

    def store_bias(j, slot, running):
        key = key_scr[jnp.minimum(j, nkb - 1)]
        eq = key == vstar
        pref = _dot(ltri, eq.astype(BF16))
        sel = ((key > vstar) | (eq & (running + pref <= need))) & ((krow + j * kt) <= qpos)
        bias_scr[slot] = jnp.where(sel, 0.0, -jnp.inf)
        return running + pref[kt - 1:kt, :]

    def apply_pv(j, slot):
        vtb = vt_ref[jnp.clip(j, 0, nkb - 1)]
        for c in range(npair):
            pv = _dot(vtb[c * LANES:(c + 1) * LANES, :], p_scr[slot, c])
            for hh in range(2):
                h = 2 * c + hh
                acc_scr[h * DH:(h + 1) * DH, :] = (a_scr[slot, h:h + 1, :] * acc_scr[h * DH:(h + 1) * DH, :]
                                                   + pv[hh * DH:(hh + 1) * DH, hh * tq:(hh + 1) * tq])

    def sweep_tile(j, cur, running):
        nxt = 1 - cur
        apply_pv(j - 1, nxt)
        store_scores(j + 1, nxt)
        running = store_bias(j + 1, nxt, running)
        bias = bias_scr[cur]
        for c in range(npair):
            pc = []
            for hh in range(2):
                h = 2 * c + hh
                s = s_scr[cur, c, :, hh * tq:(hh + 1) * tq] + bias
                m_old = m_scr[h:h + 1, :]
                m_new = jnp.maximum(m_old, jnp.max(s, axis=0, keepdims=True))
                p = jnp.exp2(s - m_new)
                alpha = jnp.exp2(m_old - m_new)
                l_scr[h:h + 1, :] = alpha * l_scr[h:h + 1, :] + jnp.sum(p, axis=0, keepdims=True)
                m_scr[h:h + 1, :] = m_new
                a_scr[cur, h:h + 1, :] = alpha
                pc.append(p.astype(BF16))
            p_scr[cur, c] = jnp.concatenate(pc, axis=1)
        return running

    def sweep(i, running):
        return sweep_tile(2 * i + 1, 1, sweep_tile(2 * i, 0, running))

    p_scr[1] = jnp.zeros_like(p_scr[1])
    a_scr[1] = jnp.ones_like(a_scr[1])
    lax.fori_loop(0, npairs_kt, sweep, store_bias(0, 0, jnp.zeros((1, tq), F32)))
    apply_pv(2 * npairs_kt - 1, 1)
    for c in range(npair):
        ot = jnp.concatenate([acc_scr[h * DH:(h + 1) * DH, :] / l_scr[h:h + 1, :] for h in (2 * c, 2 * c + 1)],
                             axis=0)
        o_ref[:, c * LANES:(c + 1) * LANES] = ot.T


def _attn_prompt(qt, qit, wt, kbf, vt, kibf, ltri, bsz, s_len):
    tq = LANES
    kt = vt.shape[-1]
    nq = s_len // tq
    topk = min(TOPK_MAX, s_len // 4)
    assert s_len % kt == 0 and kt % tq == 0
    k3 = kbf.reshape(bsz, s_len, WC)
    ki3 = kibf.reshape(bsz, s_len, DI)
    return pl.pallas_call(
        functools.partial(_attn_prompt_body, tq=tq, kt=kt, topk=topk),
        grid=(bsz, nq),
        in_specs=[pl.BlockSpec((None, WC, tq), lambda b, i: (b, 0, i)),
                  pl.BlockSpec((None, HI * DI, tq), lambda b, i: (b, 0, i)),
                  pl.BlockSpec((None, SUBLANES, tq), lambda b, i: (b, 0, i)),
                  pl.BlockSpec((None, s_len, WC), lambda b, i: (b, 0, 0)),
                  pl.BlockSpec((None, s_len // kt, WC, kt), lambda b, i: (b, 0, 0, 0)),
                  pl.BlockSpec((None, s_len, DI), lambda b, i: (b, 0, 0)),
                  pl.BlockSpec((kt, kt), lambda b, i: (0, 0))],
        out_specs=pl.BlockSpec((tq, WC), lambda b, i: (b * nq + i, 0)),
        out_shape=jax.ShapeDtypeStruct((bsz * s_len, WC), F32),
        scratch_shapes=[pltpu.VMEM((s_len // kt, kt, tq), I32),
                        pltpu.VMEM((WC, tq), F32),
                        pltpu.VMEM((SUBLANES, tq), F32),
                        pltpu.VMEM((SUBLANES, tq), F32),
                        pltpu.VMEM((2, kt, HI * tq), F32),
                        pltpu.VMEM((2, HC // 2, kt, 2 * tq), F32),
                        pltpu.VMEM((2, HC // 2, kt, 2 * tq), BF16),
                        pltpu.VMEM((2, SUBLANES, tq), F32),
                        pltpu.VMEM((2, kt, tq), F32)],
        compiler_params=_cparams(("parallel", "arbitrary")),
    )(qt, qit, wt, k3, vt, ki3, ltri)


def _attn_sample_body(pt_ref, q_ref, qi_ref, kiw_ref, knew_ref, vnew_ref, tri_ref, *refs, t, npg, topk):
    kidx_refs, k_refs, v_refs = refs[0:npg], refs[npg:2 * npg], refs[2 * npg:3 * npg]
    o_ref, s_scr, kipad_scr, kpad_scr, vpad_scr = refs[3 * npg:]
    nt = npg + 1
    qi = qi_ref[...]
    qi_stack = jnp.concatenate([qi[:, h * DI:(h + 1) * DI] for h in range(HI)], axis=0)
    w = kiw_ref[:, LANE_W:LANE_W + HI] * ((HI * DI) ** -0.5)
    w_stack = jnp.concatenate([w[:, h:h + 1] for h in range(HI)], axis=0)

    def score_keys(kib):
        rel = jnp.maximum(_dot_nt(qi_stack, kib), 0.0) * w_stack
        sc = rel[0:t]
        for h in range(1, HI):
            sc = sc + rel[h * t:(h + 1) * t]
        return _float_key(sc)

    keys = [score_keys(kidx_refs[j][...].astype(BF16)) for j in range(npg)]
    kipad_scr[...] = jnp.zeros_like(kipad_scr)
    kipad_scr[0:t, :] = kiw_ref[:, 0:DI]
    vis_new = lax.broadcasted_iota(I32, (t, LANES), 1) <= lax.broadcasted_iota(I32, (t, LANES), 0)
    keys.append(jnp.where(vis_new, score_keys(kipad_scr[...].astype(BF16)), INT_MIN))

    def count_ge(cand):
        terms = [jnp.where(key >= cand, 1.0, 0.0) for key in keys]
        while len(terms) > 1:
            terms = [a + b for a, b in zip(terms[0::2], terms[1::2])] + terms[len(terms) & ~1:]
        return jnp.sum(terms[0], axis=1, keepdims=True)

    vstar, above = _kth_largest_key(count_ge, (t, 1), topk, bits_per_step=3)
    need = topk - above

    kpad_scr[...] = jnp.zeros_like(kpad_scr)
    vpad_scr[...] = jnp.zeros_like(vpad_scr)
    kpad_scr[0:t, :] = knew_ref[...]
    vpad_scr[0:t, :] = vnew_ref[...]
    head_of_lane = lax.broadcasted_iota(I32, (t, WC), 1) // DH
    q = q_ref[...].astype(F32)
    q_stack = jnp.concatenate([jnp.where(head_of_lane == h, q, 0.0) for h in range(HC)], axis=0).astype(BF16)
    tri = tri_ref[...]
    running = jnp.zeros((t, 1), F32)
    m_acc = jnp.full((HC * t, LANES), NEG_BIG, F32)
    for j in range(nt):
        sel, running = _select_tile(keys[j], vstar, need, running, tri, vis_new if j == npg else None)
        bias = jnp.where(sel, 0.0, -jnp.inf)
        kb = (k_refs[j] if j < npg else kpad_scr)[...].astype(BF16)
        s = _dot_nt(q_stack, kb) + jnp.concatenate([bias] * HC, axis=0)
        s_scr[j] = s
        m_acc = jnp.maximum(m_acc, s)
    m = jnp.max(m_acc, axis=1, keepdims=True)
    l_acc = jnp.zeros((HC * t, LANES), F32)
    acc = jnp.zeros((HC * t, WC), F32)
    for j in range(nt):
        p = jnp.exp(s_scr[j] - m)
        l_acc = l_acc + p
        vb = (v_refs[j] if j < npg else vpad_scr)[...].astype(BF16)
        acc = acc + _dot(p.astype(BF16), vb)
    o = acc / jnp.sum(l_acc, axis=1, keepdims=True)
    out = jnp.where(head_of_lane == 0, o[0:t], 0.0)
    for h in range(1, HC):
        out = out + jnp.where(head_of_lane == h, o[h * t:(h + 1) * t], 0.0)
    o_ref[...] = out


def _attn_sample(page_table, q, qi, kiw, knew, vnew, tri, cache_kidx, cache_k, cache_v, layer, bsz, t):
    npg = page_table.shape[1]
    page = cache_k.shape[2]
    assert page == LANES
    topk = min(TOPK_MAX, (npg * page + t) // 4)
    row = lambda b, pt: (b, 0)

    def page_map(j):
        return lambda b, pt: (layer, pt[b, j], 0, 0)

    in_specs = [pl.BlockSpec((t, WC), row), pl.BlockSpec((t, HI * DI), row), pl.BlockSpec((t, LANES), row),
                pl.BlockSpec((t, WC), row), pl.BlockSpec((t, WC), row),
                pl.BlockSpec((LANES, LANES), lambda b, pt: (0, 0))]
    in_specs += [pl.BlockSpec((None, None, page, DI), page_map(j)) for j in range(npg)]
    in_specs += [pl.BlockSpec((None, None, page, WC), page_map(j)) for j in range(npg)]
    in_specs += [pl.BlockSpec((None, None, page, WC), page_map(j)) for j in range(npg)]
    grid_spec = pltpu.PrefetchScalarGridSpec(
        num_scalar_prefetch=1,
        grid=(bsz,),
        in_specs=in_specs,
        out_specs=pl.BlockSpec((t, WC), row),
        scratch_shapes=[pltpu.VMEM((npg + 1, HC * t, LANES), F32), pltpu.VMEM((LANES, DI), F32),
                        pltpu.VMEM((LANES, WC), F32), pltpu.VMEM((LANES, WC), F32)])
    return pl.pallas_call(
        functools.partial(_attn_sample_body, t=t, npg=npg, topk=topk),
        grid_spec=grid_spec,
        out_shape=jax.ShapeDtypeStruct((bsz * t, WC), F32),
        compiler_params=_cparams(("parallel",)),
    )(page_table, q, qi, kiw, knew, vnew, tri,
      *([cache_kidx] * npg), *([cache_k] * npg), *([cache_v] * npg))


def _rope_tables(pos):
    half = ROPE_DIM // 2
    inv_freq = ROPE_THETA ** (-jnp.arange(half, dtype=F32) / half)
    ang = pos.astype(F32)[:, None] * inv_freq[None, :]
    cos, sin = jnp.cos(ang), jnp.sin(ang)
    n = pos.shape[0]
    one = jnp.ones((n, DH - ROPE_DIM), F32)
    zero_h = jnp.zeros((n, half), F32)
    zero_r = jnp.zeros((n, DH - ROPE_DIM), F32)
    c64 = jnp.concatenate([cos, cos, one], axis=1)
    s1 = jnp.concatenate([-sin, zero_h, zero_r], axis=1)
    s2 = jnp.concatenate([zero_h, sin, zero_r], axis=1)
    rep = LANES // DH
    return tuple(jnp.tile(a, (1, rep)) for a in (c64, s1, s2))


def _reorder_w_in(w_in):
    sizes = (WA, WA, WA, QKV_B, WB, HB, HB, 3 * WC, HI * DI, DI, HI)
    offs = [0]
    for s in sizes:
        offs.append(offs[-1] + s)
    seg = lambda i: w_in[:, :, offs[i]:offs[i + 1]]
    pad = jnp.zeros(w_in.shape[:2] + (LANES - DI - HI - 2 * HB,), w_in.dtype)
    order = [seg(3), seg(7), seg(0), seg(1), seg(2), seg(4), seg(8), seg(9), seg(10), seg(5), seg(6), pad]
    return jnp.concatenate(order, axis=-1).astype(BF16)


def _pad_state(prev):
    return jnp.pad(prev, ((0, 0), (SUBLANES - prev.shape[1], 0), (0, 0)))


def kernel(x_prompt, x_sample, state_conv_a, state_gdn_conv, state_gdn, cache_k, cache_v, cache_kidx, page_table, ffn1_norm, ffn1_w_gate, ffn1_w_up, ffn1_w_down, mix_norm, w_in, conv_a_w, gdn_conv_w, gdn_a_log, gdn_dt_bias, gdn_out_norm, attn_q_norm, attn_k_norm, w_out, ffn2_norm, ffn2_w_gate, ffn2_w_up, ffn2_w_down):
    depth = w_in.shape[0]
    bp, tp, d = x_prompt.shape
    bs, ts, _ = x_sample.shape
    mp, ms = bp * tp, bs * ts
    npool, page = cache_k.shape[1], cache_k.shape[2]
    past = page_table.shape[1] * page
    assert ts == SUBLANES and tp % LANES == 0 and mp % 256 == 0

    x = jnp.concatenate([x_prompt.reshape(mp, d), x_sample.reshape(ms, d)], axis=0)
    w_in_r = _reorder_w_in(w_in)
    bf = lambda a: a.astype(BF16)
    f1g, f1u, f1d = bf(ffn1_w_gate), bf(ffn1_w_up), bf(ffn1_w_down)
    f2g, f2u, f2d = bf(ffn2_w_gate), bf(ffn2_w_up), bf(ffn2_w_down)
    w_out_b = bf(w_out)
    ck = cache_k.reshape(depth, npool, page, WC)
    cv = cache_v.reshape(depth, npool, page, WC)

    li = jnp.arange(LANES)
    bd = ((li[:, None] // DH) == (li[None, :] // DH)).astype(BF16)
    tri = (li[:, None] <= li[None, :]).astype(BF16)
    kt = r_p = _pick(tp, (256, 128))
    lk = jnp.arange(kt)
    ltri = (lk[None, :] <= lk[:, None]).astype(BF16)
    r_s = _pick(ms, (128, 64, 32, 16, 8))
    tabs_p = _rope_tables(jnp.arange(tp))
    tabs_s = _rope_tables(jnp.tile(past + jnp.arange(ts), r_s // ts))
    zero_a = jnp.zeros((bp, SUBLANES, WA), F32)
    zero_b = jnp.zeros((bp, SUBLANES, QKV_B), F32)
    zero_s = jnp.zeros((bp, HB, DKB, DVB), F32)
    lane_pad = lambda v, off: jnp.zeros((1, LANES), F32).at[0, off:off + v.shape[0]].set(v)

    outs = [[] for _ in range(12)]
    for l in range(depth):
        x = _ffn(x, ffn1_norm[l][None], f1g[l], f1u[l], f1d[l])
        p = _inproj(x, mix_norm[l][None], w_in_r[l])
        avec = lane_pad(jnp.exp(gdn_a_log[l].astype(F32)), LANE_GA)
        dtb = lane_pad(gdn_dt_bias[l].astype(F32), LANE_GA)
        gn = gdn_out_norm[l][None].astype(F32)
        qn = jnp.tile(attn_q_norm[l].astype(F32), LANES // DH)[None]
        kn = jnp.tile(attn_k_norm[l].astype(F32), LANES // DH)[None]

        ya, yb, yc = [], [], []
        groups = ((0, bp, tp, zero_a, zero_b, zero_s), (mp, bs, ts, _pad_state(state_conv_a[l]),
                                                       _pad_state(state_gdn_conv[l]), state_gdn[l]))
        for gi, (row0, bsz, t, prev_a, prev_b, s0) in enumerate(groups):
            m = bsz * t
            y_a, st_a = _mixer_a(p, row0, bsz, t, conv_a_w[l], prev_a)
            qg, kg, vg, gbg, st_b = _gdn_prep(p, row0, bsz, t, gdn_conv_w[l], prev_b, bd, avec, dtb)
            y_b, s_new = _gdn(qg, kg, vg, gbg, p, row0, bsz, t, s0, gn)
            if gi == 0:
                kc, vc, kiw, kbf, kibf, qt, qit, wt, vt = _attn_prep_t(p, bsz, t, r_p, kt, tabs_p, bd, qn, kn)
                y_c = _attn_prompt(qt, qit, wt, kbf, vt, kibf, ltri, bsz, t)
            else:
                qc, kc, vc, qi, kiw = _attn_prep(p, row0, m, r_s, tabs_s, bd, qn, kn)
                y_c = _attn_sample(page_table, qc, qi, kiw, kc, vc, tri, cache_kidx, ck, cv, l, bsz, t)
            ya.append(y_a)
            yb.append(y_b)
            yc.append(y_c)
            res = (st_a[:, SUBLANES - (CONV_A - 1):], st_b[:, SUBLANES - (GDN_CONV - 1):], s_new,
                   kc.reshape(bsz, t, HC, DH), vc.reshape(bsz, t, HC, DH), kiw[:, :DI].reshape(bsz, t, DI))
            for i, r in enumerate(res):
                outs[gi * 6 + i].append(r)
        x = _outproj(x, jnp.concatenate(ya), jnp.concatenate(yb), jnp.concatenate(yc), w_out_b[l])
        x = _ffn(x, ffn2_norm[l][None], f2g[l], f2u[l], f2d[l])

    y_prompt = x[:mp].reshape(bp, tp, d)
    y_sample = x[mp:].reshape(bs, ts, d)
    return (y_prompt, y_sample) + tuple(jnp.stack(o) for o in outs)
```
